```python
import jax, jax.numpy as jnp
from jax import lax
import numpy as np

D_MODEL = 1024
BATCH = 2
SEQ = 8192
DEPTH = 2

HEAD_DIM = 64
N_NSA_HEADS = 8
N_NSA_KV = 2
NSA_REP = N_NSA_HEADS // N_NSA_KV
N_FOX_HEADS = 4
N_SB_HEADS = 4
D_NSA = N_NSA_HEADS * HEAD_DIM
D_FOX = N_FOX_HEADS * HEAD_DIM
D_SB = N_SB_HEADS * HEAD_DIM
D_MIX = D_NSA + D_FOX + D_SB
N_NSA_BRANCH = 3
CMP_LEN = 32
CMP_STRIDE = 16
CMP_HIDDEN = 128
SEL_LEN = 64
SEL_TOPK = 16
WINDOW = 512
Q_BLOCK = 128
N_EXPERTS = 32
TOP_K = 4
D_EXPERT = D_MODEL
SWIGLU_LIMIT = 7.0
SWIGLU_ALPHA = 1.702
MOE_BLOCK = 256
DEEPNORM_ALPHA = (2 * DEPTH) ** 0.25
DEEPNORM_BETA = (8 * DEPTH) ** -0.25
LN_EPS = 1e-5
RMS_EPS = 1e-6
NEG_INF = -1e30
FORGET_BIAS_INIT = 3.0

COL_NSA_Q = D_NSA
COL_NSA_KV = N_NSA_BRANCH * 2 * N_NSA_KV * HEAD_DIM
COL_NSA_GATE = N_NSA_BRANCH * N_NSA_HEADS
COL_FOX_QKV = 3 * D_FOX
COL_FOX_F = N_FOX_HEADS
COL_SB_QKV = 3 * D_SB
OFF_NSA_KV = COL_NSA_Q
OFF_NSA_GATE = OFF_NSA_KV + COL_NSA_KV
OFF_FOX_QKV = OFF_NSA_GATE + COL_NSA_GATE
OFF_FOX_F = OFF_FOX_QKV + COL_FOX_QKV
OFF_SB_QKV = OFF_FOX_F + COL_FOX_F
D_IN = OFF_SB_QKV + COL_SB_QKV

kernel_name = "hybrid_nsa_fox_stickbreak_moe_deepnorm"

F32 = jnp.float32


def layer_norm(x, g, b):
    xf = x.astype(F32)
    mu = jnp.mean(xf, axis=-1, keepdims=True)
    var = jnp.mean(jnp.square(xf - mu), axis=-1, keepdims=True)
    return ((xf - mu) * lax.rsqrt(var + LN_EPS) * g + b).astype(x.dtype)


def rms_norm(x, g):
    xf = x.astype(F32)
    return xf * lax.rsqrt(jnp.mean(jnp.square(xf), axis=-1, keepdims=True) + RMS_EPS) * g


def masked_softmax(s, mask):
    s = jnp.where(mask, s.astype(F32), NEG_INF)
    p = jax.nn.softmax(s, axis=-1)
    return p * jnp.any(mask, axis=-1, keepdims=True)


def alibi_slopes(n):
    return jnp.exp2(-8.0 * jnp.arange(1, n + 1, dtype=F32) / n)


def blocks_to_seq(o):
    nqb, b, h, qb, d = o.shape
    return o.transpose(1, 0, 3, 2, 4).reshape(b, nqb * qb, h * d)


def nsa_attention(q, k_br, v_br, gates, pos_k, pos_v, w1_k, w2_k, w1_v, w2_v):
    B, H, S, dh = q.shape
    G, R = N_NSA_KV, NSA_REP
    scale = dh ** -0.5
    kc, ks, kw = k_br[0], k_br[1], k_br[2]
    vc, vs, vw = v_br[0], v_br[1], v_br[2]
    n_cmp = (S - CMP_LEN) // CMP_STRIDE + 1
    cmp_idx = jnp.arange(n_cmp)[:, None] * CMP_STRIDE + jnp.arange(CMP_LEN)[None, :]
    cmp_end = cmp_idx[:, -1]

    def compress(kv, pos, w1, w2):
        blk = kv[:, :, cmp_idx] + pos
        hid = jax.nn.gelu(blk.reshape(B, G, n_cmp, CMP_LEN * dh) @ w1)
        return hid @ w2

    k_cmp = compress(kc, pos_k, w1_k, w2_k)
    v_cmp = compress(vc, pos_v, w1_v, w2_v)
    n_blk = S // SEL_LEN
    n_sel = min(SEL_TOPK, n_blk)
    overlap = jax.nn.one_hot(cmp_idx // SEL_LEN, n_blk, dtype=F32).mean(axis=1)
    ks_blk = ks.reshape(B, G, n_blk, SEL_LEN, dh)
    vs_blk = vs.reshape(B, G, n_blk, SEL_LEN, dh)
    kw_pad = jnp.pad(kw, ((0, 0), (0, 0), (WINDOW, 0), (0, 0)))
    vw_pad = jnp.pad(vw, ((0, 0), (0, 0), (WINDOW, 0), (0, 0)))
    slopes = alibi_slopes(H).reshape(1, G, R, 1, 1)
    b_ix = jnp.arange(B)[:, None, None, None]
    g_ix = jnp.arange(G)[None, :, None, None]
    blk_ids = jnp.arange(n_blk)
    sel_off = jnp.arange(SEL_LEN)
    win_off = jnp.arange(Q_BLOCK + WINDOW)

    def block(i):
        q0 = i * Q_BLOCK
        t = q0 + jnp.arange(Q_BLOCK)
        tf = t.astype(F32)
        qb = lax.dynamic_slice_in_dim(q, q0, Q_BLOCK, axis=2).reshape(B, G, R, Q_BLOCK, dh)
        gb = lax.dynamic_slice_in_dim(gates, q0, Q_BLOCK, axis=2).reshape(B, G, R, Q_BLOCK, N_NSA_BRANCH)
        s_c = jnp.einsum('bgrqd,bgnd->bgrqn', qb, k_cmp).astype(F32) * scale \
            - slopes * (tf[:, None] - cmp_end[None, :].astype(F32))
        p_c = masked_softmax(s_c, cmp_end[None, :] <= t[:, None])
        o_c = jnp.einsum('bgrqn,bgnd->bgrqd', p_c, v_cmp)
        imp = jnp.einsum('bgrqn,nj->bgqj', p_c, overlap)
        cur = (t // SEL_LEN)[:, None]
        valid = blk_ids[None, :] <= cur
        forced = valid & ((blk_ids[None, :] == 0) | (blk_ids[None, :] >= cur - 1))
        imp = jnp.where(forced, jnp.inf, jnp.where(valid, imp, -jnp.inf))
        _, sel = lax.top_k(imp, n_sel)
        k_sel = ks_blk[b_ix, g_ix, sel].reshape(B, G, Q_BLOCK, n_sel * SEL_LEN, dh)
        v_sel = vs_blk[b_ix, g_ix, sel].reshape(B, G, Q_BLOCK, n_sel * SEL_LEN, dh)
        pos_sel = (sel[..., None] * SEL_LEN + sel_off).reshape(B, G, Q_BLOCK, n_sel * SEL_LEN)
        dist_sel = t[None, None, :, None] - pos_sel
        s_s = jnp.einsum('bgrqd,bgqkd->bgrqk', qb, k_sel).astype(F32) * scale \
            - slopes * dist_sel[:, :, None].astype(F32)
        p_s = masked_softmax(s_s, (dist_sel >= 0)[:, :, None])
        o_s = jnp.einsum('bgrqk,bgqkd->bgrqd', p_s, v_sel)
        k_win = lax.dynamic_slice_in_dim(kw_pad, q0, Q_BLOCK + WINDOW, axis=2)
        v_win = lax.dynamic_slice_in_dim(vw_pad, q0, Q_BLOCK + WINDOW, axis=2)
        pos_w = q0 - WINDOW + win_off
        dist_w = t[:, None] - pos_w[None, :]
        mask_w = (pos_w[None, :] >= 0) & (dist_w >= 0) & (dist_w < WINDOW)
        s_w = jnp.einsum('bgrqd,bgkd->bgrqk', qb, k_win).astype(F32) * scale \
            - slopes * dist_w.astype(F32)
        p_w = masked_softmax(s_w, mask_w)
        o_w = jnp.einsum('bgrqk,bgkd->bgrqd', p_w, v_win)
        o = gb[..., 0:1] * o_c + gb[..., 1:2] * o_s + gb[..., 2:3] * o_w
        return o.reshape(B, H, Q_BLOCK, dh)

    return blocks_to_seq(lax.map(block, jnp.arange(S // Q_BLOCK)))


def forgetting_attention(q, k, v, log_f):
    B, H, S, dh = q.shape
    scale = dh ** -0.5
    c = jnp.cumsum(log_f, axis=-1)
    key_pos = jnp.arange(S)

    def block(i):
        q0 = i * Q_BLOCK
        t = q0 + jnp.arange(Q_BLOCK)
        qb = lax.dynamic_slice_in_dim(q, q0, Q_BLOCK, axis=2)
        cb = lax.dynamic_slice_in_dim(c, q0, Q_BLOCK, axis=2)
        s = jnp.einsum('bhqd,bhkd->bhqk', qb, k).astype(F32) * scale \
            + cb[..., :, None] - c[..., None, :]
        p = masked_softmax(s, key_pos[None, :] <= t[:, None])
        return jnp.einsum('bhqk,bhkd->bhqd', p, v)

    return blocks_to_seq(lax.map(block, jnp.arange(S // Q_BLOCK)))


def stick_breaking_attention(q, k, v):
    B, H, S, dh = q.shape
    scale = dh ** -0.5
    key_pos = jnp.arange(S)

    def block(i):
        q0 = i * Q_BLOCK
        t = q0 + jnp.arange(Q_BLOCK)
        qb = lax.dynamic_slice_in_dim(q, q0, Q_BLOCK, axis=2)
        z = jnp.einsum('bhqd,bhkd->bhqk', qb, k).astype(F32) * scale
        strict = key_pos[None, :] < t[:, None]
        log_beta = jax.nn.log_sigmoid(z)
        log_keep = jnp.where(strict, jax.nn.log_sigmoid(-z), 0.0)
        shifted = jnp.concatenate([log_keep[..., 1:], jnp.zeros_like(log_keep[..., :1])], axis=-1)
        later = lax.cumsum(shifted, axis=3, reverse=True)
        a = jnp.where(strict, jnp.exp(log_beta + later), 0.0)
        return jnp.einsum('bhqk,bhkd->bhqd', a, v)

    return blocks_to_seq(lax.map(block, jnp.arange(S // Q_BLOCK)))


def hybrid_mixer(x, w_in, b_in, pos_k, pos_v, w1_k, w2_k, w1_v, w2_v, g_out, w_out):
    B, S, _ = x.shape
    proj = x @ w_in + b_in
    q_nsa = proj[..., :OFF_NSA_KV].reshape(B, S, N_NSA_HEADS, HEAD_DIM).transpose(0, 2, 1, 3)
    kv_nsa = proj[..., OFF_NSA_KV:OFF_NSA_GATE].reshape(
        B, S, N_NSA_BRANCH, 2, N_NSA_KV, HEAD_DIM).transpose(2, 3, 0, 4, 1, 5)
    gates = jax.nn.sigmoid(proj[..., OFF_NSA_GATE:OFF_FOX_QKV].reshape(
        B, S, N_NSA_BRANCH, N_NSA_HEADS)).transpose(0, 3, 1, 2)
    fox = proj[..., OFF_FOX_QKV:OFF_FOX_F].reshape(B, S, 3, N_FOX_HEADS, HEAD_DIM).transpose(2, 0, 3, 1, 4)
    log_f = jax.nn.log_sigmoid(proj[..., OFF_FOX_F:OFF_SB_QKV].astype(F32)).transpose(0, 2, 1)
    sb = proj[..., OFF_SB_QKV:].reshape(B, S, 3, N_SB_HEADS, HEAD_DIM).transpose(2, 0, 3, 1, 4)
    o_nsa = nsa_attention(q_nsa, kv_nsa[:, 0], kv_nsa[:, 1], gates, pos_k, pos_v, w1_k, w2_k, w1_v, w2_v)
    o_fox = forgetting_attention(fox[0], fox[1], fox[2], log_f)
    o_sb = stick_breaking_attention(sb[0], sb[1], sb[2])
    o = jnp.concatenate([
        rms_norm(o_nsa, g_out[:D_NSA]),
        rms_norm(o_fox, g_out[D_NSA:D_NSA + D_FOX]),
        rms_norm(o_sb, g_out[D_NSA + D_FOX:]),
    ], axis=-1).astype(x.dtype)
    return o @ w_out


def moe_ffn(h, router_w, router_b, w_gu, b_gu, w_dn, b_dn):
    B, S, D = h.shape
    T = B * S
    M = T * TOP_K
    xt = h.reshape(T, D)
    logits = (xt @ router_w + router_b).astype(F32)
    top_logit, top_e = lax.top_k(logits, TOP_K)
    top_w = jax.nn.softmax(top_logit, axis=-1)
    flat_e = top_e.reshape(M)
    flat_tok = jnp.repeat(jnp.arange(T, dtype=jnp.int32), TOP_K)
    flat_w = top_w.reshape(M)
    order = jnp.argsort(flat_e)
    sorted_e = flat_e[order]
    counts = jnp.bincount(flat_e, length=N_EXPERTS)
    start = jnp.cumsum(counts) - counts
    padded = (counts + MOE_BLOCK - 1) // MOE_BLOCK * MOE_BLOCK
    pad_end = jnp.cumsum(padded)
    pad_start = pad_end - padded
    dest = pad_start[sorted_e] + jnp.arange(M) - start[sorted_e]
    n_blocks = -(-M // MOE_BLOCK) + N_EXPERTS
    P = n_blocks * MOE_BLOCK
    buf_tok = jnp.zeros((P,), jnp.int32).at[dest].set(flat_tok[order])
    buf_w = jnp.zeros((P,), F32).at[dest].set(flat_w[order])
    blk_e = jnp.minimum(jnp.searchsorted(pad_end, jnp.arange(n_blocks) * MOE_BLOCK, side='right'),
                        N_EXPERTS - 1)

    def expert_block(args):
        e, tok, w = args
        xb = xt[tok]
        gu = xb @ w_gu[e] + b_gu[e]
        gate = jnp.minimum(gu[:, :D_EXPERT], SWIGLU_LIMIT)
        up = jnp.clip(gu[:, D_EXPERT:], -SWIGLU_LIMIT, SWIGLU_LIMIT)
        act = gate * jax.nn.sigmoid(SWIGLU_ALPHA * gate) * (up + 1.0)
        y = act @ w_dn[e] + b_dn[e]
        return (y * w[:, None]).astype(h.dtype)

    ys = lax.map(expert_block, (blk_e, buf_tok.reshape(n_blocks, MOE_BLOCK), buf_w.reshape(n_blocks, MOE_BLOCK)))
    out = jnp.zeros((T, D), h.dtype).at[buf_tok].add(ys.reshape(P, D))
    return out.reshape(B, S, D)


def setup_inputs(seed: int = 0) -> dict:
    key = jax.random.key(seed)
    ks = jax.random.split(key, 24)
    L = DEPTH

    def nrm(k, shape, scale):
        return jax.random.normal(k, shape, F32) * scale

    x = nrm(ks[0], (BATCH, SEQ, D_MODEL), 1.0)
    w_in = nrm(ks[1], (L, D_MODEL, D_IN), D_MODEL ** -0.5)
    b_in = nrm(ks[2], (L, D_IN), 0.01).at[:, OFF_FOX_F:OFF_SB_QKV].add(FORGET_BIAS_INIT)
    cmp_pos_k = nrm(ks[3], (L, CMP_LEN, HEAD_DIM), 0.1)
    cmp_pos_v = nrm(ks[4], (L, CMP_LEN, HEAD_DIM), 0.1)
    cmp_w1_k = nrm(ks[5], (L, CMP_LEN * HEAD_DIM, CMP_HIDDEN), (CMP_LEN * HEAD_DIM) ** -0.5)
    cmp_w2_k = nrm(ks[6], (L, CMP_HIDDEN, HEAD_DIM), CMP_HIDDEN ** -0.5)
    cmp_w1_v = nrm(ks[7], (L, CMP_LEN * HEAD_DIM, CMP_HIDDEN), (CMP_LEN * HEAD_DIM) ** -0.5)
    cmp_w2_v = nrm(ks[8], (L, CMP_HIDDEN, HEAD_DIM), CMP_HIDDEN ** -0.5)
    g_out = 1.0 + nrm(ks[9], (L, D_MIX), 0.02)
    w_out = nrm(ks[10], (L, D_MIX, D_MODEL), D_MIX ** -0.5 * DEEPNORM_BETA)
    ln1_g = 1.0 + nrm(ks[11], (L, D_MODEL), 0.02)
    ln1_b = nrm(ks[12], (L, D_MODEL), 0.02)
    router_w = nrm(ks[13], (L, D_MODEL, N_EXPERTS), D_MODEL ** -0.5)
    router_b = nrm(ks[14], (L, N_EXPERTS), 0.01)
    w_gate_up = nrm(ks[15], (L, N_EXPERTS, D_MODEL, 2 * D_EXPERT), D_MODEL ** -0.5)
    b_gate_up = nrm(ks[16], (L, N_EXPERTS, 2 * D_EXPERT), 0.01)
    w_down = nrm(ks[17], (L, N_EXPERTS, D_EXPERT, D_MODEL), D_EXPERT ** -0.5 * DEEPNORM_BETA)
    b_down = nrm(ks[18], (L, N_EXPERTS, D_MODEL), 0.01)
    ln2_g = 1.0 + nrm(ks[19], (L, D_MODEL), 0.02)
    ln2_b = nrm(ks[20], (L, D_MODEL), 0.02)
    return {'x': x, 'w_in': w_in, 'b_in': b_in, 'cmp_pos_k': cmp_pos_k, 'cmp_pos_v': cmp_pos_v,
            'cmp_w1_k': cmp_w1_k, 'cmp_w2_k': cmp_w2_k, 'cmp_w1_v': cmp_w1_v, 'cmp_w2_v': cmp_w2_v,
            'g_out': g_out, 'w_out': w_out, 'ln1_g': ln1_g, 'ln1_b': ln1_b,
            'router_w': router_w, 'router_b': router_b, 'w_gate_up': w_gate_up, 'b_gate_up': b_gate_up,
            'w_down': w_down, 'b_down': b_down, 'ln2_g': ln2_g, 'ln2_b': ln2_b}


def reference(x, w_in, b_in, cmp_pos_k, cmp_pos_v, cmp_w1_k, cmp_w2_k, cmp_w1_v, cmp_w2_v,
              g_out, w_out, ln1_g, ln1_b, router_w, router_b, w_gate_up, b_gate_up,
              w_down, b_down, ln2_g, ln2_b):
    for l in range(DEPTH):
        mix = hybrid_mixer(x, w_in[l], b_in[l], cmp_pos_k[l], cmp_pos_v[l], cmp_w1_k[l], cmp_w2_k[l],
                           cmp_w1_v[l], cmp_w2_v[l], g_out[l], w_out[l])
        x = layer_norm(DEEPNORM_ALPHA * x + mix, ln1_g[l], ln1_b[l])
        ffn = moe_ffn(x, router_w[l], router_b[l], w_gate_up[l], b_gate_up[l], w_down[l], b_down[l])
        x = layer_norm(DEEPNORM_ALPHA * x + ffn, ln2_g[l], ln2_b[l])
    return x
```

```python
import functools

import jax
import jax.numpy as jnp
import numpy as np
from jax import lax
from jax.experimental import pallas as pl
from jax.experimental.pallas import tpu as pltpu

F32 = jnp.float32
BF16 = jnp.bfloat16

HEAD_DIM = 64
N_NSA_HEADS = 8
N_NSA_KV = 2
NSA_REP = 4
N_FOX_HEADS = 4
N_SB_HEADS = 4
D_NSA = 512
D_FOX = 256
D_SB = 256
CMP_LEN = 32
CMP_STRIDE = 16
CMP_HIDDEN = 128
SEL_LEN = 64
SEL_TOPK = 16
WINDOW = 512
Q_BLOCK = 128
N_EXPERTS = 32
TOP_K = 4
SWIGLU_LIMIT = 7.0
SWIGLU_ALPHA = 1.702
LN_EPS = 1e-5
RMS_EPS = 1e-6
NEG = -1e30
BIG = 3e38
SB_CUTOFF = -110.0

D_IN_PAD = 3072
OFF_KV = 512
OFF_FOX = 1280
OFF_SB = 2048
OFF_GATE = 2816
OFF_LOGF = 2840

VMEM_LIMIT = 52 * 1024 * 1024


def _cparams(sem, vmem=VMEM_LIMIT):
    return pltpu.CompilerParams(dimension_semantics=sem, vmem_limit_bytes=vmem)


def _dot(a, b):
    return jnp.dot(a, b, preferred_element_type=F32)


def _dot_nt(a, b):
    return lax.dot_general(a, b, (((1,), (1,)), ((), ())), preferred_element_type=F32)


def _split_bf16(x):
    hi = x.astype(BF16)
    lo = (x - hi.astype(F32)).astype(BF16)
    return hi, lo


def _sigmoid(x):
    return 1.0 / (1.0 + jnp.exp(-x))


def _proj_kernel(x_ref, w_ref, b_ref, o_ref):
    o_ref[...] = _dot(x_ref[...].astype(BF16), w_ref[...]) + b_ref[...]


def _proj(x2d, w_bf, b_row, tm=512):
    T, D = x2d.shape
    N = w_bf.shape[1]
    return pl.pallas_call(
        _proj_kernel,
        grid=(T // tm,),
        in_specs=[pl.BlockSpec((tm, D), lambda i: (i, 0)),
                  pl.BlockSpec((D, N), lambda i: (0, 0)),
                  pl.BlockSpec((1, N), lambda i: (0, 0))],
        out_specs=pl.BlockSpec((tm, N), lambda i: (i, 0)),
        out_shape=jax.ShapeDtypeStruct((T, N), F32),
        compiler_params=_cparams(("parallel",)),
        name="proj",
    )(x2d, w_bf, b_row)


def _gelu_tanh(x):
    return 0.5 * x * (1.0 + jnp.tanh(0.7978845608028654 * (x + 0.044715 * (x * x * x))))


def _compress_kernel(k_ref, v_ref, pk_ref, pv_ref, w1k_ref, w2k_ref, w1v_ref, w2v_ref, ok_ref, ov_ref):
    nc = k_ref.shape[2]
    half = CMP_STRIDE * HEAD_DIM

    def one(r_ref, p_ref, w1_ref, w2_ref, o_ref):
        r = r_ref[0, 0]
        a = (r + p_ref[0:1, :]).astype(BF16)
        b = (r + p_ref[1:2, :]).astype(BF16)
        ha = _dot(a, w1_ref[0:half, :])
        hb = _dot(b, w1_ref[half:2 * half, :])
        hid = _gelu_tanh(ha + pltpu.roll(hb, nc - 1, 0))
        o_ref[0, 0] = _dot(hid.astype(BF16), w2_ref[...]).astype(o_ref.dtype)

    one(k_ref, pk_ref, w1k_ref, w2k_ref, ok_ref)
    one(v_ref, pv_ref, w1v_ref, w2v_ref, ov_ref)


def _compress(kc, vc, pos_k, pos_v, w1k, w2k, w1v, w2v):
    B, G, S, dh = kc.shape
    nc = S // CMP_STRIDE
    width = CMP_STRIDE * dh
    kr = kc.reshape(B, G, nc, width)
    vr = vc.reshape(B, G, nc, width)
    pk = pos_k.reshape(2, width)
    pv = pos_v.reshape(2, width)
    kv_spec = pl.BlockSpec((1, 1, nc, width), lambda b, g: (b, g, 0, 0))
    full = lambda shape: pl.BlockSpec(shape, lambda b, g: tuple(0 for _ in shape))
    out_spec = pl.BlockSpec((1, 1, nc, dh), lambda b, g: (b, g, 0, 0))
    return pl.pallas_call(
        _compress_kernel,
        grid=(B, G),
        in_specs=[kv_spec, kv_spec, full((2, width)), full((2, width)),
                  full((2 * width, CMP_HIDDEN)), full((CMP_HIDDEN, dh)),
                  full((2 * width, CMP_HIDDEN)), full((CMP_HIDDEN, dh))],
        out_specs=[out_spec, out_spec],
        out_shape=[jax.ShapeDtypeStruct((B, G, nc, dh), BF16)] * 2,
        compiler_params=_cparams(("parallel", "parallel")),
        name="nsa_compress",
    )(kr, vr, pk, pv, w1k.astype(BF16), w2k.astype(BF16), w1v.astype(BF16), w2v.astype(BF16))


def _nsa_cmp_kernel(slopes_ref, q_ref, kc_ref, vc_ref, ovl_ref, g_ref, o_ref, sel_ref, *, n_sel):
    g = pl.program_id(1)
    i = pl.program_id(2)
    q0 = i * Q_BLOCK
    nc = kc_ref.shape[2]
    nb = ovl_ref.shape[0]
    kc = kc_ref[0, 0]
    vc = vc_ref[0, 0]
    t = q0 + lax.broadcasted_iota(jnp.int32, (Q_BLOCK, nc), 0)
    cmp_end = lax.broadcasted_iota(jnp.int32, (Q_BLOCK, nc), 1) * CMP_STRIDE + (CMP_LEN - 1)
    mask = cmp_end <= t
    end_rel = (lax.broadcasted_iota(jnp.int32, (1, nc), 1) * CMP_STRIDE + (CMP_LEN - 1) - q0).astype(F32)
    psum = jnp.zeros((Q_BLOCK, nc), F32)
    for r in range(NSA_REP):
        slope = slopes_ref[g * NSA_REP + r]
        s = _dot_nt(q_ref[0, r], kc) + slope * end_rel
        s = jnp.where(mask, s, NEG)
        m = jnp.max(s, axis=1, keepdims=True)
        p = jnp.where(mask, jnp.exp(s - m), 0.0)
        l = jnp.sum(p, axis=1, keepdims=True)
        pn = p / jnp.where(l > 0.0, l, 1.0)
        o_ref[0, r] = _dot(pn.astype(BF16), vc) * _sigmoid(g_ref[0, r])
        psum = psum + pn
    hi, lo = _split_bf16(psum)
    ovl = ovl_ref[...]
    imp = _dot_nt(ovl, hi) + _dot_nt(ovl, lo)
    jf = lax.broadcasted_iota(jnp.int32, (nb, Q_BLOCK), 0)
    tq = q0 + lax.broadcasted_iota(jnp.int32, (nb, Q_BLOCK), 1)
    cur = tq >> 6
    valid = jf <= cur
    forced = valid & ((jf == 0) | (jf >= cur - 1))
    v = jnp.where(forced, BIG, jnp.where(valid, imp, -BIG))
    jff = jf.astype(F32)
    picked = jnp.zeros((nb, Q_BLOCK), F32)
    for _ in range(n_sel):
        mx = jnp.max(v, axis=0, keepdims=True)
        idx = jnp.min(jnp.where(v == mx, jff, float(nb)), axis=0, keepdims=True)
        hit = jff == idx
        picked = jnp.where(hit, 1.0, picked)
        v = jnp.where(hit, -BIG, v)
    sel = jnp.where(valid, picked, 0.0)
    sel_ref[0, 0, 0] = sel.T.astype(sel_ref.dtype)


def _nsa_cmp(slopes, q, kcmp, vcmp, ovl_t, gate):
    B, H, S, dh = q.shape
    G = N_NSA_KV
    nq = S // Q_BLOCK
    nc = kcmp.shape[2]
    nb = S // SEL_LEN
    n_sel = min(SEL_TOPK, nb)
    grid_spec = pltpu.PrefetchScalarGridSpec(
        num_scalar_prefetch=1,
        grid=(B, G, nq),
        in_specs=[pl.BlockSpec((1, NSA_REP, Q_BLOCK, dh), lambda b, g, i, s: (b, g, i, 0)),
                  pl.BlockSpec((1, 1, nc, dh), lambda b, g, i, s: (b, g, 0, 0)),
                  pl.BlockSpec((1, 1, nc, dh), lambda b, g, i, s: (b, g, 0, 0)),
                  pl.BlockSpec((nb, nc), lambda b, g, i, s: (0, 0)),
                  pl.BlockSpec((1, NSA_REP, Q_BLOCK, 1), lambda b, g, i, s: (b, g, i, 0))],
        out_specs=[pl.BlockSpec((1, NSA_REP, Q_BLOCK, dh), lambda b, g, i, s: (b, g, i, 0)),
                   pl.BlockSpec((1, 1, 1, Q_BLOCK, nb), lambda b, g, i, s: (b, g, i, 0, 0))],
    )
    return pl.pallas_call(
        functools.partial(_nsa_cmp_kernel, n_sel=n_sel),
        grid_spec=grid_spec,
        out_shape=[jax.ShapeDtypeStruct((B, H, S, dh), F32),
                   jax.ShapeDtypeStruct((B, G, nq, Q_BLOCK, nb), BF16)],
        compiler_params=_cparams(("parallel", "parallel", "parallel")),
        name="nsa_cmp_select",
    )(slopes, q, kcmp, vcmp, ovl_t, gate)


def _online_update(carry, s, v_tile):
    m, l, acc = carry
    m_new = jnp.maximum(m, jnp.max(s, axis=1, keepdims=True))
    alpha = jnp.exp(m - m_new)
    p = jnp.exp(s - m_new)
    l = alpha * l + jnp.sum(p, axis=1, keepdims=True)
    acc = alpha * acc + _dot(p.astype(BF16), v_tile)
    return m_new, l, acc


def _nsa_sel_kernel(slopes_ref, q_ref, k_ref, v_ref, sel_ref, e_ref, g_ref, o_ref, *, kb):
    g = pl.program_id(1)
    i = pl.program_id(2)
    q0 = i * Q_BLOCK
    rows = NSA_REP * Q_BLOCK
    q = q_ref[0].reshape(rows, HEAD_DIM)
    sel = sel_ref[0, 0, 0]
    trow = q0 + lax.broadcasted_iota(jnp.int32, (Q_BLOCK, kb), 0)
    col = lax.broadcasted_iota(jnp.int32, (Q_BLOCK, kb), 1)
    colrow = lax.broadcasted_iota(jnp.int32, (1, kb), 1)
    n_tiles = (q0 + Q_BLOCK - 1) // kb + 1

    def body(kt, carry):
        k0 = pl.multiple_of(kt * kb, kb)
        k_tile = k_ref[0, 0, pl.ds(k0, kb), :]
        v_tile = v_ref[0, 0, pl.ds(k0, kb), :]
        s = _dot_nt(q, k_tile)
        allowed = (_dot(sel, e_ref[kt]) > 0.5) & ((k0 + col) <= trow)
        posrel = (k0 - q0 + colrow).astype(F32)
        parts = []
        for r in range(NSA_REP):
            sr = s[r * Q_BLOCK:(r + 1) * Q_BLOCK] + slopes_ref[g * NSA_REP + r] * posrel
            parts.append(jnp.where(allowed, sr, NEG))
        return _online_update(carry, jnp.concatenate(parts, axis=0), v_tile)

    init = (jnp.full((rows, 1), NEG, F32), jnp.zeros((rows, 1), F32), jnp.zeros((rows, HEAD_DIM), F32))
    _, l, acc = lax.fori_loop(0, n_tiles, body, init)
    o = (acc / l).reshape(NSA_REP, Q_BLOCK, HEAD_DIM)
    o_ref[0] = o * _sigmoid(g_ref[0])


def _nsa_sel(slopes, q, ks, vs, sel, expand, gate, kb):
    B, H, S, dh = q.shape
    G = N_NSA_KV
    nq = S // Q_BLOCK
    nb = S // SEL_LEN
    nkt = S // kb
    grid_spec = pltpu.PrefetchScalarGridSpec(
        num_scalar_prefetch=1,
        grid=(B, G, nq),
        in_specs=[pl.BlockSpec((1, NSA_REP, Q_BLOCK, dh), lambda b, g, i, s: (b, g, i, 0)),
                  pl.BlockSpec((1, 1, S, dh), lambda b, g, i, s: (b, g, 0, 0)),
                  pl.BlockSpec((1, 1, S, dh), lambda b, g, i, s: (b, g, 0, 0)),
                  pl.BlockSpec((1, 1, 1, Q_BLOCK, nb), lambda b, g, i, s: (b, g, i, 0, 0)),
                  pl.BlockSpec((nkt, nb, kb), lambda b, g, i, s: (0, 0, 0)),
                  pl.BlockSpec((1, NSA_REP, Q_BLOCK, 1), lambda b, g, i, s: (b, g, i, 0))],
        out_specs=pl.BlockSpec((1, NSA_REP, Q_BLOCK, dh), lambda b, g, i, s: (b, g, i, 0)),
    )
    return pl.pallas_call(
        functools.partial(_nsa_sel_kernel, kb=kb),
        grid_spec=grid_spec,
        out_shape=jax.ShapeDtypeStruct((B, H, S, dh), F32),
        compiler_params=_cparams(("parallel", "parallel", "parallel")),
        name="nsa_selected",
    )(slopes, q, ks, vs, sel, expand, gate)


def _nsa_win_kernel(slopes_ref, q_ref, k_ref, v_ref, g_ref, o_ref, *, kb):
    g = pl.program_id(1)
    i = pl.program_id(2)
    q0 = i * Q_BLOCK
    rows = NSA_REP * Q_BLOCK
    q = q_ref[0].reshape(rows, HEAD_DIM)
    trow = q0 + lax.broadcasted_iota(jnp.int32, (Q_BLOCK, kb), 0)
    col = lax.broadcasted_iota(jnp.int32, (Q_BLOCK, kb), 1)
    colrow = lax.broadcasted_iota(jnp.int32, (1, kb), 1)
    first = jnp.maximum(q0 - WINDOW, 0) // kb
    last = (q0 + Q_BLOCK - 1) // kb

    def body(kt, carry):
        k0 = pl.multiple_of(kt * kb, kb)
        k_tile = k_ref[0, 0, pl.ds(k0, kb), :]
        v_tile = v_ref[0, 0, pl.ds(k0, kb), :]
        s = _dot_nt(q, k_tile)
        dist = trow - (k0 + col)
        allowed = (dist >= 0) & (dist < WINDOW)
        posrel = (k0 - q0 + colrow).astype(F32)
        parts = []
        for r in range(NSA_REP):
            sr = s[r * Q_BLOCK:(r + 1) * Q_BLOCK] + slopes_ref[g * NSA_REP + r] * posrel
            parts.append(jnp.where(allowed, sr, NEG))
        return _online_update(carry, jnp.concatenate(parts, axis=0), v_tile)

    init = (jnp.full((rows, 1), NEG, F32), jnp.zeros((rows, 1), F32), jnp.zeros((rows, HEAD_DIM), F32))
    _, l, acc = lax.fori_loop(first, last + 1, body, init)
    o = (acc / l).reshape(NSA_REP, Q_BLOCK, HEAD_DIM)
    o_ref[0] = o * _sigmoid(g_ref[0])


def _nsa_win(slopes, q, kw, vw, gate, kb=128):
    B, H, S, dh = q.shape
    G = N_NSA_KV
    nq = S // Q_BLOCK
    grid_spec = pltpu.PrefetchScalarGridSpec(
        num_scalar_prefetch=1,
        grid=(B, G, nq),
        in_specs=[pl.BlockSpec((1, NSA_REP, Q_BLOCK, dh), lambda b, g, i, s: (b, g, i, 0)),
                  pl.BlockSpec((1, 1, S, dh), lambda b, g, i, s: (b, g, 0, 0)),
                  pl.BlockSpec((1, 1, S, dh), lambda b, g, i, s: (b, g, 0, 0)),
                  pl.BlockSpec((1, NSA_REP, Q_BLOCK, 1), lambda b, g, i, s: (b, g, i, 0))],
        out_specs=pl.BlockSpec((1, NSA_REP, Q_BLOCK, dh), lambda b, g, i, s: (b, g, i, 0)),
    )
    return pl.pallas_call(
        functools.partial(_nsa_win_kernel, kb=kb),
        grid_spec=grid_spec,
        out_shape=jax.ShapeDtypeStruct((B, H, S, dh), F32),
        compiler_params=_cparams(("parallel", "parallel", "parallel")),
        name="nsa_window",
    )(slopes, q, kw, vw, gate)


def _fox_kernel(q_ref, k_ref, v_ref, ccol_ref, crow_ref, o_ref, *, tq, kb):
    i = pl.program_id(2)
    q0 = i * tq
    q = q_ref[0, 0]
    trow = q0 + lax.broadcasted_iota(jnp.int32, (tq, kb), 0)
    col = lax.broadcasted_iota(jnp.int32, (tq, kb), 1)
    c0 = ccol_ref[0, 0, 0:1, :]
    n_tiles = (q0 + tq - 1) // kb + 1

    def body(kt, carry):
        k0 = pl.multiple_of(kt * kb, kb)
        k_tile = k_ref[0, 0, pl.ds(k0, kb), :]
        v_tile = v_ref[0, 0, pl.ds(k0, kb), :]
        s = _dot_nt(q, k_tile) + (c0 - crow_ref[0, 0, kt])
        s = jnp.where((k0 + col) <= trow, s, NEG)
        return _online_update(carry, s, v_tile)

    init = (jnp.full((tq, 1), NEG, F32), jnp.zeros((tq, 1), F32), jnp.zeros((tq, HEAD_DIM), F32))
    _, l, acc = lax.fori_loop(0, n_tiles, body, init)
    o_ref[0, 0] = acc / l


def _fox(q, k, v, c, tq, kb):
    B, H, S, dh = q.shape
    ccol = c.reshape(B, H, S, 1)
    crow = c.reshape(B, H, S // kb, 1, kb)
    return pl.pallas_call(
        functools.partial(_fox_kernel, tq=tq, kb=kb),
        grid=(B, H, S // tq),
        in_specs=[pl.BlockSpec((1, 1, tq, dh), lambda b, h, i: (b, h, i, 0)),
                  pl.BlockSpec((1, 1, S, dh), lambda b, h, i: (b, h, 0, 0)),
                  pl.BlockSpec((1, 1, S, dh), lambda b, h, i: (b, h, 0, 0)),
                  pl.BlockSpec((1, 1, tq, 1), lambda b, h, i: (b, h, i, 0)),
                  pl.BlockSpec((1, 1, S // kb, 1, kb), lambda b, h, i: (b, h, 0, 0, 0))],
        out_specs=pl.BlockSpec((1, 1, tq, dh), lambda b, h, i: (b, h, i, 0)),
        out_shape=jax.ShapeDtypeStruct((B, H, S, dh), F32),
        compiler_params=_cparams(("parallel", "parallel", "parallel")),
        name="fox_attention",
    )(q, k, v, ccol, crow)


def _sb_kernel(q_ref, k_ref, v_ref, o_ref, *, tq):
    i = pl.program_id(2)
    q0 = i * tq
    q = q_ref[0, 0]
    trow = q0 + lax.broadcasted_iota(jnp.int32, (tq, tq), 0)
    col = lax.broadcasted_iota(jnp.int32, (tq, tq), 1)
    rr = lax.broadcasted_iota(jnp.int32, (tq, tq), 0)
    upper = (rr > col).astype(BF16)

    def cond(state):
        kt, carry, _ = state
        return (kt >= 0) & (jnp.max(carry) > SB_CUTOFF)

    def body(state):
        kt, carry, acc = state
        k0 = pl.multiple_of(kt * tq, tq)
        k_tile = k_ref[0, 0, pl.ds(k0, tq), :]
        v_tile = v_ref[0, 0, pl.ds(k0, tq), :]
        z = _dot_nt(q, k_tile)
        strict = (k0 + col) < trow
        log_beta = jnp.minimum(z, 0.0) - jnp.log1p(jnp.exp(-jnp.abs(z)))
        log_keep = jnp.where(strict, log_beta - z, 0.0)
        hi, lo = _split_bf16(log_keep)
        later = _dot(hi, upper) + _dot(lo, upper)
        a = jnp.where(strict, jnp.exp(log_beta + later + carry), 0.0)
        acc = acc + _dot(a.astype(BF16), v_tile)
        carry = carry + jnp.sum(log_keep, axis=1, keepdims=True)
        return kt - 1, carry, acc

    state = (i, jnp.zeros((tq, 1), F32), jnp.zeros((tq, HEAD_DIM), F32))
    _, _, acc = lax.while_loop(cond, body, state)
    o_ref[0, 0] = acc


def _sb(q, k, v, tq=128):
    B, H, S, dh = q.shape
    return pl.pallas_call(
        functools.partial(_sb_kernel, tq=tq),
        grid=(B, H, S // tq),
        in_specs=[pl.BlockSpec((1, 1, tq, dh), lambda b, h, i: (b, h, i, 0)),
                  pl.BlockSpec((1, 1, S, dh), lambda b, h, i: (b, h, 0, 0)),
                  pl.BlockSpec((1, 1, S, dh), lambda b, h, i: (b, h, 0, 0))],
        out_specs=pl.BlockSpec((1, 1, tq, dh), lambda b, h, i: (b, h, i, 0)),
        out_shape=jax.ShapeDtypeStruct((B, H, S, dh), F32),
        compiler_params=_cparams(("parallel", "parallel", "parallel")),
        name="sb_attention",
    )(q, k, v)


def _layer_norm(y, g, b):
    mu = jnp.mean(y, axis=1, keepdims=True)
    d = y - mu
    var = jnp.mean(d * d, axis=1, keepdims=True)
    return d * lax.rsqrt(var + LN_EPS) * g + b


def _rms(o, g):
    return o * lax.rsqrt(jnp.mean(o * o, axis=1, keepdims=True) + RMS_EPS) * g


def _post_kernel(on_ref, of_ref, os_ref, x_ref, go_ref, wo_ref, lg_ref, lb_ref, rw_ref, rb_ref,
                 x1_ref, te_ref, tw_ref, *, alpha):
    n1 = _rms(on_ref[...], go_ref[:, 0:D_NSA]).astype(BF16)
    n2 = _rms(of_ref[...], go_ref[:, D_NSA:D_NSA + D_FOX]).astype(BF16)
    n3 = _rms(os_ref[...], go_ref[:, D_NSA + D_FOX:]).astype(BF16)
    mix = (_dot(n1, wo_ref[0:D_NSA, :]) + _dot(n2, wo_ref[D_NSA:D_NSA + D_FOX, :])
           + _dot(n3, wo_ref[D_NSA + D_FOX:, :]))
    x1 = _layer_norm(alpha * x_ref[...] + mix, lg_ref[...], lb_ref[...])
    x1_ref[...] = x1
    logits = jnp.dot(x1, rw_ref[...], preferred_element_type=F32, precision=lax.Precision.HIGHEST) + rb_ref[...]
    tm, ne = logits.shape
    lane = lax.broadcasted_iota(jnp.int32, (tm, ne), 1).astype(F32)
    wide = lax.broadcasted_iota(jnp.int32, (tm, 128), 1)
    top_e = jnp.zeros((tm, 128), F32)
    top_l = jnp.full((tm, 128), NEG, F32)
    cur = logits
    for k in range(TOP_K):
        mx = jnp.max(cur, axis=1, keepdims=True)
        idx = jnp.min(jnp.where(cur == mx, lane, float(ne)), axis=1, keepdims=True)
        top_e = jnp.where(wide == k, idx, top_e)
        top_l = jnp.where(wide == k, mx, top_l)
        cur = jnp.where(lane == idx, -BIG, cur)
    pe = jnp.exp(top_l - jnp.max(top_l, axis=1, keepdims=True))
    te_ref[...] = top_e.astype(jnp.int32)
    tw_ref[...] = pe / jnp.sum(pe, axis=1, keepdims=True)


def _post(o_nsa, o_fox, o_sb, x2d, g_out, w_out_bf, ln_g, ln_b, router_w, router_b, alpha, tm=256):
    T, D = x2d.shape
    row = lambda w: pl.BlockSpec((tm, w), lambda i: (i, 0))
    full = lambda shape: pl.BlockSpec(shape, lambda i: (0, 0))
    return pl.pallas_call(
        functools.partial(_post_kernel, alpha=alpha),
        grid=(T // tm,),
        in_specs=[row(D_NSA), row(D_FOX), row(D_SB), row(D), full((1, D)), full((D, D)),
                  full((1, D)), full((1, D)), full((D, N_EXPERTS)), full((1, N_EXPERTS))],
        out_specs=[row(D), row(128), row(128)],
        out_shape=[jax.ShapeDtypeStruct((T, D), F32), jax.ShapeDtypeStruct((T, 128), jnp.int32),
                   jax.ShapeDtypeStruct((T, 128), F32)],
        compiler_params=_cparams(("parallel",)),
        name="outproj_ln_router",
    )(o_nsa, o_fox, o_sb, x2d, g_out.reshape(1, D), w_out_bf, ln_g.reshape(1, D), ln_b.reshape(1, D),
      router_w, router_b.reshape(1, N_EXPERTS))


def _row_gather_start(src_hbm, dst, sem, idx_ref, n_rows):
    def issue(r, c):
        pltpu.make_async_copy(src_hbm.at[pl.ds(idx_ref[0, 0, r], 1)], dst.at[pl.ds(r, 1)], sem).start()
        return c
    lax.fori_loop(0, n_rows, issue, 0, unroll=8)


def _row_gather_wait(src_hbm, dst, sem, n_rows):
    pltpu.make_async_copy(src_hbm.at[pl.ds(0, n_rows)], dst, sem).wait()


def _expert_kernel(blk_e_ref, nused_ref, idx_ref, idxn_ref, x_hbm, w_ref, wgu_ref, bgu_ref, wdn_ref, bdn_ref,
                   y_ref, xbuf, wgu_bf, wdn_bf, sem, *, tm, d_ff):
    i = pl.program_id(0)
    n_used = nused_ref[0]
    slot = i % 2

    @pl.when(i == 0)
    def _():
        _row_gather_start(x_hbm, xbuf.at[0], sem.at[0], idx_ref, tm)

    @pl.when(i + 1 < n_used)
    def _():
        _row_gather_start(x_hbm, xbuf.at[1 - slot], sem.at[1 - slot], idxn_ref, tm)

    @pl.when(i < n_used)
    def _():
        changed = (i == 0) | (blk_e_ref[i] != blk_e_ref[jnp.maximum(i - 1, 0)])

        @pl.when(changed)
        def _():
            wgu_bf[...] = wgu_ref[0].astype(BF16)
            wdn_bf[...] = wdn_ref[0].astype(BF16)

        _row_gather_wait(x_hbm, xbuf.at[slot], sem.at[slot], tm)
        xb = xbuf[slot].astype(BF16)
        gu = _dot(xb, wgu_bf[...]) + bgu_ref[0]
        gate = jnp.minimum(gu[:, :d_ff], SWIGLU_LIMIT)
        up = jnp.clip(gu[:, d_ff:], -SWIGLU_LIMIT, SWIGLU_LIMIT)
        act = gate * _sigmoid(SWIGLU_ALPHA * gate) * (up + 1.0)
        y = _dot(act.astype(BF16), wdn_bf[...]) + bdn_ref[0]
        y_ref[...] = y * w_ref[0]

    @pl.when(i >= n_used)
    def _():
        y_ref[...] = jnp.zeros_like(y_ref)


def _experts(x1, blk_e, n_used, buf_tok, buf_w, w_gu, b_gu, w_dn, b_dn, tm):
    T, D = x1.shape
    ne, _, d2 = w_gu.shape
    d_ff = d2 // 2
    n_blocks = blk_e.shape[0]
    idx = buf_tok.reshape(n_blocks, 1, tm)
    wcol = buf_w.reshape(n_blocks, tm, 1)
    last = n_blocks - 1
    grid_spec = pltpu.PrefetchScalarGridSpec(
        num_scalar_prefetch=2,
        grid=(n_blocks,),
        in_specs=[pl.BlockSpec((1, 1, tm), lambda i, e, n: (i, 0, 0), memory_space=pltpu.SMEM),
                  pl.BlockSpec((1, 1, tm), lambda i, e, n: (jnp.minimum(i + 1, last), 0, 0),
                               memory_space=pltpu.SMEM),
                  pl.BlockSpec(memory_space=pl.ANY),
                  pl.BlockSpec((1, tm, 1), lambda i, e, n: (i, 0, 0)),
                  pl.BlockSpec((1, D, d2), lambda i, e, n: (e[i], 0, 0)),
                  pl.BlockSpec((1, 1, d2), lambda i, e, n: (e[i], 0, 0)),
                  pl.BlockSpec((1, d_ff, D), lambda i, e, n: (e[i], 0, 0)),
                  pl.BlockSpec((1, 1, D), lambda i, e, n: (e[i], 0, 0))],
        out_specs=pl.BlockSpec((tm, D), lambda i, e, n: (i, 0)),
        scratch_shapes=[pltpu.VMEM((2, tm, D), F32), pltpu.VMEM((D, d2), BF16), pltpu.VMEM((d_ff, D), BF16),
                        pltpu.SemaphoreType.DMA((2,))],
    )
    return pl.pallas_call(
        functools.partial(_expert_kernel, tm=tm, d_ff=d_ff),
        grid_spec=grid_spec,
        out_shape=jax.ShapeDtypeStruct((n_blocks * tm, D), F32),
        compiler_params=_cparams(("arbitrary",), 58 * 1024 * 1024),
        name="moe_experts",
    )(blk_e, n_used, idx, idx, x1, wcol, w_gu, b_gu.reshape(ne, 1, d2), w_dn, b_dn.reshape(ne, 1, D))


def _combine_kernel(idx_ref, idxn_ref, y_hbm, x_ref, lg_ref, lb_ref, o_ref, ybuf, sem, *, tm, alpha):
    i = pl.program_id(0)
    n = pl.num_programs(0)
    slot = i % 2
    rows = TOP_K * tm

    @pl.when(i == 0)
    def _():
        _row_gather_start(y_hbm, ybuf.at[0], sem.at[0], idx_ref, rows)

    @pl.when(i + 1 < n)
    def _():
        _row_gather_start(y_hbm, ybuf.at[1 - slot], sem.at[1 - slot], idxn_ref, rows)

    _row_gather_wait(y_hbm, ybuf.at[slot], sem.at[slot], rows)
    ffn = ybuf[slot, 0:tm]
    for k in range(1, TOP_K):
        ffn = ffn + ybuf[slot, k * tm:(k + 1) * tm]
    o_ref[...] = _layer_norm(alpha * x_ref[...] + ffn, lg_ref[...], lb_ref[...])


def _combine(y_sorted, pos, x1, ln_g, ln_b, alpha, tm=128):
    T, D = x1.shape
    nt = T // tm
    idx = pos.reshape(nt, tm, TOP_K).transpose(0, 2, 1).reshape(nt, 1, TOP_K * tm)
    last = nt - 1
    return pl.pallas_call(
        functools.partial(_combine_kernel, tm=tm, alpha=alpha),
        grid=(nt,),
        in_specs=[pl.BlockSpec((1, 1, TOP_K * tm), lambda i: (i, 0, 0), memory_space=pltpu.SMEM),
                  pl.BlockSpec((1, 1, TOP_K * tm), lambda i: (jnp.minimum(i + 1, last), 0, 0),
                               memory_space=pltpu.SMEM),
                  pl.BlockSpec(memory_space=pl.ANY),
                  pl.BlockSpec((tm, D), lambda i: (i, 0)),
                  pl.BlockSpec((1, D), lambda i: (0, 0)),
                  pl.BlockSpec((1, D), lambda i: (0, 0))],
        out_specs=pl.BlockSpec((tm, D), lambda i: (i, 0)),
        out_shape=jax.ShapeDtypeStruct((T, D), F32),
        scratch_shapes=[pltpu.VMEM((2, TOP_K * tm, D), F32), pltpu.SemaphoreType.DMA((2,))],
        compiler_params=_cparams(("arbitrary",)),
        name="moe_combine_ln",
    )(idx, idx, y_sorted, x1, ln_g.reshape(1, D), ln_b.reshape(1, D))


def _route(top_e, top_w, tm):
    T = top_e.shape[0]
    M = T * TOP_K
    flat_e = top_e.reshape(M)
    flat_w = top_w.reshape(M)
    order = jnp.argsort(flat_e)
    sorted_e = flat_e[order]
    counts = jnp.bincount(flat_e, length=N_EXPERTS)
    start = jnp.cumsum(counts) - counts
    padded = (counts + tm - 1) // tm * tm
    pad_end = jnp.cumsum(padded)
    pad_start = pad_end - padded
    dest = (pad_start[sorted_e] + jnp.arange(M, dtype=jnp.int32) - start[sorted_e]).astype(jnp.int32)
    n_blocks = -(-M // tm) + N_EXPERTS
    P = n_blocks * tm
    buf_tok = jnp.zeros((P,), jnp.int32).at[dest].set((order // TOP_K).astype(jnp.int32))
    buf_w = jnp.zeros((P,), F32).at[dest].set(flat_w[order])
    pos = jnp.zeros((M,), jnp.int32).at[order].set(dest)
    blk_e = jnp.minimum(jnp.searchsorted(pad_end, jnp.arange(n_blocks) * tm, side='right'),
                        N_EXPERTS - 1).astype(jnp.int32)
    n_used = (pad_end[-1] // tm).astype(jnp.int32).reshape(1)
    return blk_e, n_used, buf_tok, buf_w, pos


def _proj_layout():
    src = np.zeros((D_IN_PAD,), np.int64)
    scale = np.zeros((D_IN_PAD,), np.float32)
    off_gate_src = 512 + 768
    off_fox_src = off_gate_src + 24
    off_logf_src = off_fox_src + 768
    off_sb_src = off_logf_src + 4
    qs = HEAD_DIM ** -0.5

    def put(dst, s0, n, sc=1.0):
        src[dst:dst + n] = np.arange(s0, s0 + n)
        scale[dst:dst + n] = sc

    put(0, 0, 512, qs)
    put(OFF_KV, 512, 768)
    put(OFF_FOX, off_fox_src, 256, qs)
    put(OFF_FOX + 256, off_fox_src + 256, 512)
    put(OFF_SB, off_sb_src, 256, qs)
    put(OFF_SB + 256, off_sb_src + 256, 512)
    put(OFF_GATE, off_gate_src, 24)
    put(OFF_LOGF, off_logf_src, 4)
    return src, scale


def _heads(p, b, s, n):
    return p.reshape(b, s, n, HEAD_DIM).transpose(0, 2, 1, 3)


def _mixer_heads(proj, B, S):
    kv = proj[..., OFF_KV:OFF_FOX].reshape(B, S, 3, 2, N_NSA_KV, HEAD_DIM).transpose(2, 3, 0, 4, 1, 5)
    fox = proj[..., OFF_FOX:OFF_SB].reshape(B, S, 3, N_FOX_HEADS, HEAD_DIM).transpose(2, 0, 3, 1, 4)
    sb = proj[..., OFF_SB:OFF_GATE].reshape(B, S, 3, N_SB_HEADS, HEAD_DIM).transpose(2, 0, 3, 1, 4)
    gates = proj[..., OFF_GATE:OFF_LOGF].reshape(B, S, 3, N_NSA_HEADS).transpose(2, 0, 3, 1)[..., None]
    q_nsa = _heads(proj[..., 0:D_NSA], B, S, N_NSA_HEADS).astype(BF16)
    return q_nsa, kv, fox.astype(BF16), sb.astype(BF16), gates


def _layer(x, consts, w_in, b_in, pos_k, pos_v, w1k, w2k, w1v, w2v, g_out, w_out, ln1_g, ln1_b,
           router_w, router_b, w_gu, b_gu, w_dn, b_dn, ln2_g, ln2_b, alpha, moe_tm):
    B, S, D = x.shape
    T = B * S
    src, scale, slopes, ovl_t, expand, sel_kb = consts
    w_p = (w_in[:, src] * scale).astype(BF16)
    b_p = (b_in[src] * scale).reshape(1, D_IN_PAD)
    x2d = x.reshape(T, D)
    proj = _proj(x2d, w_p, b_p).reshape(B, S, D_IN_PAD)
    q_nsa, kv, fox, sb, gates = _mixer_heads(proj, B, S)

    k_cmp, v_cmp = _compress(kv[0, 0], kv[0, 1], pos_k, pos_v, w1k, w2k, w1v, w2v)
    o_c, sel = _nsa_cmp(slopes, q_nsa, k_cmp, v_cmp, ovl_t, gates[0])
    o_s = _nsa_sel(slopes, q_nsa, kv[1, 0].astype(BF16), kv[1, 1].astype(BF16), sel, expand, gates[1], sel_kb)
    o_w = _nsa_win(slopes, q_nsa, kv[2, 0].astype(BF16), kv[2, 1].astype(BF16), gates[2])

    log_f = jax.nn.log_sigmoid(proj[..., OFF_LOGF:OFF_LOGF + N_FOX_HEADS]).transpose(0, 2, 1)
    c = jnp.cumsum(log_f, axis=-1)
    o_fox = _fox(fox[0], fox[1], fox[2], c, tq=min(512, S), kb=min(256, S))
    o_sb = _sb(sb[0], sb[1], sb[2])

    def tokens(o):
        b, h, s, d = o.shape
        return o.transpose(0, 2, 1, 3).reshape(b * s, h * d)

    x1, te, tw = _post(tokens(o_c + o_s + o_w), tokens(o_fox), tokens(o_sb), x2d, g_out, w_out.astype(BF16),
                       ln1_g, ln1_b, router_w, router_b, alpha)
    blk_e, n_used, buf_tok, buf_w, pos = _route(te[:, :TOP_K], tw[:, :TOP_K], moe_tm)
    y_sorted = _experts(x1, blk_e, n_used, buf_tok, buf_w, w_gu, b_gu, w_dn, b_dn, moe_tm)
    out = _combine(y_sorted, pos, x1, ln2_g, ln2_b, alpha)
    return out.reshape(B, S, D)


def _constants(S, sel_kb):
    src, scale = _proj_layout()
    slopes = jnp.exp2(-8.0 * jnp.arange(1, N_NSA_HEADS + 1, dtype=F32) / N_NSA_HEADS)
    nc = S // CMP_STRIDE
    nb = S // SEL_LEN
    n_cmp = (S - CMP_LEN) // CMP_STRIDE + 1
    cmp_idx = np.arange(nc)[:, None] * CMP_STRIDE + np.arange(CMP_LEN)[None, :]
    ovl = (cmp_idx[:, :, None] // SEL_LEN == np.arange(nb)[None, None, :]).astype(np.float32).mean(axis=1)
    ovl[n_cmp:] = 0.0
    ovl_t = jnp.asarray(ovl.T, dtype=BF16)
    keys = np.arange(S).reshape(S // sel_kb, 1, sel_kb)
    expand = jnp.asarray(keys // SEL_LEN == np.arange(nb)[None, :, None], dtype=BF16)
    return jnp.asarray(src), jnp.asarray(scale), slopes, ovl_t, expand, sel_kb


def kernel(x, w_in, b_in, cmp_pos_k, cmp_pos_v, cmp_w1_k, cmp_w2_k, cmp_w1_v, cmp_w2_v, g_out, w_out,
           ln1_g, ln1_b, router_w, router_b, w_gate_up, b_gate_up, w_down, b_down, ln2_g, ln2_b):
    depth = w_in.shape[0]
    S = x.shape[1]
    alpha = (2 * depth) ** 0.25
    consts = _constants(S, min(256, S))
    for l in range(depth):
        x = _layer(x, consts, w_in[l], b_in[l], cmp_pos_k[l], cmp_pos_v[l], cmp_w1_k[l], cmp_w2_k[l],
                   cmp_w1_v[l], cmp_w2_v[l], g_out[l], w_out[l], ln1_g[l], ln1_b[l], router_w[l], router_b[l],
                   w_gate_up[l], b_gate_up[l], w_down[l], b_down[l], ln2_g[l], ln2_b[l], alpha, 256)
    return x
```

```python
import functools

import jax
import jax.numpy as jnp
import numpy as np
from jax import lax
from jax.experimental import pallas as pl
from jax.experimental.pallas import tpu as pltpu

F32 = jnp.float32
BF16 = jnp.bfloat16

HEAD_DIM = 64
N_NSA_HEADS = 8
N_NSA_KV = 2
NSA_REP = 4
N_FOX_HEADS = 4
N_SB_HEADS = 4
D_NSA = 512
D_FOX = 256
D_SB = 256
CMP_LEN = 32
CMP_STRIDE = 16
CMP_HIDDEN = 128
SEL_LEN = 64
SEL_TOPK = 16
WINDOW = 512
Q_BLOCK = 128
N_EXPERTS = 32
TOP_K = 4
SWIGLU_LIMIT = 7.0
SWIGLU_ALPHA = 1.702
LN_EPS = 1e-5
RMS_EPS = 1e-6
NEG = -1e30
BIG = 3e38
SB_CUTOFF = -110.0

D_IN_PAD = 3072
OFF_KV = 512
OFF_FOX = 1280
OFF_SB = 2048
OFF_GATE = 2816
OFF_LOGF = 2840

VMEM_LIMIT = 52 * 1024 * 1024


def _cparams(sem, vmem=VMEM_LIMIT):
    return pltpu.CompilerParams(dimension_semantics=sem, vmem_limit_bytes=vmem)


def _dot(a, b):
    return jnp.dot(a, b, preferred_element_type=F32)


def _dot_nt(a, b):
    return lax.dot_general(a, b, (((1,), (1,)), ((), ())), preferred_element_type=F32)


def _split_bf16(x):
    hi = x.astype(BF16)
    lo = (x - hi.astype(F32)).astype(BF16)
    return hi, lo


def _trunc_bf16(x):
    bits = lax.bitcast_convert_type(x, jnp.uint32) & jnp.uint32(0xFFFF0000)
    return lax.bitcast_convert_type(bits, F32)


def _sigmoid(x):
    return 1.0 / (1.0 + jnp.exp(-x))


def _proj_kernel(x_ref, w_ref, b_ref, o_ref):
    o_ref[...] = _dot(x_ref[...].astype(BF16), w_ref[...]) + b_ref[...]


def _proj(x2d, w_bf, b_row, tm=512):
    T, D = x2d.shape
    N = w_bf.shape[1]
    return pl.pallas_call(
        _proj_kernel,
        grid=(T // tm,),
        in_specs=[pl.BlockSpec((tm, D), lambda i: (i, 0)),
                  pl.BlockSpec((D, N), lambda i: (0, 0)),
                  pl.BlockSpec((1, N), lambda i: (0, 0))],
        out_specs=pl.BlockSpec((tm, N), lambda i: (i, 0)),
        out_shape=jax.ShapeDtypeStruct((T, N), F32),
        compiler_params=_cparams(("parallel",)),
        name="proj",
    )(x2d, w_bf, b_row)


def _gelu_tanh(x):
    return 0.5 * x * (1.0 + jnp.tanh(0.7978845608028654 * (x + 0.044715 * (x * x * x))))


def _compress_kernel(k_ref, v_ref, pk_ref, pv_ref, w1k_ref, w2k_ref, w1v_ref, w2v_ref, ok_ref, ov_ref):
    nc = k_ref.shape[2]
    half = CMP_STRIDE * HEAD_DIM

    def one(r_ref, p_ref, w1_ref, w2_ref, o_ref):
        r = r_ref[0, 0]
        a = (r + p_ref[0:1, :]).astype(BF16)
        b = (r + p_ref[1:2, :]).astype(BF16)
        ha = _dot(a, w1_ref[0:half, :])
        hb = _dot(b, w1_ref[half:2 * half, :])
        hid = _gelu_tanh(ha + pltpu.roll(hb, nc - 1, 0))
        o_ref[0, 0] = _dot(hid.astype(BF16), w2_ref[...]).astype(o_ref.dtype)

    one(k_ref, pk_ref, w1k_ref, w2k_ref, ok_ref)
    one(v_ref, pv_ref, w1v_ref, w2v_ref, ov_ref)


def _compress(kc, vc, pos_k, pos_v, w1k, w2k, w1v, w2v):
    B, G, S, dh = kc.shape
    nc = S // CMP_STRIDE
    width = CMP_STRIDE * dh
    kr = kc.reshape(B, G, nc, width)
    vr = vc.reshape(B, G, nc, width)
    pk = pos_k.reshape(2, width)
    pv = pos_v.reshape(2, width)
    kv_spec = pl.BlockSpec((1, 1, nc, width), lambda b, g: (b, g, 0, 0))
    full = lambda shape: pl.BlockSpec(shape, lambda b, g: tuple(0 for _ in shape))
    out_spec = pl.BlockSpec((1, 1, nc, dh), lambda b, g: (b, g, 0, 0))
    return pl.pallas_call(
        _compress_kernel,
        grid=(B, G),
        in_specs=[kv_spec, kv_spec, full((2, width)), full((2, width)),
                  full((2 * width, CMP_HIDDEN)), full((CMP_HIDDEN, dh)),
                  full((2 * width, CMP_HIDDEN)), full((CMP_HIDDEN, dh))],
        out_specs=[out_spec, out_spec],
        out_shape=[jax.ShapeDtypeStruct((B, G, nc, dh), BF16)] * 2,
        compiler_params=_cparams(("parallel", "parallel")),
        name="nsa_compress",
    )(kr, vr, pk, pv, w1k.astype(BF16), w2k.astype(BF16), w1v.astype(BF16), w2v.astype(BF16))


def _nsa_cmp_kernel(slopes_ref, q_ref, kc_ref, vc_ref, ovl_ref, g_ref, o_ref, sel_ref, *, n_sel):
    g = pl.program_id(1)
    i = pl.program_id(2)
    q0 = i * Q_BLOCK
    nc = kc_ref.shape[2]
    nb = ovl_ref.shape[0]
    kc = kc_ref[0, 0]
    vc = vc_ref[0, 0]
    t = q0 + lax.broadcasted_iota(jnp.int32, (Q_BLOCK, nc), 0)
    cmp_end = lax.broadcasted_iota(jnp.int32, (Q_BLOCK, nc), 1) * CMP_STRIDE + (CMP_LEN - 1)
    mask = cmp_end <= t
    end_rel = (lax.broadcasted_iota(jnp.int32, (1, nc), 1) * CMP_STRIDE + (CMP_LEN - 1) - q0).astype(F32)
    psum = jnp.zeros((Q_BLOCK, nc), F32)
    for r in range(NSA_REP):
        slope = slopes_ref[g * NSA_REP + r]
        s = _dot_nt(q_ref[0, r], kc) + slope * end_rel
        s = jnp.where(mask, s, NEG)
        m = jnp.max(s, axis=1, keepdims=True)
        p = jnp.where(mask, jnp.exp(s - m), 0.0)
        l = jnp.sum(p, axis=1, keepdims=True)
        pn = p / jnp.where(l > 0.0, l, 1.0)
        o_ref[0, r] = _dot(pn.astype(BF16), vc) * _sigmoid(g_ref[0, r])
        psum = psum + pn
    hi, lo = _split_bf16(psum)
    ovl = ovl_ref[...]
    imp = _dot_nt(ovl, hi) + _dot_nt(ovl, lo)
    jf = lax.broadcasted_iota(jnp.int32, (nb, Q_BLOCK), 0)
    tq = q0 + lax.broadcasted_iota(jnp.int32, (nb, Q_BLOCK), 1)
    cur = tq >> 6
    valid = jf <= cur
    forced = valid & ((jf == 0) | (jf >= cur - 1))
    v = jnp.where(forced, BIG, jnp.where(valid, imp, -BIG))
    jff = jf.astype(F32)
    picked = jnp.zeros((nb, Q_BLOCK), F32)
    for _ in range(n_sel):
        mx = jnp.max(v, axis=0, keepdims=True)
        idx = jnp.min(jnp.where(v == mx, jff, float(nb)), axis=0, keepdims=True)
        hit = jff == idx
        picked = jnp.where(hit, 1.0, picked)
        v = jnp.where(hit, -BIG, v)
    msel = jnp.where(valid & (picked > 0.5), 0.0, NEG)
    sel_ref[0, 0, 0] = msel.T.astype(sel_ref.dtype)


def _nsa_cmp(slopes, q, kcmp, vcmp, ovl_t, gate):
    B, H, S, dh = q.shape
    G = N_NSA_KV
    nq = S // Q_BLOCK
    nc = kcmp.shape[2]
    nb = S // SEL_LEN
    n_sel = min(SEL_TOPK, nb)
    grid_spec = pltpu.PrefetchScalarGridSpec(
        num_scalar_prefetch=1,
        grid=(B, G, nq),
        in_specs=[pl.BlockSpec((1, NSA_REP, Q_BLOCK, dh), lambda b, g, i, s: (b, g, i, 0)),
                  pl.BlockSpec((1, 1, nc, dh), lambda b, g, i, s: (b, g, 0, 0)),
                  pl.BlockSpec((1, 1, nc, dh), lambda b, g, i, s: (b, g, 0, 0)),
                  pl.BlockSpec((nb, nc), lambda b, g, i, s: (0, 0)),
                  pl.BlockSpec((1, NSA_REP, Q_BLOCK, 1), lambda b, g, i, s: (b, g, i, 0))],
        out_specs=[pl.BlockSpec((1, NSA_REP, Q_BLOCK, dh), lambda b, g, i, s: (b, g, i, 0)),
                   pl.BlockSpec((1, 1, 1, Q_BLOCK, nb), lambda b, g, i, s: (b, g, i, 0, 0))],
    )
    return pl.pallas_call(
        functools.partial(_nsa_cmp_kernel, n_sel=n_sel),
        grid_spec=grid_spec,
        out_shape=[jax.ShapeDtypeStruct((B, H, S, dh), F32),
                   jax.ShapeDtypeStruct((B, G, nq, Q_BLOCK, nb), BF16)],
        compiler_params=_cparams(("parallel", "parallel", "parallel")),
        name="nsa_cmp_select",
    )(slopes, q, kcmp, vcmp, ovl_t, gate)


ACC_W = 128


def _flash_init(rows):
    return jnp.full((rows, 1), NEG, F32), jnp.zeros((rows, ACC_W), F32)


def _flash_step(carry, qa, k_tile, v_tile, mask=None):
    m, acc = carry
    s = _dot_nt(qa, k_tile)
    if mask is not None:
        s = jnp.where(mask, s, NEG)
    m_new = jnp.maximum(m, jnp.max(s, axis=1, keepdims=True))
    p = jnp.exp(s - m_new)
    acc = jnp.exp(m - m_new) * acc + _dot(p.astype(BF16), v_tile)
    return m_new, acc


def _flash_update(carry, s, v_tile):
    m, acc = carry
    m_new = jnp.maximum(m, jnp.max(s, axis=1, keepdims=True))
    p = jnp.exp(s - m_new)
    acc = jnp.exp(m - m_new) * acc + _dot(p.astype(BF16), v_tile)
    return m_new, acc


def _flash_out(acc):
    return acc[:, 0:HEAD_DIM] / acc[:, HEAD_DIM:HEAD_DIM + 1]


def _stacked_rows(rows, kb):
    row = lax.broadcasted_iota(jnp.int32, (rows, kb), 0) & (Q_BLOCK - 1)
    col = lax.broadcasted_iota(jnp.int32, (rows, kb), 1)
    return row, col


def _nsa_sel_kernel(q_ref, msel_ref, k_ref, v_ref, g_ref, o_ref, qa_ref, sa_ref, sb_ref, *, kb):
    i = pl.program_id(2)
    q0 = i * Q_BLOCK
    rows = NSA_REP * Q_BLOCK
    msel = msel_ref[0, 0, 0]
    for r in range(NSA_REP):
        qa_ref[r * Q_BLOCK:(r + 1) * Q_BLOCK, :] = jnp.concatenate([q_ref[0, r], msel], axis=1)
    qa = qa_ref[...]
    last = (q0 + Q_BLOCK - 1) // kb

    def logits(kt):
        k0 = pl.multiple_of(kt * kb, kb)
        return _dot_nt(qa, k_ref[0, 0, pl.ds(k0, kb), :])

    def v_tile(kt):
        return v_ref[0, 0, pl.ds(pl.multiple_of(kt * kb, kb), kb), :]

    sa_ref[...] = logits(0)

    def body(j, carry):
        sb_ref[...] = logits(2 * j + 1)
        carry = _flash_update(carry, sa_ref[...], v_tile(2 * j))
        sa_ref[...] = logits(2 * j + 2)
        return _flash_update(carry, sb_ref[...], v_tile(2 * j + 1))

    pairs = last // 2
    carry = lax.fori_loop(0, pairs, body, _flash_init(rows))
    row, col = _stacked_rows(rows, kb)
    causal = (last * kb + col) <= (q0 + row)

    def tail_odd(carry):
        carry = _flash_update(carry, sa_ref[...], v_tile(last - 1))
        return _flash_update(carry, jnp.where(causal, logits(last), NEG), v_tile(last))

    def tail_even(carry):
        return _flash_update(carry, jnp.where(causal, sa_ref[...], NEG), v_tile(last))

    _, acc = lax.cond(last - 2 * pairs == 1, tail_odd, tail_even, carry)
    o = _flash_out(acc).reshape(NSA_REP, Q_BLOCK, HEAD_DIM)
    o_ref[0] = o * _sigmoid(g_ref[0])


def _nsa_sel(q2, msel, k2, v2, gate, kb):
    B, H, S, qw = q2.shape
    G = N_NSA_KV
    nq = S // Q_BLOCK
    nb = msel.shape[-1]
    kw = k2.shape[-1]
    return pl.pallas_call(
        functools.partial(_nsa_sel_kernel, kb=kb),
        grid=(B, G, nq),
        in_specs=[pl.BlockSpec((1, NSA_REP, Q_BLOCK, qw), lambda b, g, i: (b, g, i, 0)),
                  pl.BlockSpec((1, 1, 1, Q_BLOCK, nb), lambda b, g, i: (b, g, i, 0, 0)),
                  pl.BlockSpec((1, 1, S, kw), lambda b, g, i: (b, g, 0, 0)),
                  pl.BlockSpec((1, 1, S, ACC_W), lambda b, g, i: (b, g, 0, 0)),
                  pl.BlockSpec((1, NSA_REP, Q_BLOCK, 1), lambda b, g, i: (b, g, i, 0))],
        out_specs=pl.BlockSpec((1, NSA_REP, Q_BLOCK, HEAD_DIM), lambda b, g, i: (b, g, i, 0)),
        out_shape=jax.ShapeDtypeStruct((B, H, S, HEAD_DIM), F32),
        scratch_shapes=[pltpu.VMEM((NSA_REP * Q_BLOCK, qw + nb), BF16),
                        pltpu.VMEM((NSA_REP * Q_BLOCK, kb), F32),
                        pltpu.VMEM((NSA_REP * Q_BLOCK, kb), F32)],
        compiler_params=_cparams(("parallel", "parallel", "parallel")),
        name="nsa_selected",
    )(q2, msel, k2, v2, gate)


def _nsa_win_kernel(q_ref, k_ref, v_ref, g_ref, o_ref, *, kb):
    i = pl.program_id(2)
    q0 = i * Q_BLOCK
    rows = NSA_REP * Q_BLOCK
    qa = q_ref[0].reshape(rows, q_ref.shape[-1])
    assert kb == Q_BLOCK
    n_back = WINDOW // kb
    row, col = _stacked_rows(rows, kb)
    carry = _flash_init(rows)
    for d in range(n_back + 1):
        kt = i - n_back + d
        k0 = pl.multiple_of(jnp.maximum(kt, 0) * kb, kb)
        s = _dot_nt(qa, k_ref[0, 0, pl.ds(k0, kb), :])
        if d == 0:
            s = jnp.where((col > row) & (kt >= 0), s, NEG)
        elif d == n_back:
            s = jnp.where(col <= row, s, NEG)
        else:
            s = jnp.where(kt >= 0, s, NEG)
        carry = _flash_update(carry, s, v_ref[0, 0, pl.ds(k0, kb), :])
    acc = carry[1]
    o = _flash_out(acc).reshape(NSA_REP, Q_BLOCK, HEAD_DIM)
    o_ref[0] = o * _sigmoid(g_ref[0])


def _nsa_win(q2, k2, v2, gate, kb=128):
    B, H, S, qw = q2.shape
    G = N_NSA_KV
    nq = S // Q_BLOCK
    return pl.pallas_call(
        functools.partial(_nsa_win_kernel, kb=kb),
        grid=(B, G, nq),
        in_specs=[pl.BlockSpec((1, NSA_REP, Q_BLOCK, qw), lambda b, g, i: (b, g, i, 0)),
                  pl.BlockSpec((1, 1, S, qw), lambda b, g, i: (b, g, 0, 0)),
                  pl.BlockSpec((1, 1, S, ACC_W), lambda b, g, i: (b, g, 0, 0)),
                  pl.BlockSpec((1, NSA_REP, Q_BLOCK, 1), lambda b, g, i: (b, g, i, 0))],
        out_specs=pl.BlockSpec((1, NSA_REP, Q_BLOCK, HEAD_DIM), lambda b, g, i: (b, g, i, 0)),
        out_shape=jax.ShapeDtypeStruct((B, H, S, HEAD_DIM), F32),
        compiler_params=_cparams(("parallel", "parallel", "parallel")),
        name="nsa_window",
    )(q2, k2, v2, gate)


def _fox_kernel(q_ref, k_ref, v_ref, o_ref, sa_ref, sb_ref, *, tq, kb):
    i = pl.program_id(2)
    q0 = i * tq
    qa = q_ref[0, 0]
    assert tq == 2 * kb

    def logits(kt):
        k0 = pl.multiple_of(kt * kb, kb)
        return _dot_nt(qa, k_ref[0, 0, pl.ds(k0, kb), :])

    def v_tile(kt):
        return v_ref[0, 0, pl.ds(pl.multiple_of(kt * kb, kb), kb), :]

    sa_ref[...] = logits(0)

    def body(j, carry):
        sb_ref[...] = logits(2 * j + 1)
        carry = _flash_update(carry, sa_ref[...], v_tile(2 * j))
        sa_ref[...] = logits(2 * j + 2)
        return _flash_update(carry, sb_ref[...], v_tile(2 * j + 1))

    carry = lax.fori_loop(0, i, body, _flash_init(tq))
    row = lax.broadcasted_iota(jnp.int32, (tq, kb), 0)
    col = lax.broadcasted_iota(jnp.int32, (tq, kb), 1)
    s1 = logits(2 * i + 1)
    carry = _flash_update(carry, jnp.where(col <= row, sa_ref[...], NEG), v_tile(2 * i))
    carry = _flash_update(carry, jnp.where(kb + col <= row, s1, NEG), v_tile(2 * i + 1))
    o_ref[0, 0] = _flash_out(carry[1])


def _fox(q2, k2, v2, tq, kb):
    B, H, S, qw = q2.shape
    return pl.pallas_call(
        functools.partial(_fox_kernel, tq=tq, kb=kb),
        grid=(B, H, S // tq),
        in_specs=[pl.BlockSpec((1, 1, tq, qw), lambda b, h, i: (b, h, i, 0)),
                  pl.BlockSpec((1, 1, S, qw), lambda b, h, i: (b, h, 0, 0)),
                  pl.BlockSpec((1, 1, S, ACC_W), lambda b, h, i: (b, h, 0, 0))],
        out_specs=pl.BlockSpec((1, 1, tq, HEAD_DIM), lambda b, h, i: (b, h, i, 0)),
        out_shape=jax.ShapeDtypeStruct((B, H, S, HEAD_DIM), F32),
        scratch_shapes=[pltpu.VMEM((tq, kb), F32), pltpu.VMEM((tq, kb), F32)],
        compiler_params=_cparams(("parallel", "parallel", "parallel")),
        name="fox_attention",
    )(q2, k2, v2)


def _sb_kernel(q_ref, k_ref, v_ref, o_ref, *, tq):
    i = pl.program_id(2)
    q0 = i * tq
    q = q_ref[0, 0]
    trow = q0 + lax.broadcasted_iota(jnp.int32, (tq, tq), 0)
    col = lax.broadcasted_iota(jnp.int32, (tq, tq), 1)
    rr = lax.broadcasted_iota(jnp.int32, (tq, tq), 0)
    upper = (rr > col).astype(BF16)

    def cond(state):
        kt, carry, _ = state
        return (kt >= 0) & (jnp.max(carry) > SB_CUTOFF)

    def body(state):
        kt, carry, acc = state
        k0 = pl.multiple_of(kt * tq, tq)
        k_tile = k_ref[0, 0, pl.ds(k0, tq), :]
        v_tile = v_ref[0, 0, pl.ds(k0, tq), :]
        z = _dot_nt(q, k_tile)
        strict = (k0 + col) < trow
        log_beta = jnp.minimum(z, 0.0) - jnp.log1p(jnp.exp(-jnp.abs(z)))
        log_keep = jnp.where(strict, log_beta - z, 0.0)
        hi, lo = _split_bf16(log_keep)
        later = _dot(hi, upper) + _dot(lo, upper)
        a = jnp.where(strict, jnp.exp(log_beta + later + carry), 0.0)
        acc = acc + _dot(a.astype(BF16), v_tile)
        carry = carry + jnp.sum(log_keep, axis=1, keepdims=True)
        return kt - 1, carry, acc

    state = (i, jnp.zeros((tq, 1), F32), jnp.zeros((tq, HEAD_DIM), F32))
    _, _, acc = lax.while_loop(cond, body, state)
    o_ref[0, 0] = acc


def _sb(q, k, v, tq=128):
    B, H, S, dh = q.shape
    return pl.pallas_call(
        functools.partial(_sb_kernel, tq=tq),
        grid=(B, H, S // tq),
        in_specs=[pl.BlockSpec((1, 1, tq, dh), lambda b, h, i: (b, h, i, 0)),
                  pl.BlockSpec((1, 1, S, dh), lambda b, h, i: (b, h, 0, 0)),
                  pl.BlockSpec((1, 1, S, dh), lambda b, h, i: (b, h, 0, 0))],
        out_specs=pl.BlockSpec((1, 1, tq, dh), lambda b, h, i: (b, h, i, 0)),
        out_shape=jax.ShapeDtypeStruct((B, H, S, dh), F32),
        compiler_params=_cparams(("parallel", "parallel", "parallel")),
        name="sb_attention",
    )(q, k, v)


def _layer_norm(y, g, b):
    mu = jnp.mean(y, axis=1, keepdims=True)
    d = y - mu
    var = jnp.mean(d * d, axis=1, keepdims=True)
    return d * lax.rsqrt(var + LN_EPS) * g + b


def _rms(o, g):
    return o * lax.rsqrt(jnp.mean(o * o, axis=1, keepdims=True) + RMS_EPS) * g


def _post_kernel(on_ref, of_ref, os_ref, x_ref, go_ref, wo_ref, lg_ref, lb_ref, rw_ref, rb_ref,
                 x1_ref, te_ref, tw_ref, *, alpha):
    n1 = _rms(on_ref[...], go_ref[:, 0:D_NSA]).astype(BF16)
    n2 = _rms(of_ref[...], go_ref[:, D_NSA:D_NSA + D_FOX]).astype(BF16)
    n3 = _rms(os_ref[...], go_ref[:, D_NSA + D_FOX:]).astype(BF16)
    mix = (_dot(n1, wo_ref[0:D_NSA, :]) + _dot(n2, wo_ref[D_NSA:D_NSA + D_FOX, :])
           + _dot(n3, wo_ref[D_NSA + D_FOX:, :]))
    x1 = _layer_norm(alpha * x_ref[...] + mix, lg_ref[...], lb_ref[...])
    x1_ref[...] = x1
    logits = jnp.dot(x1, rw_ref[...], preferred_element_type=F32, precision=lax.Precision.HIGHEST) + rb_ref[...]
    tm, ne = logits.shape
    lane = lax.broadcasted_iota(jnp.int32, (tm, ne), 1).astype(F32)
    wide = lax.broadcasted_iota(jnp.int32, (tm, 128), 1)
    top_e = jnp.zeros((tm, 128), F32)
    top_l = jnp.full((tm, 128), NEG, F32)
    cur = logits
    for k in range(TOP_K):
        mx = jnp.max(cur, axis=1, keepdims=True)
        idx = jnp.min(jnp.where(cur == mx, lane, float(ne)), axis=1, keepdims=True)
        top_e = jnp.where(wide == k, idx, top_e)
        top_l = jnp.where(wide == k, mx, top_l)
        cur = jnp.where(lane == idx, -BIG, cur)
    pe = jnp.exp(top_l - jnp.max(top_l, axis=1, keepdims=True))
    te_ref[...] = top_e.astype(jnp.int32)
    tw_ref[...] = pe / jnp.sum(pe, axis=1, keepdims=True)


def _post(o_nsa, o_fox, o_sb, x2d, g_out, w_out_bf, ln_g, ln_b, router_w, router_b, alpha, tm=256):
    T, D = x2d.shape
    row = lambda w: pl.BlockSpec((tm, w), lambda i: (i, 0))
    full = lambda shape: pl.BlockSpec(shape, lambda i: (0, 0))
    return pl.pallas_call(
        functools.partial(_post_kernel, alpha=alpha),
        grid=(T // tm,),
        in_specs=[row(D_NSA), row(D_FOX), row(D_SB), row(D), full((1, D)), full((D, D)),
                  full((1, D)), full((1, D)), full((D, N_EXPERTS)), full((1, N_EXPERTS))],
        out_specs=[row(D), row(128), row(128)],
        out_shape=[jax.ShapeDtypeStruct((T, D), F32), jax.ShapeDtypeStruct((T, 128), jnp.int32),
                   jax.ShapeDtypeStruct((T, 128), F32)],
        compiler_params=_cparams(("parallel",)),
        name="outproj_ln_router",
    )(o_nsa, o_fox, o_sb, x2d, g_out.reshape(1, D), w_out_bf, ln_g.reshape(1, D), ln_b.reshape(1, D),
      router_w, router_b.reshape(1, N_EXPERTS))


def _row_gather_start(src_hbm, dst, sem, idx_ref, n_rows):
    def issue(r, c):
        pltpu.make_async_copy(src_hbm.at[pl.ds(idx_ref[0, 0, r], 1)], dst.at[pl.ds(r, 1)], sem).start()
        return c
    lax.fori_loop(0, n_rows, issue, 0, unroll=8)


def _row_gather_wait(src_hbm, dst, sem, n_rows):
    pltpu.make_async_copy(src_hbm.at[pl.ds(0, n_rows)], dst, sem).wait()


def _expert_kernel(blk_e_ref, nused_ref, idx_ref, idxn_ref, x_hbm, w_ref, wgu_ref, bgu_ref, wdn_ref, bdn_ref,
                   y_ref, xbuf, wgu_bf, wdn_bf, sem, *, tm, d_ff):
    i = pl.program_id(0)
    n_used = nused_ref[0]
    slot = i % 2

    @pl.when(i == 0)
    def _():
        _row_gather_start(x_hbm, xbuf.at[0], sem.at[0], idx_ref, tm)

    @pl.when(i + 1 < n_used)
    def _():
        _row_gather_start(x_hbm, xbuf.at[1 - slot], sem.at[1 - slot], idxn_ref, tm)

    @pl.when(i < n_used)
    def _():
        changed = (i == 0) | (blk_e_ref[i] != blk_e_ref[jnp.maximum(i - 1, 0)])

        @pl.when(changed)
        def _():
            wgu_bf[...] = wgu_ref[0].astype(BF16)
            wdn_bf[...] = wdn_ref[0].astype(BF16)

        _row_gather_wait(x_hbm, xbuf.at[slot], sem.at[slot], tm)
        xb = xbuf[slot].astype(BF16)
        gu = _dot(xb, wgu_bf[...]) + bgu_ref[0]
        gate = jnp.minimum(gu[:, :d_ff], SWIGLU_LIMIT)
        up = jnp.clip(gu[:, d_ff:], -SWIGLU_LIMIT, SWIGLU_LIMIT)
        act = gate * _sigmoid(SWIGLU_ALPHA * gate) * (up + 1.0)
        y = _dot(act.astype(BF16), wdn_bf[...]) + bdn_ref[0]
        y_ref[...] = y * w_ref[0]

    @pl.when(i >= n_used)
    def _():
        y_ref[...] = jnp.zeros_like(y_ref)


def _experts(x1, blk_e, n_used, buf_tok, buf_w, w_gu, b_gu, w_dn, b_dn, tm):
    T, D = x1.shape
    ne, _, d2 = w_gu.shape
    d_ff = d2 // 2
    n_blocks = blk_e.shape[0]
    idx = buf_tok.reshape(n_blocks, 1, tm)
    wcol = buf_w.reshape(n_blocks, tm, 1)
    last = n_blocks - 1
    grid_spec = pltpu.PrefetchScalarGridSpec(
        num_scalar_prefetch=2,
        grid=(n_blocks,),
        in_specs=[pl.BlockSpec((1, 1, tm), lambda i, e, n: (i, 0, 0), memory_space=pltpu.SMEM),
                  pl.BlockSpec((1, 1, tm), lambda i, e, n: (jnp.minimum(i + 1, last), 0, 0),
                               memory_space=pltpu.SMEM),
                  pl.BlockSpec(memory_space=pl.ANY),
                  pl.BlockSpec((1, tm, 1), lambda i, e, n: (i, 0, 0)),
                  pl.BlockSpec((1, D, d2), lambda i, e, n: (e[i], 0, 0)),
                  pl.BlockSpec((1, 1, d2), lambda i, e, n: (e[i], 0, 0)),
                  pl.BlockSpec((1, d_ff, D), lambda i, e, n: (e[i], 0, 0)),
                  pl.BlockSpec((1, 1, D), lambda i, e, n: (e[i], 0, 0))],
        out_specs=pl.BlockSpec((tm, D), lambda i, e, n: (i, 0)),
        scratch_shapes=[pltpu.VMEM((2, tm, D), F32), pltpu.VMEM((D, d2), BF16), pltpu.VMEM((d_ff, D), BF16),
                        pltpu.SemaphoreType.DMA((2,))],
    )
    return pl.pallas_call(
        functools.partial(_expert_kernel, tm=tm, d_ff=d_ff),
        grid_spec=grid_spec,
        out_shape=jax.ShapeDtypeStruct((n_blocks * tm, D), F32),
        compiler_params=_cparams(("arbitrary",), 58 * 1024 * 1024),
        name="moe_experts",
    )(blk_e, n_used, idx, idx, x1, wcol, w_gu, b_gu.reshape(ne, 1, d2), w_dn, b_dn.reshape(ne, 1, D))


def _combine_kernel(idx_ref, idxn_ref, y_hbm, x_ref, lg_ref, lb_ref, o_ref, ybuf, sem, *, tm, alpha):
    i = pl.program_id(0)
    n = pl.num_programs(0)
    slot = i % 2
    rows = TOP_K * tm

    @pl.when(i == 0)
    def _():
        _row_gather_start(y_hbm, ybuf.at[0], sem.at[0], idx_ref, rows)

    @pl.when(i + 1 < n)
    def _():
        _row_gather_start(y_hbm, ybuf.at[1 - slot], sem.at[1 - slot], idxn_ref, rows)

    _row_gather_wait(y_hbm, ybuf.at[slot], sem.at[slot], rows)
    ffn = ybuf[slot, 0:tm]
    for k in range(1, TOP_K):
        ffn = ffn + ybuf[slot, k * tm:(k + 1) * tm]
    o_ref[...] = _layer_norm(alpha * x_ref[...] + ffn, lg_ref[...], lb_ref[...])


def _combine(y_sorted, pos, x1, ln_g, ln_b, alpha, tm=128):
    T, D = x1.shape
    nt = T // tm
    idx = pos.reshape(nt, tm, TOP_K).transpose(0, 2, 1).reshape(nt, 1, TOP_K * tm)
    last = nt - 1
    return pl.pallas_call(
        functools.partial(_combine_kernel, tm=tm, alpha=alpha),
        grid=(nt,),
        in_specs=[pl.BlockSpec((1, 1, TOP_K * tm), lambda i: (i, 0, 0), memory_space=pltpu.SMEM),
                  pl.BlockSpec((1, 1, TOP_K * tm), lambda i: (jnp.minimum(i + 1, last), 0, 0),
                               memory_space=pltpu.SMEM),
                  pl.BlockSpec(memory_space=pl.ANY),
                  pl.BlockSpec((tm, D), lambda i: (i, 0)),
                  pl.BlockSpec((1, D), lambda i: (0, 0)),
                  pl.BlockSpec((1, D), lambda i: (0, 0))],
        out_specs=pl.BlockSpec((tm, D), lambda i: (i, 0)),
        out_shape=jax.ShapeDtypeStruct((T, D), F32),
        scratch_shapes=[pltpu.VMEM((2, TOP_K * tm, D), F32), pltpu.SemaphoreType.DMA((2,))],
        compiler_params=_cparams(("arbitrary",)),
        name="moe_combine_ln",
    )(idx, idx, y_sorted, x1, ln_g.reshape(1, D), ln_b.reshape(1, D))


def _route(top_e, top_w, tm):
    T = top_e.shape[0]
    M = T * TOP_K
    flat_e = top_e.reshape(M)
    flat_w = top_w.reshape(M)
    order = jnp.argsort(flat_e)
    sorted_e = flat_e[order]
    counts = jnp.bincount(flat_e, length=N_EXPERTS)
    start = jnp.cumsum(counts) - counts
    padded = (counts + tm - 1) // tm * tm
    pad_end = jnp.cumsum(padded)
    pad_start = pad_end - padded
    dest = (pad_start[sorted_e] + jnp.arange(M, dtype=jnp.int32) - start[sorted_e]).astype(jnp.int32)
    n_blocks = -(-M // tm) + N_EXPERTS
    P = n_blocks * tm
    buf_tok = jnp.zeros((P,), jnp.int32).at[dest].set((order // TOP_K).astype(jnp.int32))
    buf_w = jnp.zeros((P,), F32).at[dest].set(flat_w[order])
    pos = jnp.zeros((M,), jnp.int32).at[order].set(dest)
    blk_e = jnp.minimum(jnp.searchsorted(pad_end, jnp.arange(n_blocks) * tm, side='right'),
                        N_EXPERTS - 1).astype(jnp.int32)
    n_used = (pad_end[-1] // tm).astype(jnp.int32).reshape(1)
    return blk_e, n_used, buf_tok, buf_w, pos


def _proj_layout():
    src = np.zeros((D_IN_PAD,), np.int64)
    scale = np.zeros((D_IN_PAD,), np.float32)
    off_gate_src = 512 + 768
    off_fox_src = off_gate_src + 24
    off_logf_src = off_fox_src + 768
    off_sb_src = off_logf_src + 4
    qs = HEAD_DIM ** -0.5

    def put(dst, s0, n, sc=1.0):
        src[dst:dst + n] = np.arange(s0, s0 + n)
        scale[dst:dst + n] = sc

    put(0, 0, 512, qs)
    put(OFF_KV, 512, 768)
    put(OFF_FOX, off_fox_src, 256, qs)
    put(OFF_FOX + 256, off_fox_src + 256, 512)
    put(OFF_SB, off_sb_src, 256, qs)
    put(OFF_SB + 256, off_sb_src + 256, 512)
    put(OFF_GATE, off_gate_src, 24)
    put(OFF_LOGF, off_logf_src, 4)
    return src, scale


def _heads(p, b, s, n):
    return p.reshape(b, s, n, HEAD_DIM).transpose(0, 2, 1, 3)


def _mixer_heads(proj, B, S):
    kv = proj[..., OFF_KV:OFF_FOX].reshape(B, S, 3, 2, N_NSA_KV, HEAD_DIM).transpose(2, 3, 0, 4, 1, 5)
    fox = proj[..., OFF_FOX:OFF_SB].reshape(B, S, 3, N_FOX_HEADS, HEAD_DIM).transpose(2, 0, 3, 1, 4)
    sb = proj[..., OFF_SB:OFF_GATE].reshape(B, S, 3, N_SB_HEADS, HEAD_DIM).transpose(2, 0, 3, 1, 4)
    gates = proj[..., OFF_GATE:OFF_LOGF].reshape(B, S, 3, N_NSA_HEADS).transpose(2, 0, 3, 1)[..., None]
    q_nsa = _heads(proj[..., 0:D_NSA], B, S, N_NSA_HEADS).astype(BF16)
    return q_nsa, kv, fox.astype(BF16), sb.astype(BF16), gates


def _layer(x, consts, w_in, b_in, pos_k, pos_v, w1k, w2k, w1v, w2v, g_out, w_out, ln1_g, ln1_b,
           router_w, router_b, w_gu, b_gu, w_dn, b_dn, ln2_g, ln2_b, alpha, moe_tm):
    B, S, D = x.shape
    T = B * S
    cs = consts
    w_p = (w_in[:, cs["src"]] * cs["scale"]).astype(BF16)
    b_p = (b_in[cs["src"]] * cs["scale"]).reshape(1, D_IN_PAD)
    x2d = x.reshape(T, D)
    proj = _proj(x2d, w_p, b_p).reshape(B, S, D_IN_PAD)
    q_nsa, kv, fox, sb, gates = _mixer_heads(proj, B, S)

    def aug(a, extra):
        extra = jnp.broadcast_to(extra, a.shape[:-1] + extra.shape[-1:])
        return jnp.concatenate([a.astype(BF16), extra.astype(BF16)], axis=-1)

    k_cmp, v_cmp = _compress(kv[0, 0], kv[0, 1], pos_k, pos_v, w1k, w2k, w1v, w2v)
    o_c, msel = _nsa_cmp(cs["slopes"], q_nsa, k_cmp, v_cmp, cs["ovl_t"], gates[0])
    q2 = aug(q_nsa, cs["slope_aug"][None, :, None, :])
    o_s = _nsa_sel(q2, msel, aug(kv[1, 0], cs["pos_block_aug"]), aug(kv[1, 1], cs["one_aug"]), gates[1],
                   cs["sel_kb"])
    o_w = _nsa_win(q2, aug(kv[2, 0], cs["pos_aug"]), aug(kv[2, 1], cs["one_aug"]), gates[2])

    log_f = jax.nn.log_sigmoid(proj[..., OFF_LOGF:OFF_LOGF + N_FOX_HEADS]).transpose(0, 2, 1)
    c = jnp.cumsum(log_f, axis=-1)
    c_hi = _trunc_bf16(c)
    c_mid = _trunc_bf16(c - c_hi)
    c_lo = c - c_hi - c_mid
    c_aug = jnp.pad(-jnp.stack([c_hi, c_mid, c_lo], axis=-1), ((0, 0), (0, 0), (0, 0), (0, HEAD_DIM - 3)))
    o_fox = _fox(aug(fox[0], cs["three_aug"]), aug(fox[1], c_aug), aug(fox[2], cs["one_aug"]), tq=min(512, S),
                 kb=min(256, S))
    o_sb = _sb(sb[0], sb[1], sb[2])

    def tokens(o):
        b, h, s, d = o.shape
        return o.transpose(0, 2, 1, 3).reshape(b * s, h * d)

    x1, te, tw = _post(tokens(o_c + o_s + o_w), tokens(o_fox), tokens(o_sb), x2d, g_out, w_out.astype(BF16),
                       ln1_g, ln1_b, router_w, router_b, alpha)
    blk_e, n_used, buf_tok, buf_w, pos = _route(te[:, :TOP_K], tw[:, :TOP_K], moe_tm)
    y_sorted = _experts(x1, blk_e, n_used, buf_tok, buf_w, w_gu, b_gu, w_dn, b_dn, moe_tm)
    out = _combine(y_sorted, pos, x1, ln2_g, ln2_b, alpha)
    return out.reshape(B, S, D)


def _constants(S, sel_kb):
    src, scale = _proj_layout()
    slopes = jnp.exp2(-8.0 * jnp.arange(1, N_NSA_HEADS + 1, dtype=F32) / N_NSA_HEADS)
    nc = S // CMP_STRIDE
    nb = S // SEL_LEN
    n_cmp = (S - CMP_LEN) // CMP_STRIDE + 1
    cmp_idx = np.arange(nc)[:, None] * CMP_STRIDE + np.arange(CMP_LEN)[None, :]
    ovl = (cmp_idx[:, :, None] // SEL_LEN == np.arange(nb)[None, None, :]).astype(np.float32).mean(axis=1)
    ovl[n_cmp:] = 0.0
    ovl_t = jnp.asarray(ovl.T, dtype=BF16)
    pos = np.arange(S)
    pos_aug = np.zeros((S, HEAD_DIM), np.float32)
    pos_aug[:, 0] = pos // SEL_LEN * SEL_LEN
    pos_aug[:, 1] = pos % SEL_LEN
    block_onehot = (pos[:, None] // SEL_LEN == np.arange(nb)[None, :]).astype(np.float32)
    slope_np = 2.0 ** (-8.0 * np.arange(1, N_NSA_HEADS + 1) / N_NSA_HEADS)
    slope_aug = np.zeros((N_NSA_HEADS, HEAD_DIM), np.float32)
    slope_aug[:, 0] = slope_np
    slope_aug[:, 1] = slope_np
    one_aug = np.zeros((HEAD_DIM,), np.float32)
    one_aug[0] = 1.0
    three_aug = np.zeros((HEAD_DIM,), np.float32)
    three_aug[0:3] = 1.0
    return dict(src=jnp.asarray(src), scale=jnp.asarray(scale), slopes=slopes, ovl_t=ovl_t, sel_kb=sel_kb,
                pos_aug=jnp.asarray(pos_aug), pos_block_aug=jnp.asarray(np.concatenate([pos_aug, block_onehot], 1)),
                slope_aug=jnp.asarray(slope_aug), one_aug=jnp.asarray(one_aug), three_aug=jnp.asarray(three_aug))


def kernel(x, w_in, b_in, cmp_pos_k, cmp_pos_v, cmp_w1_k, cmp_w2_k, cmp_w1_v, cmp_w2_v, g_out, w_out,
           ln1_g, ln1_b, router_w, router_b, w_gate_up, b_gate_up, w_down, b_down, ln2_g, ln2_b):
    depth = w_in.shape[0]
    S = x.shape[1]
    alpha = (2 * depth) ** 0.25
    consts = _constants(S, min(256, S))
    for l in range(depth):
        x = _layer(x, consts, w_in[l], b_in[l], cmp_pos_k[l], cmp_pos_v[l], cmp_w1_k[l], cmp_w2_k[l],
                   cmp_w1_v[l], cmp_w2_v[l], g_out[l], w_out[l], ln1_g[l], ln1_b[l], router_w[l], router_b[l],
                   w_gate_up[l], b_gate_up[l], w_down[l], b_down[l], ln2_g[l], ln2_b[l], alpha, 256)
    return x
```

```python
import functools

import jax
import jax.numpy as jnp
import numpy as np
from jax import lax
from jax.experimental import pallas as pl
from jax.experimental.pallas import tpu as pltpu

F32 = jnp.float32
BF16 = jnp.bfloat16

HEAD_DIM = 64
N_NSA_HEADS = 8
N_NSA_KV = 2
NSA_REP = 4
N_FOX_HEADS = 4
N_SB_HEADS = 4
D_NSA = 512
D_FOX = 256
D_SB = 256
CMP_LEN = 32
CMP_STRIDE = 16
CMP_HIDDEN = 128
SEL_LEN = 64
SEL_TOPK = 16
WINDOW = 512
Q_BLOCK = 128
N_EXPERTS = 32
TOP_K = 4
SWIGLU_LIMIT = 7.0
SWIGLU_ALPHA = 1.702
LN_EPS = 1e-5
RMS_EPS = 1e-6
NEG = -1e30
BIG = 3e38
SB_CUTOFF = -90.0

D_IN_PAD = 3072
OFF_KV = 512
OFF_FOX = 1280
OFF_SB = 2048
OFF_GATE = 2816
OFF_LOGF = 2840

VMEM_LIMIT = 52 * 1024 * 1024


def _cparams(sem, vmem=VMEM_LIMIT):
    return pltpu.CompilerParams(dimension_semantics=sem, vmem_limit_bytes=vmem)


def _dot(a, b):
    return jnp.dot(a, b, preferred_element_type=F32)


def _dot_nt(a, b):
    return lax.dot_general(a, b, (((1,), (1,)), ((), ())), preferred_element_type=F32)


def _split_bf16(x):
    hi = x.astype(BF16)
    lo = (x - hi.astype(F32)).astype(BF16)
    return hi, lo


def _trunc_bf16(x):
    bits = lax.bitcast_convert_type(x, jnp.uint32) & jnp.uint32(0xFFFF0000)
    return lax.bitcast_convert_type(bits, F32)


def _sigmoid(x):
    return 1.0 / (1.0 + jnp.exp(-x))


def _proj_kernel(x_ref, w_ref, b_ref, o_ref):
    o_ref[...] = _dot(x_ref[...].astype(BF16), w_ref[...]) + b_ref[...]


def _proj(x2d, w_bf, b_row, tm=512):
    T, D = x2d.shape
    N = w_bf.shape[1]
    return pl.pallas_call(
        _proj_kernel,
        grid=(T // tm,),
        in_specs=[pl.BlockSpec((tm, D), lambda i: (i, 0)),
                  pl.BlockSpec((D, N), lambda i: (0, 0)),
                  pl.BlockSpec((1, N), lambda i: (0, 0))],
        out_specs=pl.BlockSpec((tm, N), lambda i: (i, 0)),
        out_shape=jax.ShapeDtypeStruct((T, N), F32),
        compiler_params=_cparams(("parallel",)),
        name="proj",
    )(x2d, w_bf, b_row)


def _gelu_tanh(x):
    return 0.5 * x * (1.0 + jnp.tanh(0.7978845608028654 * (x + 0.044715 * (x * x * x))))


def _compress_kernel(k_ref, v_ref, pk_ref, pv_ref, w1k_ref, w2k_ref, w1v_ref, w2v_ref, ok_ref, ov_ref):
    nc = k_ref.shape[2]
    half = CMP_STRIDE * HEAD_DIM

    def one(r_ref, p_ref, w1_ref, w2_ref, o_ref):
        r = r_ref[0, 0]
        a = (r + p_ref[0:1, :]).astype(BF16)
        b = (r + p_ref[1:2, :]).astype(BF16)
        ha = _dot(a, w1_ref[0:half, :])
        hb = _dot(b, w1_ref[half:2 * half, :])
        hid = _gelu_tanh(ha + pltpu.roll(hb, nc - 1, 0))
        o_ref[0, 0] = _dot(hid.astype(BF16), w2_ref[...]).astype(o_ref.dtype)

    one(k_ref, pk_ref, w1k_ref, w2k_ref, ok_ref)
    one(v_ref, pv_ref, w1v_ref, w2v_ref, ov_ref)


def _compress(kc, vc, pos_k, pos_v, w1k, w2k, w1v, w2v):
    B, G, S, dh = kc.shape
    nc = S // CMP_STRIDE
    width = CMP_STRIDE * dh
    kr = kc.reshape(B, G, nc, width)
    vr = vc.reshape(B, G, nc, width)
    pk = pos_k.reshape(2, width)
    pv = pos_v.reshape(2, width)
    kv_spec = pl.BlockSpec((1, 1, nc, width), lambda b, g: (b, g, 0, 0))
    full = lambda shape: pl.BlockSpec(shape, lambda b, g: tuple(0 for _ in shape))
    out_spec = pl.BlockSpec((1, 1, nc, dh), lambda b, g: (b, g, 0, 0))
    return pl.pallas_call(
        _compress_kernel,
        grid=(B, G),
        in_specs=[kv_spec, kv_spec, full((2, width)), full((2, width)),
                  full((2 * width, CMP_HIDDEN)), full((CMP_HIDDEN, dh)),
                  full((2 * width, CMP_HIDDEN)), full((CMP_HIDDEN, dh))],
        out_specs=[out_spec, out_spec],
        out_shape=[jax.ShapeDtypeStruct((B, G, nc, dh), BF16)] * 2,
        compiler_params=_cparams(("parallel", "parallel")),
        name="nsa_compress",
    )(kr, vr, pk, pv, w1k.astype(BF16), w2k.astype(BF16), w1v.astype(BF16), w2v.astype(BF16))


def _nsa_cmp_kernel(slopes_ref, q_ref, kc_ref, vc_ref, ovl_ref, g_ref, o_ref, sel_ref, *, n_sel):
    g = pl.program_id(1)
    i = pl.program_id(2)
    q0 = i * Q_BLOCK
    nc = kc_ref.shape[2]
    nb = ovl_ref.shape[0]
    kc = kc_ref[0, 0]
    vc = vc_ref[0, 0]
    t = q0 + lax.broadcasted_iota(jnp.int32, (Q_BLOCK, nc), 0)
    cmp_end = lax.broadcasted_iota(jnp.int32, (Q_BLOCK, nc), 1) * CMP_STRIDE + (CMP_LEN - 1)
    mask = cmp_end <= t
    end_rel = (lax.broadcasted_iota(jnp.int32, (1, nc), 1) * CMP_STRIDE + (CMP_LEN - 1) - q0).astype(F32)
    psum = jnp.zeros((Q_BLOCK, nc), F32)
    for r in range(NSA_REP):
        slope = slopes_ref[g * NSA_REP + r]
        s = _dot_nt(q_ref[0, r], kc) + slope * end_rel
        s = jnp.where(mask, s, NEG)
        m = jnp.max(s, axis=1, keepdims=True)
        p = jnp.where(mask, jnp.exp(s - m), 0.0)
        l = jnp.sum(p, axis=1, keepdims=True)
        pn = p / jnp.where(l > 0.0, l, 1.0)
        o_ref[0, r] = _dot(pn.astype(BF16), vc) * _sigmoid(g_ref[0, r])
        psum = psum + pn
    hi, lo = _split_bf16(psum)
    ovl = ovl_ref[...]
    imp = _dot_nt(ovl, hi) + _dot_nt(ovl, lo)
    jf = lax.broadcasted_iota(jnp.int32, (nb, Q_BLOCK), 0)
    tq = q0 + lax.broadcasted_iota(jnp.int32, (nb, Q_BLOCK), 1)
    cur = tq >> 6
    valid = jf <= cur
    forced = valid & ((jf == 0) | (jf >= cur - 1))
    v = jnp.where(forced, BIG, jnp.where(valid, imp, -BIG))
    jff = jf.astype(F32)
    picked = jnp.zeros((nb, Q_BLOCK), F32)
    for _ in range(n_sel):
        mx = jnp.max(v, axis=0, keepdims=True)
        idx = jnp.min(jnp.where(v == mx, jff, float(nb)), axis=0, keepdims=True)
        hit = jff == idx
        picked = jnp.where(hit, 1.0, picked)
        v = jnp.where(hit, -BIG, v)
    msel = jnp.where(valid & (picked > 0.5), 0.0, NEG)
    sel_ref[0, 0, 0] = msel.T.astype(sel_ref.dtype)


def _nsa_cmp(slopes, q, kcmp, vcmp, ovl_t, gate):
    B, H, S, dh = q.shape
    G = N_NSA_KV
    nq = S // Q_BLOCK
    nc = kcmp.shape[2]
    nb = S // SEL_LEN
    n_sel = min(SEL_TOPK, nb)
    grid_spec = pltpu.PrefetchScalarGridSpec(
        num_scalar_prefetch=1,
        grid=(B, G, nq),
        in_specs=[pl.BlockSpec((1, NSA_REP, Q_BLOCK, dh), lambda b, g, i, s: (b, g, i, 0)),
                  pl.BlockSpec((1, 1, nc, dh), lambda b, g, i, s: (b, g, 0, 0)),
                  pl.BlockSpec((1, 1, nc, dh), lambda b, g, i, s: (b, g, 0, 0)),
                  pl.BlockSpec((nb, nc), lambda b, g, i, s: (0, 0)),
                  pl.BlockSpec((1, NSA_REP, Q_BLOCK, 1), lambda b, g, i, s: (b, g, i, 0))],
        out_specs=[pl.BlockSpec((1, NSA_REP, Q_BLOCK, dh), lambda b, g, i, s: (b, g, i, 0)),
                   pl.BlockSpec((1, 1, 1, Q_BLOCK, nb), lambda b, g, i, s: (b, g, i, 0, 0))],
    )
    return pl.pallas_call(
        functools.partial(_nsa_cmp_kernel, n_sel=n_sel),
        grid_spec=grid_spec,
        out_shape=[jax.ShapeDtypeStruct((B, H, S, dh), F32),
                   jax.ShapeDtypeStruct((B, G, nq, Q_BLOCK, nb), BF16)],
        compiler_params=_cparams(("parallel", "parallel", "parallel")),
        name="nsa_cmp_select",
    )(slopes, q, kcmp, vcmp, ovl_t, gate)


ACC_W = 128


def _flash_init(rows):
    return jnp.full((rows, 1), NEG, F32), jnp.zeros((rows, ACC_W), F32)


def _flash_step(carry, qa, k_tile, v_tile, mask=None):
    m, acc = carry
    s = _dot_nt(qa, k_tile)
    if mask is not None:
        s = jnp.where(mask, s, NEG)
    m_new = jnp.maximum(m, jnp.max(s, axis=1, keepdims=True))
    p = jnp.exp(s - m_new)
    acc = jnp.exp(m - m_new) * acc + _dot(p.astype(BF16), v_tile)
    return m_new, acc


def _flash_update(carry, s, v_tile):
    m, acc = carry
    m_new = jnp.maximum(m, jnp.max(s, axis=1, keepdims=True))
    p = jnp.exp(s - m_new)
    acc = jnp.exp(m - m_new) * acc + _dot(p.astype(BF16), v_tile)
    return m_new, acc


def _flash_out(acc):
    return acc[:, 0:HEAD_DIM] / acc[:, HEAD_DIM:HEAD_DIM + 1]


def _stacked_rows(rows, kb):
    row = lax.broadcasted_iota(jnp.int32, (rows, kb), 0) & (Q_BLOCK - 1)
    col = lax.broadcasted_iota(jnp.int32, (rows, kb), 1)
    return row, col


def _nsa_sel_kernel(q_ref, msel_ref, k_ref, v_ref, g_ref, o_ref, qa_ref, sa_ref, sb_ref, *, kb):
    i = pl.program_id(2)
    q0 = i * Q_BLOCK
    rows = NSA_REP * Q_BLOCK
    msel = msel_ref[0, 0, 0]
    for r in range(NSA_REP):
        qa_ref[r * Q_BLOCK:(r + 1) * Q_BLOCK, :] = jnp.concatenate([q_ref[0, r], msel], axis=1)
    qa = qa_ref[...]
    last = (q0 + Q_BLOCK - 1) // kb

    def logits(kt):
        k0 = pl.multiple_of(kt * kb, kb)
        return _dot_nt(qa, k_ref[0, 0, pl.ds(k0, kb), :])

    def v_tile(kt):
        return v_ref[0, 0, pl.ds(pl.multiple_of(kt * kb, kb), kb), :]

    sa_ref[...] = logits(0)

    def body(j, carry):
        sb_ref[...] = logits(2 * j + 1)
        carry = _flash_update(carry, sa_ref[...], v_tile(2 * j))
        sa_ref[...] = logits(2 * j + 2)
        return _flash_update(carry, sb_ref[...], v_tile(2 * j + 1))

    pairs = last // 2
    carry = lax.fori_loop(0, pairs, body, _flash_init(rows))
    row, col = _stacked_rows(rows, kb)
    causal = (last * kb + col) <= (q0 + row)

    def tail_odd(carry):
        carry = _flash_update(carry, sa_ref[...], v_tile(last - 1))
        return _flash_update(carry, jnp.where(causal, logits(last), NEG), v_tile(last))

    def tail_even(carry):
        return _flash_update(carry, jnp.where(causal, sa_ref[...], NEG), v_tile(last))

    _, acc = lax.cond(last - 2 * pairs == 1, tail_odd, tail_even, carry)
    o = _flash_out(acc).reshape(NSA_REP, Q_BLOCK, HEAD_DIM)
    o_ref[0] = o * _sigmoid(g_ref[0])


def _nsa_sel(q2, msel, k2, v2, gate, kb):
    B, H, S, qw = q2.shape
    G = N_NSA_KV
    nq = S // Q_BLOCK
    nb = msel.shape[-1]
    kw = k2.shape[-1]
    return pl.pallas_call(
        functools.partial(_nsa_sel_kernel, kb=kb),
        grid=(B, G, nq),
        in_specs=[pl.BlockSpec((1, NSA_REP, Q_BLOCK, qw), lambda b, g, i: (b, g, i, 0)),
                  pl.BlockSpec((1, 1, 1, Q_BLOCK, nb), lambda b, g, i: (b, g, i, 0, 0)),
                  pl.BlockSpec((1, 1, S, kw), lambda b, g, i: (b, g, 0, 0)),
                  pl.BlockSpec((1, 1, S, ACC_W), lambda b, g, i: (b, g, 0, 0)),
                  pl.BlockSpec((1, NSA_REP, Q_BLOCK, 1), lambda b, g, i: (b, g, i, 0))],
        out_specs=pl.BlockSpec((1, NSA_REP, Q_BLOCK, HEAD_DIM), lambda b, g, i: (b, g, i, 0)),
        out_shape=jax.ShapeDtypeStruct((B, H, S, HEAD_DIM), F32),
        scratch_shapes=[pltpu.VMEM((NSA_REP * Q_BLOCK, qw + nb), BF16),
                        pltpu.VMEM((NSA_REP * Q_BLOCK, kb), F32),
                        pltpu.VMEM((NSA_REP * Q_BLOCK, kb), F32)],
        compiler_params=_cparams(("parallel", "parallel", "parallel")),
        name="nsa_selected",
    )(q2, msel, k2, v2, gate)


def _nsa_win_kernel(q_ref, k_ref, v_ref, g_ref, o_ref, *, kb):
    i = pl.program_id(2)
    q0 = i * Q_BLOCK
    rows = NSA_REP * Q_BLOCK
    qa = q_ref[0].reshape(rows, q_ref.shape[-1])
    assert kb == Q_BLOCK
    n_back = WINDOW // kb
    row, col = _stacked_rows(rows, kb)
    carry = _flash_init(rows)
    for d in range(n_back + 1):
        kt = i - n_back + d
        k0 = pl.multiple_of(jnp.maximum(kt, 0) * kb, kb)
        s = _dot_nt(qa, k_ref[0, 0, pl.ds(k0, kb), :])
        if d == 0:
            s = jnp.where((col > row) & (kt >= 0), s, NEG)
        elif d == n_back:
            s = jnp.where(col <= row, s, NEG)
        else:
            s = jnp.where(kt >= 0, s, NEG)
        carry = _flash_update(carry, s, v_ref[0, 0, pl.ds(k0, kb), :])
    acc = carry[1]
    o = _flash_out(acc).reshape(NSA_REP, Q_BLOCK, HEAD_DIM)
    o_ref[0] = o * _sigmoid(g_ref[0])


def _nsa_win(q2, k2, v2, gate, kb=128):
    B, H, S, qw = q2.shape
    G = N_NSA_KV
    nq = S // Q_BLOCK
    return pl.pallas_call(
        functools.partial(_nsa_win_kernel, kb=kb),
        grid=(B, G, nq),
        in_specs=[pl.BlockSpec((1, NSA_REP, Q_BLOCK, qw), lambda b, g, i: (b, g, i, 0)),
                  pl.BlockSpec((1, 1, S, qw), lambda b, g, i: (b, g, 0, 0)),
                  pl.BlockSpec((1, 1, S, ACC_W), lambda b, g, i: (b, g, 0, 0)),
                  pl.BlockSpec((1, NSA_REP, Q_BLOCK, 1), lambda b, g, i: (b, g, i, 0))],
        out_specs=pl.BlockSpec((1, NSA_REP, Q_BLOCK, HEAD_DIM), lambda b, g, i: (b, g, i, 0)),
        out_shape=jax.ShapeDtypeStruct((B, H, S, HEAD_DIM), F32),
        compiler_params=_cparams(("parallel", "parallel", "parallel")),
        name="nsa_window",
    )(q2, k2, v2, gate)


def _fox_kernel(q_ref, k_ref, v_ref, o_ref, sa_ref, sb_ref, *, tq, kb):
    i = pl.program_id(2)
    q0 = i * tq
    qa = q_ref[0, 0]
    assert tq == 2 * kb

    def logits(kt):
        k0 = pl.multiple_of(kt * kb, kb)
        return _dot_nt(qa, k_ref[0, 0, pl.ds(k0, kb), :])

    def v_tile(kt):
        return v_ref[0, 0, pl.ds(pl.multiple_of(kt * kb, kb), kb), :]

    sa_ref[...] = logits(0)

    def body(j, carry):
        sb_ref[...] = logits(2 * j + 1)
        carry = _flash_update(carry, sa_ref[...], v_tile(2 * j))
        sa_ref[...] = logits(2 * j + 2)
        return _flash_update(carry, sb_ref[...], v_tile(2 * j + 1))

    carry = lax.fori_loop(0, i, body, _flash_init(tq))
    row = lax.broadcasted_iota(jnp.int32, (tq, kb), 0)
    col = lax.broadcasted_iota(jnp.int32, (tq, kb), 1)
    s1 = logits(2 * i + 1)
    carry = _flash_update(carry, jnp.where(col <= row, sa_ref[...], NEG), v_tile(2 * i))
    carry = _flash_update(carry, jnp.where(kb + col <= row, s1, NEG), v_tile(2 * i + 1))
    o_ref[0, 0] = _flash_out(carry[1])


def _fox(q2, k2, v2, tq, kb):
    B, H, S, qw = q2.shape
    return pl.pallas_call(
        functools.partial(_fox_kernel, tq=tq, kb=kb),
        grid=(B, H, S // tq),
        in_specs=[pl.BlockSpec((1, 1, tq, qw), lambda b, h, i: (b, h, i, 0)),
                  pl.BlockSpec((1, 1, S, qw), lambda b, h, i: (b, h, 0, 0)),
                  pl.BlockSpec((1, 1, S, ACC_W), lambda b, h, i: (b, h, 0, 0))],
        out_specs=pl.BlockSpec((1, 1, tq, HEAD_DIM), lambda b, h, i: (b, h, i, 0)),
        out_shape=jax.ShapeDtypeStruct((B, H, S, HEAD_DIM), F32),
        scratch_shapes=[pltpu.VMEM((tq, kb), F32), pltpu.VMEM((tq, kb), F32)],
        compiler_params=_cparams(("parallel", "parallel", "parallel")),
        name="fox_attention",
    )(q2, k2, v2)


def _logf_cumsum_kernel(x_ref, c_ref):
    x = x_ref[0]
    nr, nl = x.shape
    log_f = jnp.minimum(x, 0.0) - jnp.log1p(jnp.exp(-jnp.abs(x)))
    hp = lax.Precision.HIGHEST
    incl = (lax.broadcasted_iota(jnp.int32, (nl, nl), 0) <= lax.broadcasted_iota(jnp.int32, (nl, nl), 1))
    within = jnp.dot(log_f, incl.astype(F32), preferred_element_type=F32, precision=hp)
    totals = jnp.broadcast_to(within[:, nl - 1:nl], (nr, nl))
    before = (lax.broadcasted_iota(jnp.int32, (nr, nr), 1) < lax.broadcasted_iota(jnp.int32, (nr, nr), 0))
    c_ref[0] = within + jnp.dot(before.astype(F32), totals, preferred_element_type=F32, precision=hp)


def _logf_cumsum(logits):
    B, H, S = logits.shape
    nl = 128
    x = logits.reshape(B * H, S // nl, nl)
    c = pl.pallas_call(
        _logf_cumsum_kernel,
        grid=(B * H,),
        in_specs=[pl.BlockSpec((1, S // nl, nl), lambda i: (i, 0, 0))],
        out_specs=pl.BlockSpec((1, S // nl, nl), lambda i: (i, 0, 0)),
        out_shape=jax.ShapeDtypeStruct(x.shape, F32),
        compiler_params=_cparams(("parallel",)),
        name="logf_cumsum",
    )(x)
    return c.reshape(B, H, S)


def _sb_kernel(q_ref, k_ref, v_ref, o_ref, *, tq, nh):
    i = pl.program_id(2)
    q0 = i * tq
    trow = q0 + lax.broadcasted_iota(jnp.int32, (tq, tq), 0)
    col = lax.broadcasted_iota(jnp.int32, (tq, tq), 1)
    rr = lax.broadcasted_iota(jnp.int32, (tq, tq), 0)
    upper = (rr > col).astype(BF16)

    def cond(state):
        kt, carries, _ = state
        alive = jnp.max(carries[0])
        for h in range(1, nh):
            alive = jnp.maximum(alive, jnp.max(carries[h]))
        return (kt >= 0) & (alive > SB_CUTOFF)

    def body(state):
        kt, carries, accs = state
        k0 = pl.multiple_of(kt * tq, tq)
        strict = (k0 + col) < trow
        new_c, new_a = [], []
        for h in range(nh):
            z = _dot_nt(q_ref[0, h], k_ref[0, h, pl.ds(k0, tq), :])
            log_beta = jnp.minimum(z, 0.0) - jnp.log1p(jnp.exp(-jnp.abs(z)))
            log_keep = jnp.where(strict, log_beta - z, 0.0)
            hi, lo = _split_bf16(log_keep)
            later = _dot(hi, upper) + _dot(lo, upper)
            a = jnp.where(strict, jnp.exp(log_beta + later + carries[h]), 0.0)
            new_a.append(accs[h] + _dot(a.astype(BF16), v_ref[0, h, pl.ds(k0, tq), :]))
            new_c.append(carries[h] + jnp.sum(log_keep, axis=1, keepdims=True))
        return kt - 1, tuple(new_c), tuple(new_a)

    state = (i, tuple(jnp.zeros((tq, 1), F32) for _ in range(nh)),
             tuple(jnp.zeros((tq, HEAD_DIM), F32) for _ in range(nh)))
    _, _, accs = lax.while_loop(cond, body, state)
    for h in range(nh):
        o_ref[0, h] = accs[h]


def _sb(q, k, v, tq=128, nh=4):
    B, H, S, dh = q.shape
    return pl.pallas_call(
        functools.partial(_sb_kernel, tq=tq, nh=nh),
        grid=(B, H // nh, S // tq),
        in_specs=[pl.BlockSpec((1, nh, tq, dh), lambda b, h, i: (b, h, i, 0)),
                  pl.BlockSpec((1, nh, S, dh), lambda b, h, i: (b, h, 0, 0)),
                  pl.BlockSpec((1, nh, S, dh), lambda b, h, i: (b, h, 0, 0))],
        out_specs=pl.BlockSpec((1, nh, tq, dh), lambda b, h, i: (b, h, i, 0)),
        out_shape=jax.ShapeDtypeStruct((B, H, S, dh), F32),
        compiler_params=_cparams(("parallel", "parallel", "parallel")),
        name="sb_attention",
    )(q, k, v)


def _layer_norm(y, g, b):
    mu = jnp.mean(y, axis=1, keepdims=True)
    d = y - mu
    var = jnp.mean(d * d, axis=1, keepdims=True)
    return d * lax.rsqrt(var + LN_EPS) * g + b


def _rms(o, g):
    return o * lax.rsqrt(jnp.mean(o * o, axis=1, keepdims=True) + RMS_EPS) * g


def _post_kernel(on_ref, of_ref, os_ref, x_ref, go_ref, wo_ref, lg_ref, lb_ref, rw_ref, rb_ref,
                 x1_ref, te_ref, tw_ref, *, alpha):
    n1 = _rms(on_ref[...], go_ref[:, 0:D_NSA]).astype(BF16)
    n2 = _rms(of_ref[...], go_ref[:, D_NSA:D_NSA + D_FOX]).astype(BF16)
    n3 = _rms(os_ref[...], go_ref[:, D_NSA + D_FOX:]).astype(BF16)
    mix = (_dot(n1, wo_ref[0:D_NSA, :]) + _dot(n2, wo_ref[D_NSA:D_NSA + D_FOX, :])
           + _dot(n3, wo_ref[D_NSA + D_FOX:, :]))
    x1 = _layer_norm(alpha * x_ref[...] + mix, lg_ref[...], lb_ref[...])
    x1_ref[...] = x1
    logits = jnp.dot(x1, rw_ref[...], preferred_element_type=F32, precision=lax.Precision.HIGHEST) + rb_ref[...]
    tm, ne = logits.shape
    lane = lax.broadcasted_iota(jnp.int32, (tm, ne), 1).astype(F32)
    wide = lax.broadcasted_iota(jnp.int32, (tm, 128), 1)
    top_e = jnp.zeros((tm, 128), F32)
    top_l = jnp.full((tm, 128), NEG, F32)
    cur = logits
    for k in range(TOP_K):
        mx = jnp.max(cur, axis=1, keepdims=True)
        idx = jnp.min(jnp.where(cur == mx, lane, float(ne)), axis=1, keepdims=True)
        top_e = jnp.where(wide == k, idx, top_e)
        top_l = jnp.where(wide == k, mx, top_l)
        cur = jnp.where(lane == idx, -BIG, cur)
    pe = jnp.exp(top_l - jnp.max(top_l, axis=1, keepdims=True))
    te_ref[...] = top_e.astype(jnp.int32)
    tw_ref[...] = pe / jnp.sum(pe, axis=1, keepdims=True)


def _post(o_nsa, o_fox, o_sb, x2d, g_out, w_out_bf, ln_g, ln_b, router_w, router_b, alpha, tm=256):
    T, D = x2d.shape
    row = lambda w: pl.BlockSpec((tm, w), lambda i: (i, 0))
    full = lambda shape: pl.BlockSpec(shape, lambda i: (0, 0))
    return pl.pallas_call(
        functools.partial(_post_kernel, alpha=alpha),
        grid=(T // tm,),
        in_specs=[row(D_NSA), row(D_FOX), row(D_SB), row(D), full((1, D)), full((D, D)),
                  full((1, D)), full((1, D)), full((D, N_EXPERTS)), full((1, N_EXPERTS))],
        out_specs=[row(D), row(128), row(128)],
        out_shape=[jax.ShapeDtypeStruct((T, D), F32), jax.ShapeDtypeStruct((T, 128), jnp.int32),
                   jax.ShapeDtypeStruct((T, 128), F32)],
        compiler_params=_cparams(("parallel",)),
        name="outproj_ln_router",
    )(o_nsa, o_fox, o_sb, x2d, g_out.reshape(1, D), w_out_bf, ln_g.reshape(1, D), ln_b.reshape(1, D),
      router_w, router_b.reshape(1, N_EXPERTS))


def _row_gather_start(src_hbm, dst, sem, idx_ref, n_rows):
    def issue(r, c):
        pltpu.make_async_copy(src_hbm.at[pl.ds(idx_ref[0, 0, r], 1)], dst.at[pl.ds(r, 1)], sem).start()
        return c
    lax.fori_loop(0, n_rows, issue, 0, unroll=8)


def _row_gather_wait(src_hbm, dst, sem, n_rows):
    pltpu.make_async_copy(src_hbm.at[pl.ds(0, n_rows)], dst, sem).wait()


def _expert_kernel(blk_e_ref, nused_ref, idx_ref, idxn_ref, x_hbm, w_ref, wgu_ref, bgu_ref, wdn_ref, bdn_ref,
                   y_ref, xbuf, wgu_bf, wdn_bf, sem, *, tm, d_ff):
    i = pl.program_id(0)
    n_used = nused_ref[0]
    slot = i % 2

    @pl.when(i == 0)
    def _():
        _row_gather_start(x_hbm, xbuf.at[0], sem.at[0], idx_ref, tm)

    @pl.when(i + 1 < n_used)
    def _():
        _row_gather_start(x_hbm, xbuf.at[1 - slot], sem.at[1 - slot], idxn_ref, tm)

    @pl.when(i < n_used)
    def _():
        changed = (i == 0) | (blk_e_ref[i] != blk_e_ref[jnp.maximum(i - 1, 0)])

        @pl.when(changed)
        def _():
            wgu_bf[...] = wgu_ref[0].astype(BF16)
            wdn_bf[...] = wdn_ref[0].astype(BF16)

        _row_gather_wait(x_hbm, xbuf.at[slot], sem.at[slot], tm)
        xb = xbuf[slot].astype(BF16)
        gu = _dot(xb, wgu_bf[...]) + bgu_ref[0]
        gate = jnp.minimum(gu[:, :d_ff], SWIGLU_LIMIT)
        up = jnp.clip(gu[:, d_ff:], -SWIGLU_LIMIT, SWIGLU_LIMIT)
        act = gate * _sigmoid(SWIGLU_ALPHA * gate) * (up + 1.0)
        y = _dot(act.astype(BF16), wdn_bf[...]) + bdn_ref[0]
        y_ref[...] = y * w_ref[0]

    @pl.when(i >= n_used)
    def _():
        y_ref[...] = jnp.zeros_like(y_ref)


def _experts(x1, blk_e, n_used, buf_tok, buf_w, w_gu, b_gu, w_dn, b_dn, tm):
    T, D = x1.shape
    ne, _, d2 = w_gu.shape
    d_ff = d2 // 2
    n_blocks = blk_e.shape[0]
    idx = buf_tok.reshape(n_blocks, 1, tm)
    wcol = buf_w.reshape(n_blocks, tm, 1)
    last = n_blocks - 1
    grid_spec = pltpu.PrefetchScalarGridSpec(
        num_scalar_prefetch=2,
        grid=(n_blocks,),
        in_specs=[pl.BlockSpec((1, 1, tm), lambda i, e, n: (i, 0, 0), memory_space=pltpu.SMEM),
                  pl.BlockSpec((1, 1, tm), lambda i, e, n: (jnp.minimum(i + 1, last), 0, 0),
                               memory_space=pltpu.SMEM),
                  pl.BlockSpec(memory_space=pl.ANY),
                  pl.BlockSpec((1, tm, 1), lambda i, e, n: (i, 0, 0)),
                  pl.BlockSpec((1, D, d2), lambda i, e, n: (e[i], 0, 0)),
                  pl.BlockSpec((1, 1, d2), lambda i, e, n: (e[i], 0, 0)),
                  pl.BlockSpec((1, d_ff, D), lambda i, e, n: (e[i], 0, 0)),
                  pl.BlockSpec((1, 1, D), lambda i, e, n: (e[i], 0, 0))],
        out_specs=pl.BlockSpec((tm, D), lambda i, e, n: (i, 0)),
        scratch_shapes=[pltpu.VMEM((2, tm, D), F32), pltpu.VMEM((D, d2), BF16), pltpu.VMEM((d_ff, D), BF16),
                        pltpu.SemaphoreType.DMA((2,))],
    )
    return pl.pallas_call(
        functools.partial(_expert_kernel, tm=tm, d_ff=d_ff),
        grid_spec=grid_spec,
        out_shape=jax.ShapeDtypeStruct((n_blocks * tm, D), F32),
        compiler_params=_cparams(("arbitrary",), 58 * 1024 * 1024),
        name="moe_experts",
    )(blk_e, n_used, idx, idx, x1, wcol, w_gu, b_gu.reshape(ne, 1, d2), w_dn, b_dn.reshape(ne, 1, D))


def _combine_kernel(idx_ref, idxn_ref, y_hbm, x_ref, lg_ref, lb_ref, o_ref, ybuf, sem, *, tm, alpha):
    i = pl.program_id(0)
    n = pl.num_programs(0)
    slot = i % 2
    rows = TOP_K * tm

    @pl.when(i == 0)
    def _():
        _row_gather_start(y_hbm, ybuf.at[0], sem.at[0], idx_ref, rows)

    @pl.when(i + 1 < n)
    def _():
        _row_gather_start(y_hbm, ybuf.at[1 - slot], sem.at[1 - slot], idxn_ref, rows)

    _row_gather_wait(y_hbm, ybuf.at[slot], sem.at[slot], rows)
    ffn = ybuf[slot, 0:tm]
    for k in range(1, TOP_K):
        ffn = ffn + ybuf[slot, k * tm:(k + 1) * tm]
    o_ref[...] = _layer_norm(alpha * x_ref[...] + ffn, lg_ref[...], lb_ref[...])


def _combine(y_sorted, pos, x1, ln_g, ln_b, alpha, tm=128):
    T, D = x1.shape
    nt = T // tm
    idx = pos.reshape(nt, tm, TOP_K).transpose(0, 2, 1).reshape(nt, 1, TOP_K * tm)
    last = nt - 1
    return pl.pallas_call(
        functools.partial(_combine_kernel, tm=tm, alpha=alpha),
        grid=(nt,),
        in_specs=[pl.BlockSpec((1, 1, TOP_K * tm), lambda i: (i, 0, 0), memory_space=pltpu.SMEM),
                  pl.BlockSpec((1, 1, TOP_K * tm), lambda i: (jnp.minimum(i + 1, last), 0, 0),
                               memory_space=pltpu.SMEM),
                  pl.BlockSpec(memory_space=pl.ANY),
                  pl.BlockSpec((tm, D), lambda i: (i, 0)),
                  pl.BlockSpec((1, D), lambda i: (0, 0)),
                  pl.BlockSpec((1, D), lambda i: (0, 0))],
        out_specs=pl.BlockSpec((tm, D), lambda i: (i, 0)),
        out_shape=jax.ShapeDtypeStruct((T, D), F32),
        scratch_shapes=[pltpu.VMEM((2, TOP_K * tm, D), F32), pltpu.SemaphoreType.DMA((2,))],
        compiler_params=_cparams(("arbitrary",)),
        name="moe_combine_ln",
    )(idx, idx, y_sorted, x1, ln_g.reshape(1, D), ln_b.reshape(1, D))


def _route(top_e, top_w, tm):
    T = top_e.shape[0]
    M = T * TOP_K
    i32 = jnp.int32
    flat_e = top_e.reshape(M).astype(i32)
    flat_w = top_w.reshape(M)
    ar = jnp.arange(M, dtype=i32)
    experts = jnp.arange(N_EXPERTS, dtype=i32)
    skey, sorted_w = lax.sort((flat_e * M + ar, flat_w), num_keys=1)
    sorted_e = skey // M
    order = skey - sorted_e * M
    counts = jnp.sum((flat_e[:, None] == experts[None, :]).astype(i32), axis=0)
    start = jnp.cumsum(counts) - counts
    padded = (counts + tm - 1) // tm * tm
    pad_end = jnp.cumsum(padded)
    pad_start = pad_end - padded
    dest = pad_start[sorted_e] + ar - start[sorted_e]
    n_blocks = -(-M // tm) + N_EXPERTS
    blk_first = jnp.arange(n_blocks, dtype=i32) * tm
    blk_e = jnp.minimum(jnp.sum((pad_end[None, :] <= blk_first[:, None]).astype(i32), axis=1), N_EXPERTS - 1)
    within = (blk_first - pad_start[blk_e])[:, None] + jnp.arange(tm, dtype=i32)[None, :]
    valid = within < counts[blk_e][:, None]
    src_row = jnp.clip(start[blk_e][:, None] + within, 0, M - 1)
    buf_tok = jnp.where(valid, order[src_row] // TOP_K, 0).reshape(n_blocks * tm)
    buf_w = jnp.where(valid, sorted_w[src_row], 0.0).reshape(n_blocks * tm)
    _, pos = lax.sort((order, dest), num_keys=1)
    n_used = (pad_end[-1] // tm).astype(i32).reshape(1)
    return blk_e, n_used, buf_tok, buf_w, pos


def _proj_layout():
    src = np.zeros((D_IN_PAD,), np.int64)
    scale = np.zeros((D_IN_PAD,), np.float32)
    off_gate_src = 512 + 768
    off_fox_src = off_gate_src + 24
    off_logf_src = off_fox_src + 768
    off_sb_src = off_logf_src + 4
    qs = HEAD_DIM ** -0.5

    def put(dst, s0, n, sc=1.0):
        src[dst:dst + n] = np.arange(s0, s0 + n)
        scale[dst:dst + n] = sc

    put(0, 0, 512, qs)
    put(OFF_KV, 512, 768)
    put(OFF_FOX, off_fox_src, 256, qs)
    put(OFF_FOX + 256, off_fox_src + 256, 512)
    put(OFF_SB, off_sb_src, 256, qs)
    put(OFF_SB + 256, off_sb_src + 256, 512)
    put(OFF_GATE, off_gate_src, 24)
    put(OFF_LOGF, off_logf_src, 4)
    return src, scale


def _heads(p, b, s, n):
    return p.reshape(b, s, n, HEAD_DIM).transpose(0, 2, 1, 3)


def _mixer_heads(proj, B, S):
    kv = proj[..., OFF_KV:OFF_FOX].reshape(B, S, 3, 2, N_NSA_KV, HEAD_DIM).transpose(2, 3, 0, 4, 1, 5)
    fox = proj[..., OFF_FOX:OFF_SB].reshape(B, S, 3, N_FOX_HEADS, HEAD_DIM).transpose(2, 0, 3, 1, 4)
    sb = proj[..., OFF_SB:OFF_GATE].reshape(B, S, 3, N_SB_HEADS, HEAD_DIM).transpose(2, 0, 3, 1, 4)
    gates = proj[..., OFF_GATE:OFF_LOGF].reshape(B, S, 3, N_NSA_HEADS).transpose(2, 0, 3, 1)[..., None]
    q_nsa = _heads(proj[..., 0:D_NSA], B, S, N_NSA_HEADS).astype(BF16)
    return q_nsa, kv, fox.astype(BF16), sb.astype(BF16), gates


def _layer(x, consts, w_in, b_in, pos_k, pos_v, w1k, w2k, w1v, w2v, g_out, w_out, ln1_g, ln1_b,
           router_w, router_b, w_gu, b_gu, w_dn, b_dn, ln2_g, ln2_b, alpha, moe_tm):
    B, S, D = x.shape
    T = B * S
    cs = consts
    w_p = (w_in[:, cs["src"]] * cs["scale"]).astype(BF16)
    b_p = (b_in[cs["src"]] * cs["scale"]).reshape(1, D_IN_PAD)
    x2d = x.reshape(T, D)
    proj = _proj(x2d, w_p, b_p).reshape(B, S, D_IN_PAD)
    q_nsa, kv, fox, sb, gates = _mixer_heads(proj, B, S)

    def aug(a, extra):
        extra = jnp.broadcast_to(extra, a.shape[:-1] + extra.shape[-1:])
        return jnp.concatenate([a.astype(BF16), extra.astype(BF16)], axis=-1)

    k_cmp, v_cmp = _compress(kv[0, 0], kv[0, 1], pos_k, pos_v, w1k, w2k, w1v, w2v)
    o_c, msel = _nsa_cmp(cs["slopes"], q_nsa, k_cmp, v_cmp, cs["ovl_t"], gates[0])
    q2 = aug(q_nsa, cs["slope_aug"][None, :, None, :])
    o_s = _nsa_sel(q2, msel, aug(kv[1, 0], cs["pos_block_aug"]), aug(kv[1, 1], cs["one_aug"]), gates[1],
                   cs["sel_kb"])
    o_w = _nsa_win(q2, aug(kv[2, 0], cs["pos_aug"]), aug(kv[2, 1], cs["one_aug"]), gates[2])

    c = _logf_cumsum(proj[..., OFF_LOGF:OFF_LOGF + N_FOX_HEADS].transpose(0, 2, 1))
    c_hi = _trunc_bf16(c)
    c_mid = _trunc_bf16(c - c_hi)
    c_lo = c - c_hi - c_mid
    c_aug = jnp.pad(-jnp.stack([c_hi, c_mid, c_lo], axis=-1), ((0, 0), (0, 0), (0, 0), (0, HEAD_DIM - 3)))
    o_fox = _fox(aug(fox[0], cs["three_aug"]), aug(fox[1], c_aug), aug(fox[2], cs["one_aug"]), tq=min(512, S),
                 kb=min(256, S))
    o_sb = _sb(sb[0], sb[1], sb[2])

    def tokens(o):
        b, h, s, d = o.shape
        return o.transpose(0, 2, 1, 3).reshape(b * s, h * d)

    x1, te, tw = _post(tokens(o_c + o_s + o_w), tokens(o_fox), tokens(o_sb), x2d, g_out, w_out.astype(BF16),
                       ln1_g, ln1_b, router_w, router_b, alpha)
    blk_e, n_used, buf_tok, buf_w, pos = _route(te[:, :TOP_K], tw[:, :TOP_K], moe_tm)
    y_sorted = _experts(x1, blk_e, n_used, buf_tok, buf_w, w_gu, b_gu, w_dn, b_dn, moe_tm)
    out = _combine(y_sorted, pos, x1, ln2_g, ln2_b, alpha)
    return out.reshape(B, S, D)


def _constants(S, sel_kb):
    src, scale = _proj_layout()
    slopes = jnp.exp2(-8.0 * jnp.arange(1, N_NSA_HEADS + 1, dtype=F32) / N_NSA_HEADS)
    nc = S // CMP_STRIDE
    nb = S // SEL_LEN
    n_cmp = (S - CMP_LEN) // CMP_STRIDE + 1
    cmp_idx = np.arange(nc)[:, None] * CMP_STRIDE + np.arange(CMP_LEN)[None, :]
    ovl = (cmp_idx[:, :, None] // SEL_LEN == np.arange(nb)[None, None, :]).astype(np.float32).mean(axis=1)
    ovl[n_cmp:] = 0.0
    ovl_t = jnp.asarray(ovl.T, dtype=BF16)
    pos = np.arange(S)
    pos_aug = np.zeros((S, HEAD_DIM), np.float32)
    pos_aug[:, 0] = pos // SEL_LEN * SEL_LEN
    pos_aug[:, 1] = pos % SEL_LEN
    block_onehot = (pos[:, None] // SEL_LEN == np.arange(nb)[None, :]).astype(np.float32)
    slope_np = 2.0 ** (-8.0 * np.arange(1, N_NSA_HEADS + 1) / N_NSA_HEADS)
    slope_aug = np.zeros((N_NSA_HEADS, HEAD_DIM), np.float32)
    slope_aug[:, 0] = slope_np
    slope_aug[:, 1] = slope_np
    one_aug = np.zeros((HEAD_DIM,), np.float32)
    one_aug[0] = 1.0
    three_aug = np.zeros((HEAD_DIM,), np.float32)
    three_aug[0:3] = 1.0
    return dict(src=jnp.asarray(src), scale=jnp.asarray(scale), slopes=slopes, ovl_t=ovl_t, sel_kb=sel_kb,
                pos_aug=jnp.asarray(pos_aug), pos_block_aug=jnp.asarray(np.concatenate([pos_aug, block_onehot], 1)),
                slope_aug=jnp.asarray(slope_aug), one_aug=jnp.asarray(one_aug), three_aug=jnp.asarray(three_aug))


def kernel(x, w_in, b_in, cmp_pos_k, cmp_pos_v, cmp_w1_k, cmp_w2_k, cmp_w1_v, cmp_w2_v, g_out, w_out,
           ln1_g, ln1_b, router_w, router_b, w_gate_up, b_gate_up, w_down, b_down, ln2_g, ln2_b):
    depth = w_in.shape[0]
    S = x.shape[1]
    alpha = (2 * depth) ** 0.25
    consts = _constants(S, min(256, S))
    for l in range(depth):
        x = _layer(x, consts, w_in[l], b_in[l], cmp_pos_k[l], cmp_pos_v[l], cmp_w1_k[l], cmp_w2_k[l],
                   cmp_w1_v[l], cmp_w2_v[l], g_out[l], w_out[l], ln1_g[l], ln1_b[l], router_w[l], router_b[l],
                   w_gate_up[l], b_gate_up[l], w_down[l], b_down[l], ln2_g[l], ln2_b[l], alpha, 256)
    return x
```

```python
import functools

import jax
import jax.numpy as jnp
import numpy as np
from jax import lax
from jax.experimental import pallas as pl
from jax.experimental.pallas import tpu as pltpu

F32 = jnp.float32
BF16 = jnp.bfloat16

HEAD_DIM = 64
N_NSA_HEADS = 8
N_NSA_KV = 2
NSA_REP = 4
N_FOX_HEADS = 4
N_SB_HEADS = 4
D_NSA = 512
D_FOX = 256
D_SB = 256
CMP_LEN = 32
CMP_STRIDE = 16
CMP_HIDDEN = 128
SEL_LEN = 64
SEL_TOPK = 16
WINDOW = 512
Q_BLOCK = 128
N_EXPERTS = 32
TOP_K = 4
SWIGLU_LIMIT = 7.0
SWIGLU_ALPHA = 1.702
LN_EPS = 1e-5
RMS_EPS = 1e-6
NEG = -1e30
BIG = 3e38
SB_CUTOFF = -90.0

LANES = 128
HALF = LANES // 2
NSA_ROW_HEADS = (0, 2, 1, 3)

SRC_KV = 512
SRC_GATE = SRC_KV + 768
SRC_FOX = SRC_GATE + 24
SRC_LOGF = SRC_FOX + 768
SRC_SB = SRC_LOGF + 4

VMEM_LIMIT = 52 * 1024 * 1024

PROJ_TM = 512
SEL_KB = 256
FOX_TQ = 512
FOX_KB = 256
SB_TQ = 128
POST_TM = 256
MOE_TM = 256
COMBINE_TM = 128


def _cparams(sem, vmem=VMEM_LIMIT):
    return pltpu.CompilerParams(dimension_semantics=sem, vmem_limit_bytes=vmem)


def _dot(a, b):
    return jnp.dot(a, b, preferred_element_type=F32)


def _dot_nt(a, b):
    return lax.dot_general(a, b, (((1,), (1,)), ((), ())), preferred_element_type=F32)


def _split_bf16(x):
    hi = x.astype(BF16)
    lo = (x - hi.astype(F32)).astype(BF16)
    return hi, lo


def _trunc_bf16(x):
    bits = lax.bitcast_convert_type(x, jnp.uint32) & jnp.uint32(0xFFFF0000)
    return lax.bitcast_convert_type(bits, F32)


def _sigmoid(x):
    return 1.0 / (1.0 + jnp.exp(-x))


def _lane_column(tile, idx):
    lane = lax.broadcasted_iota(jnp.int32, tile.shape, 1)
    return jnp.sum(jnp.where(lane == idx, tile, 0.0), axis=1, keepdims=True)


def _low_half(shape):
    return lax.broadcasted_iota(jnp.int32, shape, 1) < HALF


def _proj_kernel(*refs, outs, has_extra):
    x_ref, w_ref, b_ref = refs[:3]
    o_refs = refs[3 + has_extra:]
    y = _dot(x_ref[...].astype(BF16), w_ref[...]) + b_ref[...]
    if has_extra:
        y = y + refs[3][...].astype(F32)
    off = 0
    for o_ref, (nh, width, _) in zip(o_refs, outs):
        for h in range(nh):
            o_ref[0, h] = y[:, off:off + width].astype(o_ref.dtype)
            off += width


def _proj(x2d, w_bf, b_row, extra, outs, B, S):
    T, D = x2d.shape
    N = w_bf.shape[1]
    tm = min(PROJ_TM, S)
    spb = S // tm
    in_specs = [pl.BlockSpec((tm, D), lambda i: (i, 0)),
                pl.BlockSpec((D, N), lambda i: (0, 0)),
                pl.BlockSpec((1, N), lambda i: (0, 0))]
    args = [x2d, w_bf, b_row]
    if extra is not None:
        in_specs.append(pl.BlockSpec((tm, N), lambda i: (i % spb, 0)))
        args.append(extra)
    return pl.pallas_call(
        functools.partial(_proj_kernel, outs=outs, has_extra=extra is not None),
        grid=(T // tm,),
        in_specs=in_specs,
        out_specs=[pl.BlockSpec((1, nh, tm, w), lambda i: (i // spb, 0, i % spb, 0)) for nh, w, _ in outs],
        out_shape=[jax.ShapeDtypeStruct((B, nh, S, w), dt) for nh, w, dt in outs],
        compiler_params=_cparams(("parallel",)),
        name="proj",
    )(*args)


def _gelu_tanh(x):
    return 0.5 * x * (1.0 + jnp.tanh(0.7978845608028654 * (x + 0.044715 * (x * x * x))))


def _compress_kernel(k_ref, v_ref, pk_ref, pv_ref, w1k_ref, w2k_ref, w1v_ref, w2v_ref, ka_ref, ok_ref, ov_ref):
    nc = k_ref.shape[2]
    half = CMP_STRIDE * HEAD_DIM

    def one(r_ref, p_ref, w1_ref, w2_ref):
        r = r_ref[0, 0]
        a = (r + p_ref[0:1, :]).astype(BF16)
        b = (r + p_ref[1:2, :]).astype(BF16)
        ha = _dot(a, w1_ref[0:half, :])
        hb = _dot(b, w1_ref[half:2 * half, :])
        hid = _gelu_tanh(ha + pltpu.roll(hb, nc - 1, 0))
        return _dot(hid.astype(BF16), w2_ref[...])

    ok_ref[0, 0] = (one(k_ref, pk_ref, w1k_ref, w2k_ref) + ka_ref[...]).astype(ok_ref.dtype)
    ov_ref[0, 0] = one(v_ref, pv_ref, w1v_ref, w2v_ref).astype(ov_ref.dtype)


def _compress(kc, vc, pos_k, pos_v, w1k, w2k, w1v, w2v, kcmp_aug):
    B, G, S, dh = kc.shape
    nc = S // CMP_STRIDE
    width = CMP_STRIDE * dh
    kr = kc.reshape(B, G, nc, width)
    vr = vc.reshape(B, G, nc, width)
    pk = pos_k.reshape(2, width)
    pv = pos_v.reshape(2, width)
    w2k_p = jnp.pad(w2k, ((0, 0), (0, LANES - dh))).astype(BF16)
    w2v_p = jnp.concatenate([jnp.pad(w2v, ((0, 0), (0, LANES - dh))), jnp.pad(w2v, ((0, 0), (LANES - dh, 0)))],
                            axis=1).astype(BF16)
    kv_spec = pl.BlockSpec((1, 1, nc, width), lambda b, g: (b, g, 0, 0))
    full = lambda shape: pl.BlockSpec(shape, lambda b, g: tuple(0 for _ in shape))
    return pl.pallas_call(
        _compress_kernel,
        grid=(B, G),
        in_specs=[kv_spec, kv_spec, full((2, width)), full((2, width)),
                  full((2 * width, CMP_HIDDEN)), full((CMP_HIDDEN, LANES)),
                  full((2 * width, CMP_HIDDEN)), full((CMP_HIDDEN, 2 * LANES)), full((nc, LANES))],
        out_specs=[pl.BlockSpec((1, 1, nc, LANES), lambda b, g: (b, g, 0, 0)),
                   pl.BlockSpec((1, 1, nc, 2 * LANES), lambda b, g: (b, g, 0, 0))],
        out_shape=[jax.ShapeDtypeStruct((B, G, nc, LANES), BF16),
                   jax.ShapeDtypeStruct((B, G, nc, 2 * LANES), BF16)],
        compiler_params=_cparams(("parallel", "parallel")),
        name="nsa_compress",
    )(kr, vr, pk, pv, w1k.astype(BF16), w2k_p, w1v.astype(BF16), w2v_p, kcmp_aug)


def _nsa_gates(small, branch, g):
    return [_sigmoid(_lane_column(small, branch * N_NSA_HEADS + g * NSA_REP + h)) for h in NSA_ROW_HEADS]


def _nsa_cmp_kernel(q_ref, kc_ref, vc_ref, ovl_ref, sm_ref, o_ref, sel_ref, *, n_sel):
    g = pl.program_id(1)
    i = pl.program_id(2)
    q0 = i * Q_BLOCK
    nc = kc_ref.shape[2]
    nb = ovl_ref.shape[0]
    kc = kc_ref[0, 0]
    vc = vc_ref[0, 0]
    gates = _nsa_gates(sm_ref[0], 0, g)
    t = q0 + lax.broadcasted_iota(jnp.int32, (Q_BLOCK, nc), 0)
    cmp_end = lax.broadcasted_iota(jnp.int32, (Q_BLOCK, nc), 1) * CMP_STRIDE + (CMP_LEN - 1)
    mask = cmp_end <= t
    psum = jnp.zeros((Q_BLOCK, nc), F32)
    outs = []
    for r in range(NSA_REP):
        s = jnp.where(mask, _dot_nt(q_ref[0, r], kc), NEG)
        m = jnp.max(s, axis=1, keepdims=True)
        p = jnp.where(mask, jnp.exp(s - m), 0.0)
        l = jnp.sum(p, axis=1, keepdims=True)
        pn = p / jnp.where(l > 0.0, l, 1.0)
        v_half = vc[:, 0:LANES] if r < 2 else vc[:, LANES:2 * LANES]
        outs.append(_dot(pn.astype(BF16), v_half) * gates[r])
        psum = psum + pn
    o_ref[0] = jnp.concatenate([outs[0] + outs[2], outs[1] + outs[3]], axis=1)
    hi, lo = _split_bf16(psum)
    ovl = ovl_ref[...]
    imp = _dot_nt(ovl, hi) + _dot_nt(ovl, lo)
    jf = lax.broadcasted_iota(jnp.int32, (nb, Q_BLOCK), 0)
    tq = q0 + lax.broadcasted_iota(jnp.int32, (nb, Q_BLOCK), 1)
    cur = tq >> 6
    valid = jf <= cur
    forced = valid & ((jf == 0) | (jf >= cur - 1))
    v = jnp.where(forced, BIG, jnp.where(valid, imp, -BIG))
    jff = jf.astype(F32)
    picked = jnp.zeros((nb, Q_BLOCK), F32)
    for _ in range(n_sel):
        mx = jnp.max(v, axis=0, keepdims=True)
        idx = jnp.min(jnp.where(v == mx, jff, float(nb)), axis=0, keepdims=True)
        hit = jff == idx
        picked = jnp.where(hit, 1.0, picked)
        v = jnp.where(hit, -BIG, v)
    msel = jnp.where(valid & (picked > 0.5), 0.0, NEG)
    sel_ref[0, 0, 0] = msel.T.astype(sel_ref.dtype)


def _nsa_cmp(q2, kcmp, vcmp, ovl_t, small):
    B, H, S, qw = q2.shape
    G = N_NSA_KV
    nq = S // Q_BLOCK
    nc = kcmp.shape[2]
    nb = ovl_t.shape[0]
    n_sel = min(SEL_TOPK, S // SEL_LEN)
    return pl.pallas_call(
        functools.partial(_nsa_cmp_kernel, n_sel=n_sel),
        grid=(B, G, nq),
        in_specs=[pl.BlockSpec((1, NSA_REP, Q_BLOCK, qw), lambda b, g, i: (b, g, i, 0)),
                  pl.BlockSpec((1, 1, nc, LANES), lambda b, g, i: (b, g, 0, 0)),
                  pl.BlockSpec((1, 1, nc, 2 * LANES), lambda b, g, i: (b, g, 0, 0)),
                  pl.BlockSpec((nb, nc), lambda b, g, i: (0, 0)),
                  pl.BlockSpec((1, Q_BLOCK, LANES), lambda b, g, i: (b, i, 0))],
        out_specs=[pl.BlockSpec((1, Q_BLOCK, 2 * LANES), lambda b, g, i: (b, i, g)),
                   pl.BlockSpec((1, 1, 1, Q_BLOCK, nb), lambda b, g, i: (b, g, i, 0, 0))],
        out_shape=[jax.ShapeDtypeStruct((B, S, D_NSA), F32),
                   jax.ShapeDtypeStruct((B, G, nq, Q_BLOCK, nb), BF16)],
        compiler_params=_cparams(("parallel", "parallel", "parallel")),
        name="nsa_cmp_select",
    )(q2, kcmp, vcmp, ovl_t, small)


def _flash_init(rows):
    return jnp.full((rows, 1), NEG, F32), jnp.zeros((rows, LANES), F32)


def _flash_update(carry, s, v_tile):
    m, acc = carry
    half = s.shape[0] // 2
    m_new = jnp.maximum(m, jnp.max(s, axis=1, keepdims=True))
    p = jnp.exp(s - m_new).astype(BF16)
    pv = jnp.concatenate([_dot(p[:half], v_tile[:, 0:LANES]), _dot(p[half:], v_tile[:, LANES:2 * LANES])], axis=0)
    return m_new, jnp.exp(m - m_new) * acc + pv


def _pair_out(acc_even, acc_odd):
    even = acc_even / acc_even[:, HALF:HALF + 1]
    odd = acc_odd / acc_odd[:, 0:1]
    return jnp.where(_low_half(acc_even.shape), even, odd)


def _nsa_out(acc, gates):
    c = [acc[r * Q_BLOCK:(r + 1) * Q_BLOCK] for r in range(NSA_REP)]
    low = _low_half((Q_BLOCK, LANES))
    pair_a = _pair_out(c[0], c[2]) * jnp.where(low, gates[0], gates[2])
    pair_b = _pair_out(c[1], c[3]) * jnp.where(low, gates[1], gates[3])
    return jnp.concatenate([pair_a, pair_b], axis=1)


def _stacked_rows(rows, kb):
    row = lax.broadcasted_iota(jnp.int32, (rows, kb), 0) & (Q_BLOCK - 1)
    col = lax.broadcasted_iota(jnp.int32, (rows, kb), 1)
    return row, col


def _nsa_sel_kernel(q_ref, msel_ref, k_ref, v_ref, sm_ref, o_ref, qa_ref, sa_ref, sb_ref, *, kb):
    g = pl.program_id(1)
    i = pl.program_id(2)
    q0 = i * Q_BLOCK
    rows = NSA_REP * Q_BLOCK
    msel = msel_ref[0, 0, 0]
    for r in range(NSA_REP):
        qa_ref[r * Q_BLOCK:(r + 1) * Q_BLOCK, :] = jnp.concatenate([q_ref[0, r], msel], axis=1)
    qa = qa_ref[...]
    last = (q0 + Q_BLOCK - 1) // kb

    def logits(kt):
        k0 = pl.multiple_of(kt * kb, kb)
        return _dot_nt(qa, k_ref[0, 0, pl.ds(k0, kb), :])

    def v_tile(kt):
        return v_ref[0, 0, pl.ds(pl.multiple_of(kt * kb, kb), kb), :]

    sa_ref[...] = logits(0)

    def body(j, carry):
        sb_ref[...] = logits(2 * j + 1)
        carry = _flash_update(carry, sa_ref[...], v_tile(2 * j))
        sa_ref[...] = logits(2 * j + 2)
        return _flash_update(carry, sb_ref[...], v_tile(2 * j + 1))

    pairs = last // 2
    carry = lax.fori_loop(0, pairs, body, _flash_init(rows))
    row, col = _stacked_rows(rows, kb)
    causal = (last * kb + col) <= (q0 + row)

    def tail_odd(carry):
        carry = _flash_update(carry, sa_ref[...], v_tile(last - 1))
        return _flash_update(carry, jnp.where(causal, logits(last), NEG), v_tile(last))

    def tail_even(carry):
        return _flash_update(carry, jnp.where(causal, sa_ref[...], NEG), v_tile(last))

    _, acc = lax.cond(last - 2 * pairs == 1, tail_odd, tail_even, carry)
    o_ref[0] = _nsa_out(acc, _nsa_gates(sm_ref[0], 1, g))


def _nsa_sel(q2, msel, k2, v2, small, kb):
    B, H, S, qw = q2.shape
    G = N_NSA_KV
    nq = S // Q_BLOCK
    nb = msel.shape[-1]
    return pl.pallas_call(
        functools.partial(_nsa_sel_kernel, kb=kb),
        grid=(B, G, nq),
        in_specs=[pl.BlockSpec((1, NSA_REP, Q_BLOCK, qw), lambda b, g, i: (b, g, i, 0)),
                  pl.BlockSpec((1, 1, 1, Q_BLOCK, nb), lambda b, g, i: (b, g, i, 0, 0)),
                  pl.BlockSpec((1, 1, S, k2.shape[-1]), lambda b, g, i: (b, g, 0, 0)),
                  pl.BlockSpec((1, 1, S, 2 * LANES), lambda b, g, i: (b, g, 0, 0)),
                  pl.BlockSpec((1, Q_BLOCK, LANES), lambda b, g, i: (b, i, 0))],
        out_specs=pl.BlockSpec((1, Q_BLOCK, 2 * LANES), lambda b, g, i: (b, i, g)),
        out_shape=jax.ShapeDtypeStruct((B, S, D_NSA), F32),
        scratch_shapes=[pltpu.VMEM((NSA_REP * Q_BLOCK, qw + nb), BF16),
                        pltpu.VMEM((NSA_REP * Q_BLOCK, kb), F32),
                        pltpu.VMEM((NSA_REP * Q_BLOCK, kb), F32)],
        compiler_params=_cparams(("parallel", "parallel", "parallel")),
        name="nsa_selected",
    )(q2, msel, k2, v2, small)


def _nsa_win_kernel(q_ref, k_ref, v_ref, sm_ref, o_ref):
    g = pl.program_id(1)
    i = pl.program_id(2)
    kb = Q_BLOCK
    rows = NSA_REP * Q_BLOCK
    qa = q_ref[0].reshape(rows, q_ref.shape[-1])
    n_back = WINDOW // kb
    row, col = _stacked_rows(rows, kb)
    carry = _flash_init(rows)
    for d in range(n_back + 1):
        kt = i - n_back + d
        k0 = pl.multiple_of(jnp.maximum(kt, 0) * kb, kb)
        s = _dot_nt(qa, k_ref[0, 0, pl.ds(k0, kb), :])
        if d == 0:
            s = jnp.where((col > row) & (kt >= 0), s, NEG)
        elif d == n_back:
            s = jnp.where(col <= row, s, NEG)
        else:
            s = jnp.where(kt >= 0, s, NEG)
        carry = _flash_update(carry, s, v_ref[0, 0, pl.ds(k0, kb), :])
    o_ref[0] = _nsa_out(carry[1], _nsa_gates(sm_ref[0], 2, g))


def _nsa_win(q2, k2, v2, small):
    B, H, S, qw = q2.shape
    G = N_NSA_KV
    nq = S // Q_BLOCK
    return pl.pallas_call(
        _nsa_win_kernel,
        grid=(B, G, nq),
        in_specs=[pl.BlockSpec((1, NSA_REP, Q_BLOCK, qw), lambda b, g, i: (b, g, i, 0)),
                  pl.BlockSpec((1, 1, S, qw), lambda b, g, i: (b, g, 0, 0)),
                  pl.BlockSpec((1, 1, S, 2 * LANES), lambda b, g, i: (b, g, 0, 0)),
                  pl.BlockSpec((1, Q_BLOCK, LANES), lambda b, g, i: (b, i, 0))],
        out_specs=pl.BlockSpec((1, Q_BLOCK, 2 * LANES), lambda b, g, i: (b, i, g)),
        out_shape=jax.ShapeDtypeStruct((B, S, D_NSA), F32),
        compiler_params=_cparams(("parallel", "parallel", "parallel")),
        name="nsa_window",
    )(q2, k2, v2, small)


def _fox_update(carry, s, v_tile):
    m, acc = carry
    m_new = jnp.maximum(m, jnp.max(s, axis=1, keepdims=True))
    p = jnp.exp(s - m_new).astype(BF16)
    return m_new, jnp.exp(m - m_new) * acc + _dot(p, v_tile)


def _fox_kernel(q_ref, k_ref, v_ref, o_ref, *s_refs, tq, kb):
    i = pl.program_id(2)

    def logits(h, kt):
        k0 = pl.multiple_of(kt * kb, kb)
        return _dot_nt(q_ref[0, h], k_ref[0, h, pl.ds(k0, kb), :])

    def v_tile(h, kt):
        return v_ref[0, h, pl.ds(pl.multiple_of(kt * kb, kb), kb), :]

    for h in range(2):
        s_refs[2 * h][...] = logits(h, 0)

    def body(j, carries):
        out = []
        for h in range(2):
            sa_ref, sb_ref = s_refs[2 * h], s_refs[2 * h + 1]
            sb_ref[...] = logits(h, 2 * j + 1)
            c = _fox_update(carries[h], sa_ref[...], v_tile(h, 2 * j))
            sa_ref[...] = logits(h, 2 * j + 2)
            out.append(_fox_update(c, sb_ref[...], v_tile(h, 2 * j + 1)))
        return tuple(out)

    carries = lax.fori_loop(0, i, body, (_flash_init(tq), _flash_init(tq)))
    row = lax.broadcasted_iota(jnp.int32, (tq, kb), 0)
    col = lax.broadcasted_iota(jnp.int32, (tq, kb), 1)
    accs = []
    for h in range(2):
        s1 = logits(h, 2 * i + 1)
        c = _fox_update(carries[h], jnp.where(col <= row, s_refs[2 * h][...], NEG), v_tile(h, 2 * i))
        c = _fox_update(c, jnp.where(kb + col <= row, s1, NEG), v_tile(h, 2 * i + 1))
        accs.append(c[1])
    o_ref[0] = _pair_out(accs[0], accs[1])


def _fox(q2, k2, v2):
    B, H, S, qw = q2.shape
    tq, kb = min(FOX_TQ, S), min(FOX_KB, S)
    assert tq == 2 * kb
    return pl.pallas_call(
        functools.partial(_fox_kernel, tq=tq, kb=kb),
        grid=(B, H // 2, S // tq),
        in_specs=[pl.BlockSpec((1, 2, tq, qw), lambda b, h, i: (b, h, i, 0)),
                  pl.BlockSpec((1, 2, S, qw), lambda b, h, i: (b, h, 0, 0)),
                  pl.BlockSpec((1, 2, S, LANES), lambda b, h, i: (b, h, 0, 0))],
        out_specs=pl.BlockSpec((1, tq, LANES), lambda b, h, i: (b, i, h)),
        out_shape=jax.ShapeDtypeStruct((B, S, D_FOX), F32),
        scratch_shapes=[pltpu.VMEM((tq, kb), F32)] * 4,
        compiler_params=_cparams(("parallel", "parallel", "parallel")),
        name="fox_attention",
    )(q2, k2, v2)


def _logf_cumsum_kernel(x_ref, c_ref):
    x = x_ref[0]
    nr, nl = x.shape
    log_f = jnp.minimum(x, 0.0) - jnp.log1p(jnp.exp(-jnp.abs(x)))
    hp = lax.Precision.HIGHEST
    incl = (lax.broadcasted_iota(jnp.int32, (nl, nl), 0) <= lax.broadcasted_iota(jnp.int32, (nl, nl), 1))
    within = jnp.dot(log_f, incl.astype(F32), preferred_element_type=F32, precision=hp)
    totals = jnp.broadcast_to(within[:, nl - 1:nl], (nr, nl))
    before = (lax.broadcasted_iota(jnp.int32, (nr, nr), 1) < lax.broadcasted_iota(jnp.int32, (nr, nr), 0))
    c_ref[0] = within + jnp.dot(before.astype(F32), totals, preferred_element_type=F32, precision=hp)


def _logf_cumsum(logits):
    B, H, S = logits.shape
    x = logits.reshape(B * H, S // LANES, LANES)
    c = pl.pallas_call(
        _logf_cumsum_kernel,
        grid=(B * H,),
        in_specs=[pl.BlockSpec((1, S // LANES, LANES), lambda i: (i, 0, 0))],
        out_specs=pl.BlockSpec((1, S // LANES, LANES), lambda i: (i, 0, 0)),
        out_shape=jax.ShapeDtypeStruct(x.shape, F32),
        compiler_params=_cparams(("parallel",)),
        name="logf_cumsum",
    )(x)
    return c.reshape(B, H, S)


def _sb_kernel(q_ref, k_ref, v_ref, o_ref, *, tq, nh):
    i = pl.program_id(1)
    q0 = i * tq
    trow = q0 + lax.broadcasted_iota(jnp.int32, (tq, tq), 0)
    col = lax.broadcasted_iota(jnp.int32, (tq, tq), 1)
    rr = lax.broadcasted_iota(jnp.int32, (tq, tq), 0)
    upper = (rr > col).astype(BF16)

    def cond(state):
        kt, carries, _ = state
        alive = jnp.max(carries[0])
        for h in range(1, nh):
            alive = jnp.maximum(alive, jnp.max(carries[h]))
        return (kt >= 0) & (alive > SB_CUTOFF)

    def body(state):
        kt, carries, accs = state
        k0 = pl.multiple_of(kt * tq, tq)
        strict = (k0 + col) < trow
        new_c, new_a = [], []
        for h in range(nh):
            z = _dot_nt(q_ref[0, h], k_ref[0, h, pl.ds(k0, tq), :])
            log_beta = jnp.minimum(z, 0.0) - jnp.log1p(jnp.exp(-jnp.abs(z)))
            log_keep = jnp.where(strict, log_beta - z, 0.0)
            hi, lo = _split_bf16(log_keep)
            later = _dot(hi, upper) + _dot(lo, upper)
            a = jnp.where(strict, jnp.exp(log_beta + later + carries[h]), 0.0)
            new_a.append(accs[h] + _dot(a.astype(BF16), v_ref[0, h, pl.ds(k0, tq), :]))
            new_c.append(carries[h] + jnp.sum(log_keep, axis=1, keepdims=True))
        return kt - 1, tuple(new_c), tuple(new_a)

    state = (i, tuple(jnp.zeros((tq, 1), F32) for _ in range(nh)),
             tuple(jnp.zeros((tq, LANES), F32) for _ in range(nh)))
    _, _, accs = lax.while_loop(cond, body, state)
    o_ref[0] = jnp.concatenate([accs[2 * j] + accs[2 * j + 1] for j in range(nh // 2)], axis=1)


def _sb(q2, k2, v2):
    B, H, S, w = q2.shape
    tq = min(SB_TQ, S)
    return pl.pallas_call(
        functools.partial(_sb_kernel, tq=tq, nh=H),
        grid=(B, S // tq),
        in_specs=[pl.BlockSpec((1, H, tq, w), lambda b, i: (b, 0, i, 0)),
                  pl.BlockSpec((1, H, S, w), lambda b, i: (b, 0, 0, 0)),
                  pl.BlockSpec((1, H, S, w), lambda b, i: (b, 0, 0, 0))],
        out_specs=pl.BlockSpec((1, tq, D_SB), lambda b, i: (b, i, 0)),
        out_shape=jax.ShapeDtypeStruct((B, S, D_SB), F32),
        compiler_params=_cparams(("parallel", "parallel")),
        name="sb_attention",
    )(q2, k2, v2)


def _layer_norm(y, g, b):
    mu = jnp.mean(y, axis=1, keepdims=True)
    d = y - mu
    var = jnp.mean(d * d, axis=1, keepdims=True)
    return d * lax.rsqrt(var + LN_EPS) * g + b


def _rms(o, g):
    return o * lax.rsqrt(jnp.mean(o * o, axis=1, keepdims=True) + RMS_EPS) * g


def _post_kernel(oc_ref, osel_ref, ow_ref, of_ref, os_ref, x_ref, go_ref, wo_ref, lg_ref, lb_ref, rw_ref, rb_ref,
                 x1_ref, te_ref, tw_ref, *, alpha):
    o_nsa = oc_ref[...] + osel_ref[...] + ow_ref[...]
    n1 = _rms(o_nsa, go_ref[:, 0:D_NSA]).astype(BF16)
    n2 = _rms(of_ref[...], go_ref[:, D_NSA:D_NSA + D_FOX]).astype(BF16)
    n3 = _rms(os_ref[...], go_ref[:, D_NSA + D_FOX:]).astype(BF16)
    mix = (_dot(n1, wo_ref[0:D_NSA, :]) + _dot(n2, wo_ref[D_NSA:D_NSA + D_FOX, :])
           + _dot(n3, wo_ref[D_NSA + D_FOX:, :]))
    x1 = _layer_norm(alpha * x_ref[...] + mix, lg_ref[...], lb_ref[...])
    x1_ref[...] = x1
    logits = jnp.dot(x1, rw_ref[...], preferred_element_type=F32, precision=lax.Precision.HIGHEST) + rb_ref[...]
    tm, ne = logits.shape
    lane = lax.broadcasted_iota(jnp.int32, (tm, ne), 1).astype(F32)
    wide = lax.broadcasted_iota(jnp.int32, (tm, LANES), 1)
    top_e = jnp.zeros((tm, LANES), F32)
    top_l = jnp.full((tm, LANES), NEG, F32)
    cur = logits
    for k in range(TOP_K):
        mx = jnp.max(cur, axis=1, keepdims=True)
        idx = jnp.min(jnp.where(cur == mx, lane, float(ne)), axis=1, keepdims=True)
        top_e = jnp.where(wide == k, idx, top_e)
        top_l = jnp.where(wide == k, mx, top_l)
        cur = jnp.where(lane == idx, -BIG, cur)
    pe = jnp.exp(top_l - jnp.max(top_l, axis=1, keepdims=True))
    te_ref[...] = top_e.astype(jnp.int32)
    tw_ref[...] = pe / jnp.sum(pe, axis=1, keepdims=True)


def _post(o_c, o_s, o_w, o_fox, o_sb, x2d, g_out, w_out_bf, ln_g, ln_b, router_w, router_b, alpha):
    T, D = x2d.shape
    tm = POST_TM
    row = lambda w: pl.BlockSpec((tm, w), lambda i: (i, 0))
    full = lambda shape: pl.BlockSpec(shape, lambda i: (0, 0))
    return pl.pallas_call(
        functools.partial(_post_kernel, alpha=alpha),
        grid=(T // tm,),
        in_specs=[row(D_NSA), row(D_NSA), row(D_NSA), row(D_FOX), row(D_SB), row(D), full((1, D)), full((D, D)),
                  full((1, D)), full((1, D)), full((D, N_EXPERTS)), full((1, N_EXPERTS))],
        out_specs=[row(D), row(LANES), row(LANES)],
        out_shape=[jax.ShapeDtypeStruct((T, D), F32), jax.ShapeDtypeStruct((T, LANES), jnp.int32),
                   jax.ShapeDtypeStruct((T, LANES), F32)],
        compiler_params=_cparams(("parallel",)),
        name="outproj_ln_router",
    )(o_c, o_s, o_w, o_fox, o_sb, x2d, g_out.reshape(1, D), w_out_bf, ln_g.reshape(1, D), ln_b.reshape(1, D),
      router_w, router_b.reshape(1, N_EXPERTS))


def _row_gather_start(src_hbm, dst, sem, idx_ref, n_rows):
    def issue(r, c):
        pltpu.make_async_copy(src_hbm.at[pl.ds(idx_ref[0, 0, r], 1)], dst.at[pl.ds(r, 1)], sem).start()
        return c
    lax.fori_loop(0, n_rows, issue, 0, unroll=8)


def _row_gather_wait(src_hbm, dst, sem, n_rows):
    pltpu.make_async_copy(src_hbm.at[pl.ds(0, n_rows)], dst, sem).wait()


def _expert_kernel(blk_e_ref, nused_ref, idx_ref, idxn_ref, x_hbm, w_ref, wgu_ref, bgu_ref, wdn_ref, bdn_ref,
                   y_ref, xbuf, wgu_bf, wdn_bf, sem, *, tm, d_ff):
    i = pl.program_id(0)
    n_used = nused_ref[0]
    slot = i % 2

    @pl.when(i == 0)
    def _():
        _row_gather_start(x_hbm, xbuf.at[0], sem.at[0], idx_ref, tm)

    @pl.when(i + 1 < n_used)
    def _():
        _row_gather_start(x_hbm, xbuf.at[1 - slot], sem.at[1 - slot], idxn_ref, tm)

    @pl.when(i < n_used)
    def _():
        changed = (i == 0) | (blk_e_ref[i] != blk_e_ref[jnp.maximum(i - 1, 0)])

        @pl.when(changed)
        def _():
            wgu_bf[...] = wgu_ref[0].astype(BF16)
            wdn_bf[...] = wdn_ref[0].astype(BF16)

        _row_gather_wait(x_hbm, xbuf.at[slot], sem.at[slot], tm)
        xb = xbuf[slot].astype(BF16)
        gu = _dot(xb, wgu_bf[...]) + bgu_ref[0]
        gate = jnp.minimum(gu[:, :d_ff], SWIGLU_LIMIT)
        up = jnp.clip(gu[:, d_ff:], -SWIGLU_LIMIT, SWIGLU_LIMIT)
        act = gate * _sigmoid(SWIGLU_ALPHA * gate) * (up + 1.0)
        y = _dot(act.astype(BF16), wdn_bf[...]) + bdn_ref[0]
        y_ref[...] = y * w_ref[0]

    @pl.when(i >= n_used)
    def _():
        y_ref[...] = jnp.zeros_like(y_ref)


def _experts(x1, blk_e, n_used, buf_tok, buf_w, w_gu, b_gu, w_dn, b_dn, tm):
    T, D = x1.shape
    ne, _, d2 = w_gu.shape
    d_ff = d2 // 2
    n_blocks = blk_e.shape[0]
    idx = buf_tok.reshape(n_blocks, 1, tm)
    wcol = buf_w.reshape(n_blocks, tm, 1)
    last = n_blocks - 1
    grid_spec = pltpu.PrefetchScalarGridSpec(
        num_scalar_prefetch=2,
        grid=(n_blocks,),
        in_specs=[pl.BlockSpec((1, 1, tm), lambda i, e, n: (i, 0, 0), memory_space=pltpu.SMEM),
                  pl.BlockSpec((1, 1, tm), lambda i, e, n: (jnp.minimum(i + 1, last), 0, 0),
                               memory_space=pltpu.SMEM),
                  pl.BlockSpec(memory_space=pl.ANY),
                  pl.BlockSpec((1, tm, 1), lambda i, e, n: (i, 0, 0)),
                  pl.BlockSpec((1, D, d2), lambda i, e, n: (e[i], 0, 0)),
                  pl.BlockSpec((1, 1, d2), lambda i, e, n: (e[i], 0, 0)),
                  pl.BlockSpec((1, d_ff, D), lambda i, e, n: (e[i], 0, 0)),
                  pl.BlockSpec((1, 1, D), lambda i, e, n: (e[i], 0, 0))],
        out_specs=pl.BlockSpec((tm, D), lambda i, e, n: (i, 0)),
        scratch_shapes=[pltpu.VMEM((2, tm, D), F32), pltpu.VMEM((D, d2), BF16), pltpu.VMEM((d_ff, D), BF16),
                        pltpu.SemaphoreType.DMA((2,))],
    )
    return pl.pallas_call(
        functools.partial(_expert_kernel, tm=tm, d_ff=d_ff),
        grid_spec=grid_spec,
        out_shape=jax.ShapeDtypeStruct((n_blocks * tm, D), F32),
        compiler_params=_cparams(("arbitrary",), 58 * 1024 * 1024),
        name="moe_experts",
    )(blk_e, n_used, idx, idx, x1, wcol, w_gu, b_gu.reshape(ne, 1, d2), w_dn, b_dn.reshape(ne, 1, D))


def _combine_kernel(idx_ref, idxn_ref, y_hbm, x_ref, lg_ref, lb_ref, o_ref, ybuf, sem, *, tm, alpha):
    i = pl.program_id(0)
    n = pl.num_programs(0)
    slot = i % 2
    rows = TOP_K * tm

    @pl.when(i == 0)
    def _():
        _row_gather_start(y_hbm, ybuf.at[0], sem.at[0], idx_ref, rows)

    @pl.when(i + 1 < n)
    def _():
        _row_gather_start(y_hbm, ybuf.at[1 - slot], sem.at[1 - slot], idxn_ref, rows)

    _row_gather_wait(y_hbm, ybuf.at[slot], sem.at[slot], rows)
    ffn = ybuf[slot, 0:tm]
    for k in range(1, TOP_K):
        ffn = ffn + ybuf[slot, k * tm:(k + 1) * tm]
    o_ref[...] = _layer_norm(alpha * x_ref[...] + ffn, lg_ref[...], lb_ref[...])


def _combine(y_sorted, pos, x1, ln_g, ln_b, alpha):
    T, D = x1.shape
    tm = COMBINE_TM
    nt = T // tm
    idx = pos.reshape(nt, tm, TOP_K).transpose(0, 2, 1).reshape(nt, 1, TOP_K * tm)
    last = nt - 1
    return pl.pallas_call(
        functools.partial(_combine_kernel, tm=tm, alpha=alpha),
        grid=(nt,),
        in_specs=[pl.BlockSpec((1, 1, TOP_K * tm), lambda i: (i, 0, 0), memory_space=pltpu.SMEM),
                  pl.BlockSpec((1, 1, TOP_K * tm), lambda i: (jnp.minimum(i + 1, last), 0, 0),
                               memory_space=pltpu.SMEM),
                  pl.BlockSpec(memory_space=pl.ANY),
                  pl.BlockSpec((tm, D), lambda i: (i, 0)),
                  pl.BlockSpec((1, D), lambda i: (0, 0)),
                  pl.BlockSpec((1, D), lambda i: (0, 0))],
        out_specs=pl.BlockSpec((tm, D), lambda i: (i, 0)),
        out_shape=jax.ShapeDtypeStruct((T, D), F32),
        scratch_shapes=[pltpu.VMEM((2, TOP_K * tm, D), F32), pltpu.SemaphoreType.DMA((2,))],
        compiler_params=_cparams(("arbitrary",)),
        name="moe_combine_ln",
    )(idx, idx, y_sorted, x1, ln_g.reshape(1, D), ln_b.reshape(1, D))


def _route(top_e, top_w, tm):
    T = top_e.shape[0]
    M = T * TOP_K
    i32 = jnp.int32
    flat_e = top_e.reshape(M).astype(i32)
    flat_w = top_w.reshape(M)
    ar = jnp.arange(M, dtype=i32)
    experts = jnp.arange(N_EXPERTS, dtype=i32)
    skey, sorted_w = lax.sort((flat_e * M + ar, flat_w), num_keys=1)
    sorted_e = skey // M
    order = skey - sorted_e * M
    counts = jnp.sum((flat_e[:, None] == experts[None, :]).astype(i32), axis=0)
    start = jnp.cumsum(counts) - counts
    padded = (counts + tm - 1) // tm * tm
    pad_end = jnp.cumsum(padded)
    pad_start = pad_end - padded
    dest = pad_start[sorted_e] + ar - start[sorted_e]
    n_blocks = -(-M // tm) + N_EXPERTS
    blk_first = jnp.arange(n_blocks, dtype=i32) * tm
    blk_e = jnp.minimum(jnp.sum((pad_end[None, :] <= blk_first[:, None]).astype(i32), axis=1), N_EXPERTS - 1)
    within = (blk_first - pad_start[blk_e])[:, None] + jnp.arange(tm, dtype=i32)[None, :]
    valid = within < counts[blk_e][:, None]
    src_row = jnp.clip(start[blk_e][:, None] + within, 0, M - 1)
    buf_tok = jnp.where(valid, order[src_row] // TOP_K, 0).reshape(n_blocks * tm)
    buf_w = jnp.where(valid, sorted_w[src_row], 0.0).reshape(n_blocks * tm)
    _, pos = lax.sort((order, dest), num_keys=1)
    n_used = (pad_end[-1] // tm).astype(i32).reshape(1)
    return blk_e, n_used, buf_tok, buf_w, pos


class _Plan:
    def __init__(self):
        self.src, self.scale, self.const, self.outs = [], [], [], []

    def group(self, width, src_cols=(), at=0, scale=1.0, ones=()):
        src = np.full((width,), -1, np.int64)
        sc = np.zeros((width,), np.float32)
        const = np.zeros((width,), np.float32)
        src[at:at + len(src_cols)] = src_cols
        sc[at:at + len(src_cols)] = scale
        for col, val in ones:
            const[col] = val
        self.src.append(src)
        self.scale.append(sc)
        self.const.append(const)

    def out(self, n_heads, width, dtype):
        self.outs.append((n_heads, width, dtype))

    def arrays(self):
        return (jnp.asarray(np.concatenate(self.src)), jnp.asarray(np.concatenate(self.scale)),
                jnp.asarray(np.concatenate(self.const)))


def _cols(start):
    return np.arange(start, start + HEAD_DIM)


def _value_group(plan, src, odd):
    if odd:
        plan.group(LANES, _cols(src), at=HALF, ones=[(0, 1.0)])
    else:
        plan.group(LANES, _cols(src), ones=[(HALF, 1.0)])


def _plans(S):
    qs = HEAD_DIM ** -0.5
    slope = 2.0 ** (-8.0 * np.arange(1, N_NSA_HEADS + 1) / N_NSA_HEADS)
    a = _Plan()
    for g in range(N_NSA_KV):
        for hl in NSA_ROW_HEADS:
            h = g * NSA_REP + hl
            a.group(LANES, _cols(h * HEAD_DIM), scale=qs, ones=[(HALF, slope[h]), (HALF + 1, slope[h])])
    a.out(N_NSA_HEADS, LANES, BF16)
    kv = lambda branch, which, g: SRC_KV + branch * 256 + which * 128 + g * HEAD_DIM
    for which in range(2):
        for g in range(N_NSA_KV):
            a.group(LANES, _cols(kv(0, which, g)))
    a.out(2 * N_NSA_KV, LANES, F32)
    for g in range(N_NSA_KV):
        a.group(2 * LANES, _cols(kv(1, 0, g)))
    a.out(N_NSA_KV, 2 * LANES, BF16)
    for g in range(N_NSA_KV):
        _value_group(a, kv(1, 1, g), False)
        _value_group(a, kv(1, 1, g), True)
    a.out(N_NSA_KV, 2 * LANES, BF16)
    for g in range(N_NSA_KV):
        a.group(LANES, _cols(kv(2, 0, g)))
    a.out(N_NSA_KV, LANES, BF16)
    for g in range(N_NSA_KV):
        _value_group(a, kv(2, 1, g), False)
        _value_group(a, kv(2, 1, g), True)
    a.out(N_NSA_KV, 2 * LANES, BF16)
    a.group(LANES, np.concatenate([np.arange(SRC_GATE, SRC_GATE + 24), np.arange(SRC_LOGF, SRC_LOGF + 4)]))
    a.out(1, LANES, F32)

    n_a = sum(len(s) for s in a.src)
    extra = np.zeros((S, n_a), np.float32)
    pos = np.arange(S)
    pos_hi, pos_lo = pos // SEL_LEN * SEL_LEN, pos % SEL_LEN
    off_ksel = N_NSA_HEADS * LANES + 2 * N_NSA_KV * LANES
    off_kwin = off_ksel + N_NSA_KV * 2 * LANES + N_NSA_KV * 2 * LANES
    nb = S // SEL_LEN
    for g in range(N_NSA_KV):
        o = off_ksel + g * 2 * LANES
        extra[:, o + HALF], extra[:, o + HALF + 1] = pos_hi, pos_lo
        extra[pos, o + LANES + pos // SEL_LEN] = 1.0
        o = off_kwin + g * LANES
        extra[:, o + HALF], extra[:, o + HALF + 1] = pos_hi, pos_lo
    assert nb <= LANES

    b = _Plan()
    fox = lambda which, h: SRC_FOX + which * 256 + h * HEAD_DIM
    sb = lambda which, h: SRC_SB + which * 256 + h * HEAD_DIM
    for h in range(N_FOX_HEADS):
        b.group(LANES, _cols(fox(0, h)), scale=qs, ones=[(HALF, 1.0), (HALF + 1, 1.0), (HALF + 2, 1.0)])
    b.out(N_FOX_HEADS, LANES, BF16)
    for h in range(N_FOX_HEADS):
        b.group(LANES, _cols(fox(1, h)))
    b.out(N_FOX_HEADS, LANES, BF16)
    for h in range(N_FOX_HEADS):
        _value_group(b, fox(2, h), h % 2 == 1)
    b.out(N_FOX_HEADS, LANES, BF16)
    for h in range(N_SB_HEADS):
        b.group(LANES, _cols(sb(0, h)), scale=qs)
    b.out(N_SB_HEADS, LANES, BF16)
    for h in range(N_SB_HEADS):
        b.group(LANES, _cols(sb(1, h)))
    b.out(N_SB_HEADS, LANES, BF16)
    for h in range(N_SB_HEADS):
        b.group(LANES, _cols(sb(2, h)), at=HALF if h % 2 else 0)
    b.out(N_SB_HEADS, LANES, BF16)
    return a, jnp.asarray(extra, dtype=BF16), b


def _constants(S):
    plan_a, extra_a, plan_b = _plans(S)
    nc = S // CMP_STRIDE
    nb = S // SEL_LEN
    n_cmp = (S - CMP_LEN) // CMP_STRIDE + 1
    cmp_idx = np.arange(nc)[:, None] * CMP_STRIDE + np.arange(CMP_LEN)[None, :]
    ovl = (cmp_idx[:, :, None] // SEL_LEN == np.arange(nb)[None, None, :]).astype(np.float32).mean(axis=1)
    ovl[n_cmp:] = 0.0
    ovl = np.pad(ovl, ((0, 0), (0, LANES - nb)))
    cmp_end = np.arange(nc) * CMP_STRIDE + CMP_LEN - 1
    kcmp_aug = np.zeros((nc, LANES), np.float32)
    kcmp_aug[:, HALF] = cmp_end // SEL_LEN * SEL_LEN
    kcmp_aug[:, HALF + 1] = cmp_end % SEL_LEN
    return dict(a=plan_a.arrays(), a_outs=tuple(plan_a.outs), extra_a=extra_a,
                b=plan_b.arrays(), b_outs=tuple(plan_b.outs),
                ovl_t=jnp.asarray(ovl.T, dtype=BF16), kcmp_aug=jnp.asarray(kcmp_aug))


def _plan_weights(w_in, b_in, plan_arrays):
    src, scale, const = plan_arrays
    used = src >= 0
    safe = jnp.maximum(src, 0)
    w = jnp.where(used[None, :], w_in[:, safe], 0.0) * scale
    b = jnp.where(used, b_in[safe], 0.0) * scale + const
    return w.astype(BF16), b.reshape(1, -1)


def _layer(x, cs, w_in, b_in, pos_k, pos_v, w1k, w2k, w1v, w2v, g_out, w_out, ln1_g, ln1_b,
           router_w, router_b, w_gu, b_gu, w_dn, b_dn, ln2_g, ln2_b, alpha):
    B, S, D = x.shape
    T = B * S
    x2d = x.reshape(T, D)
    wa, ba = _plan_weights(w_in, b_in, cs["a"])
    wb, bb = _plan_weights(w_in, b_in, cs["b"])
    q2, kvc, ksel, vsel, kwin, vwin, small = _proj(x2d, wa, ba, cs["extra_a"], cs["a_outs"], B, S)
    fq, fk, fv, sq, sk, sv = _proj(x2d, wb, bb, None, cs["b_outs"], B, S)
    small = small.reshape(B, S, LANES)

    kc = kvc[:, 0:N_NSA_KV, :, 0:HEAD_DIM]
    vc = kvc[:, N_NSA_KV:, :, 0:HEAD_DIM]
    k_cmp, v_cmp = _compress(kc, vc, pos_k, pos_v, w1k, w2k, w1v, w2v, cs["kcmp_aug"])
    o_c, msel = _nsa_cmp(q2, k_cmp, v_cmp, cs["ovl_t"], small)
    o_s = _nsa_sel(q2, msel, ksel, vsel, small, min(SEL_KB, S))
    o_w = _nsa_win(q2, kwin, vwin, small)

    c = _logf_cumsum(small[..., 24:24 + N_FOX_HEADS].transpose(0, 2, 1))
    c_hi = _trunc_bf16(c)
    c_mid = _trunc_bf16(c - c_hi)
    c_lo = c - c_hi - c_mid
    c_aug = jnp.pad(-jnp.stack([c_hi, c_mid, c_lo], axis=-1), ((0, 0), (0, 0), (0, 0), (HALF, HALF - 3)))
    o_fox = _fox(fq, fk + c_aug.astype(BF16), fv)
    o_sb = _sb(sq, sk, sv)

    flat = lambda o: o.reshape(T, o.shape[-1])
    x1, te, tw = _post(flat(o_c), flat(o_s), flat(o_w), flat(o_fox), flat(o_sb), x2d, g_out, w_out.astype(BF16),
                       ln1_g, ln1_b, router_w, router_b, alpha)
    blk_e, n_used, buf_tok, buf_w, pos = _route(te[:, :TOP_K], tw[:, :TOP_K], MOE_TM)
    y_sorted = _experts(x1, blk_e, n_used, buf_tok, buf_w, w_gu, b_gu, w_dn, b_dn, MOE_TM)
    out = _combine(y_sorted, pos, x1, ln2_g, ln2_b, alpha)
    return out.reshape(B, S, D)


def kernel(x, w_in, b_in, cmp_pos_k, cmp_pos_v, cmp_w1_k, cmp_w2_k, cmp_w1_v, cmp_w2_v, g_out, w_out,
           ln1_g, ln1_b, router_w, router_b, w_gate_up, b_gate_up, w_down, b_down, ln2_g, ln2_b):
    depth = w_in.shape[0]
    alpha = (2 * depth) ** 0.25
    consts = _constants(x.shape[1])
    for l in range(depth):
        x = _layer(x, consts, w_in[l], b_in[l], cmp_pos_k[l], cmp_pos_v[l], cmp_w1_k[l], cmp_w2_k[l],
                   cmp_w1_v[l], cmp_w2_v[l], g_out[l], w_out[l], ln1_g[l], ln1_b[l], router_w[l], router_b[l],
                   w_gate_up[l], b_gate_up[l], w_down[l], b_down[l], ln2_g[l], ln2_b[l], alpha)
    return x
```

```python
import functools

import jax
import jax.numpy as jnp
import numpy as np
from jax import lax
from jax.experimental import pallas as pl
from jax.experimental.pallas import tpu as pltpu

F32 = jnp.float32
BF16 = jnp.bfloat16

HEAD_DIM = 64
N_NSA_HEADS = 8
N_NSA_KV = 2
NSA_REP = 4
N_FOX_HEADS = 4
N_SB_HEADS = 4
D_NSA = 512
D_FOX = 256
D_SB = 256
CMP_LEN = 32
CMP_STRIDE = 16
CMP_HIDDEN = 128
SEL_LEN = 64
SEL_TOPK = 16
WINDOW = 512
Q_BLOCK = 128
N_EXPERTS = 32
TOP_K = 4
SWIGLU_LIMIT = 7.0
SWIGLU_ALPHA = 1.702
LN_EPS = 1e-5
RMS_EPS = 1e-6
NEG = -1e30
BIG = 3e38
SB_CUTOFF = -90.0

LANES = 128
HALF = LANES // 2
NSA_ROW_HEADS = (0, 2, 1, 3)

SRC_KV = 512
SRC_GATE = SRC_KV + 768
SRC_FOX = SRC_GATE + 24
SRC_LOGF = SRC_FOX + 768
SRC_SB = SRC_LOGF + 4

VMEM_LIMIT = 52 * 1024 * 1024

PROJ_TM = 512
CMP_SUB = 2
SEL_KB = 256
FOX_TQ = 512
FOX_KB = 256
SB_TQ = 128
POST_TM = 256
MOE_TM = 256
COMBINE_TM = 128


def _cparams(sem, vmem=VMEM_LIMIT):
    return pltpu.CompilerParams(dimension_semantics=sem, vmem_limit_bytes=vmem)


def _dot(a, b):
    return jnp.dot(a, b, preferred_element_type=F32)


def _dot_nt(a, b):
    return lax.dot_general(a, b, (((1,), (1,)), ((), ())), preferred_element_type=F32)


def _split_bf16(x):
    hi = x.astype(BF16)
    lo = (x - hi.astype(F32)).astype(BF16)
    return hi, lo


def _trunc_bf16(x):
    bits = lax.bitcast_convert_type(x, jnp.uint32) & jnp.uint32(0xFFFF0000)
    return lax.bitcast_convert_type(bits, F32)


def _sigmoid(x):
    return 1.0 / (1.0 + jnp.exp(-x))


def _lane_column(tile, idx):
    lane = lax.broadcasted_iota(jnp.int32, tile.shape, 1)
    return jnp.sum(jnp.where(lane == idx, tile, 0.0), axis=1, keepdims=True)


def _low_half(shape):
    return lax.broadcasted_iota(jnp.int32, shape, 1) < HALF


def _proj_kernel(*refs, outs, has_extra):
    x_ref, w_ref, b_ref = refs[:3]
    o_refs = refs[3 + has_extra:]
    y = _dot(x_ref[...].astype(BF16), w_ref[...]) + b_ref[...]
    if has_extra:
        y = y + refs[3][...].astype(F32)
    off = 0
    for o_ref, (nh, width, _) in zip(o_refs, outs):
        for h in range(nh):
            o_ref[0, h] = y[:, off:off + width].astype(o_ref.dtype)
            off += width


def _proj(x2d, w_bf, b_row, extra, outs, B, S):
    T, D = x2d.shape
    N = w_bf.shape[1]
    tm = min(PROJ_TM, S)
    spb = S // tm
    in_specs = [pl.BlockSpec((tm, D), lambda i: (i, 0)),
                pl.BlockSpec((D, N), lambda i: (0, 0)),
                pl.BlockSpec((1, N), lambda i: (0, 0))]
    args = [x2d, w_bf, b_row]
    if extra is not None:
        in_specs.append(pl.BlockSpec((tm, N), lambda i: (i % spb, 0)))
        args.append(extra)
    return pl.pallas_call(
        functools.partial(_proj_kernel, outs=outs, has_extra=extra is not None),
        grid=(T // tm,),
        in_specs=in_specs,
        out_specs=[pl.BlockSpec((1, nh, tm, w), lambda i: (i // spb, 0, i % spb, 0)) for nh, w, _ in outs],
        out_shape=[jax.ShapeDtypeStruct((B, nh, S, w), dt) for nh, w, dt in outs],
        compiler_params=_cparams(("parallel",)),
        name="proj",
    )(*args)


def _gelu_tanh(x):
    return 0.5 * x * (1.0 + jnp.tanh(0.7978845608028654 * (x + 0.044715 * (x * x * x))))


def _compress_kernel(k_ref, v_ref, pk_ref, pv_ref, w1k_ref, w2k_ref, w1v_ref, w2v_ref, ka_ref, ok_ref, ov_ref):
    nc = k_ref.shape[2]
    half = CMP_STRIDE * HEAD_DIM

    def one(r_ref, p_ref, w1_ref, w2_ref):
        r = r_ref[0, 0]
        a = (r + p_ref[0:1, :]).astype(BF16)
        b = (r + p_ref[1:2, :]).astype(BF16)
        ha = _dot(a, w1_ref[0:half, :])
        hb = _dot(b, w1_ref[half:2 * half, :])
        hid = _gelu_tanh(ha + pltpu.roll(hb, nc - 1, 0))
        return _dot(hid.astype(BF16), w2_ref[...])

    ok_ref[0, 0] = (one(k_ref, pk_ref, w1k_ref, w2k_ref) + ka_ref[...]).astype(ok_ref.dtype)
    ov_ref[0, 0] = one(v_ref, pv_ref, w1v_ref, w2v_ref).astype(ov_ref.dtype)


def _compress(kc, vc, pos_k, pos_v, w1k, w2k, w1v, w2v, kcmp_aug):
    B, G, S, dh = kc.shape
    nc = S // CMP_STRIDE
    width = CMP_STRIDE * dh
    kr = kc.reshape(B, G, nc, width)
    vr = vc.reshape(B, G, nc, width)
    pk = pos_k.reshape(2, width)
    pv = pos_v.reshape(2, width)
    w2k_p = jnp.pad(w2k, ((0, 0), (0, LANES - dh))).astype(BF16)
    w2v_p = jnp.concatenate([jnp.pad(w2v, ((0, 0), (0, LANES - dh))), jnp.pad(w2v, ((0, 0), (LANES - dh, 0)))],
                            axis=1).astype(BF16)
    kv_spec = pl.BlockSpec((1, 1, nc, width), lambda b, g: (b, g, 0, 0))
    full = lambda shape: pl.BlockSpec(shape, lambda b, g: tuple(0 for _ in shape))
    return pl.pallas_call(
        _compress_kernel,
        grid=(B, G),
        in_specs=[kv_spec, kv_spec, full((2, width)), full((2, width)),
                  full((2 * width, CMP_HIDDEN)), full((CMP_HIDDEN, LANES)),
                  full((2 * width, CMP_HIDDEN)), full((CMP_HIDDEN, 2 * LANES)), full((nc, LANES))],
        out_specs=[pl.BlockSpec((1, 1, nc, LANES), lambda b, g: (b, g, 0, 0)),
                   pl.BlockSpec((1, 1, nc, 2 * LANES), lambda b, g: (b, g, 0, 0))],
        out_shape=[jax.ShapeDtypeStruct((B, G, nc, LANES), BF16),
                   jax.ShapeDtypeStruct((B, G, nc, 2 * LANES), BF16)],
        compiler_params=_cparams(("parallel", "parallel")),
        name="nsa_compress",
    )(kr, vr, pk, pv, w1k.astype(BF16), w2k_p, w1v.astype(BF16), w2v_p, kcmp_aug)


def _nsa_gates(small, branch, g):
    return [_sigmoid(_lane_column(small, branch * N_NSA_HEADS + g * NSA_REP + h)) for h in NSA_ROW_HEADS]


def _nsa_cmp_kernel(q_ref, kc_ref, vc_ref, ovl_ref, sm_ref, o_ref, sel_ref, *, n_sel, n_sub):
    g = pl.program_id(1)
    i = pl.program_id(2)
    results = [_nsa_cmp_block(q_ref, kc_ref, vc_ref, ovl_ref, sm_ref, g, i * n_sub + u, u, n_sel)
               for u in range(n_sub)]
    for u, (o, msel) in enumerate(results):
        o_ref[0, u * Q_BLOCK:(u + 1) * Q_BLOCK, :] = o
        sel_ref[0, 0, u] = msel


def _nsa_cmp_block(q_ref, kc_ref, vc_ref, ovl_ref, sm_ref, g, qb, u, n_sel):
    q0 = qb * Q_BLOCK
    rows = slice(u * Q_BLOCK, (u + 1) * Q_BLOCK)
    nc = kc_ref.shape[2]
    nb = ovl_ref.shape[0]
    kc = kc_ref[0, 0]
    vc = vc_ref[0, 0]
    gates = _nsa_gates(sm_ref[0, rows, :], 0, g)
    t = q0 + lax.broadcasted_iota(jnp.int32, (Q_BLOCK, nc), 0)
    cmp_end = lax.broadcasted_iota(jnp.int32, (Q_BLOCK, nc), 1) * CMP_STRIDE + (CMP_LEN - 1)
    mask = cmp_end <= t
    psum = jnp.zeros((Q_BLOCK, nc), F32)
    outs = []
    for r in range(NSA_REP):
        s = jnp.where(mask, _dot_nt(q_ref[0, r, rows, :], kc), NEG)
        m = jnp.max(s, axis=1, keepdims=True)
        p = jnp.where(mask, jnp.exp(s - m), 0.0)
        l = jnp.sum(p, axis=1, keepdims=True)
        pn = p / jnp.where(l > 0.0, l, 1.0)
        v_half = vc[:, 0:LANES] if r < 2 else vc[:, LANES:2 * LANES]
        outs.append(_dot(pn.astype(BF16), v_half) * gates[r])
        psum = psum + pn
    o = jnp.concatenate([outs[0] + outs[2], outs[1] + outs[3]], axis=1)
    hi, lo = _split_bf16(psum)
    ovl = ovl_ref[...]
    imp = _dot_nt(ovl, hi) + _dot_nt(ovl, lo)
    jf = lax.broadcasted_iota(jnp.int32, (nb, Q_BLOCK), 0)
    tq = q0 + lax.broadcasted_iota(jnp.int32, (nb, Q_BLOCK), 1)
    cur = tq >> 6
    valid = jf <= cur
    forced = valid & ((jf == 0) | (jf >= cur - 1))
    v = jnp.where(valid & ~forced, imp, -BIG)
    jff = jf.astype(F32)
    picked = jnp.where(forced, 1.0, 0.0)
    for _ in range(max(n_sel - 3, 0)):
        mx = jnp.max(v, axis=0, keepdims=True)
        idx = jnp.min(jnp.where(v == mx, jff, float(nb)), axis=0, keepdims=True)
        hit = jff == idx
        picked = jnp.where(hit, 1.0, picked)
        v = jnp.where(hit, -BIG, v)
    msel = jnp.where(valid & (picked > 0.5), 0.0, NEG)
    return o, msel.T.astype(BF16)


def _nsa_cmp(q2, kcmp, vcmp, ovl_t, small):
    B, H, S, qw = q2.shape
    G = N_NSA_KV
    nq = S // Q_BLOCK
    nc = kcmp.shape[2]
    nb = ovl_t.shape[0]
    n_sel = min(SEL_TOPK, S // SEL_LEN)
    n_sub = CMP_SUB
    return pl.pallas_call(
        functools.partial(_nsa_cmp_kernel, n_sel=n_sel, n_sub=n_sub),
        grid=(B, G, nq // n_sub),
        in_specs=[pl.BlockSpec((1, NSA_REP, n_sub * Q_BLOCK, qw), lambda b, g, i: (b, g, i, 0)),
                  pl.BlockSpec((1, 1, nc, LANES), lambda b, g, i: (b, g, 0, 0)),
                  pl.BlockSpec((1, 1, nc, 2 * LANES), lambda b, g, i: (b, g, 0, 0)),
                  pl.BlockSpec((nb, nc), lambda b, g, i: (0, 0)),
                  pl.BlockSpec((1, n_sub * Q_BLOCK, LANES), lambda b, g, i: (b, i, 0))],
        out_specs=[pl.BlockSpec((1, n_sub * Q_BLOCK, 2 * LANES), lambda b, g, i: (b, i, g)),
                   pl.BlockSpec((1, 1, n_sub, Q_BLOCK, nb), lambda b, g, i: (b, g, i, 0, 0))],
        out_shape=[jax.ShapeDtypeStruct((B, S, D_NSA), F32),
                   jax.ShapeDtypeStruct((B, G, nq, Q_BLOCK, nb), BF16)],
        compiler_params=_cparams(("parallel", "parallel", "parallel")),
        name="nsa_cmp_select",
    )(q2, kcmp, vcmp, ovl_t, small)


def _flash_init(rows):
    return jnp.full((rows, 1), NEG, F32), jnp.zeros((rows, LANES), F32)


def _flash_update(carry, s, v_tile):
    m, acc = carry
    half = s.shape[0] // 2
    m_new = jnp.maximum(m, jnp.max(s, axis=1, keepdims=True))
    p = jnp.exp(s - m_new).astype(BF16)
    pv = jnp.concatenate([_dot(p[:half], v_tile[:, 0:LANES]), _dot(p[half:], v_tile[:, LANES:2 * LANES])], axis=0)
    return m_new, jnp.exp(m - m_new) * acc + pv


def _pair_out(acc_even, acc_odd):
    even = acc_even / acc_even[:, HALF:HALF + 1]
    odd = acc_odd / acc_odd[:, 0:1]
    return jnp.where(_low_half(acc_even.shape), even, odd)


def _nsa_out(acc, gates):
    c = [acc[r * Q_BLOCK:(r + 1) * Q_BLOCK] for r in range(NSA_REP)]
    low = _low_half((Q_BLOCK, LANES))
    pair_a = _pair_out(c[0], c[2]) * jnp.where(low, gates[0], gates[2])
    pair_b = _pair_out(c[1], c[3]) * jnp.where(low, gates[1], gates[3])
    return jnp.concatenate([pair_a, pair_b], axis=1)


def _stacked_rows(rows, kb):
    row = lax.broadcasted_iota(jnp.int32, (rows, kb), 0) & (Q_BLOCK - 1)
    col = lax.broadcasted_iota(jnp.int32, (rows, kb), 1)
    return row, col


def _nsa_sel_kernel(q_ref, msel_ref, k_ref, v_ref, sm_ref, o_ref, qa_ref, sa_ref, sb_ref, *, kb):
    g = pl.program_id(1)
    i = pl.program_id(2)
    q0 = i * Q_BLOCK
    rows = NSA_REP * Q_BLOCK
    msel = msel_ref[0, 0, 0]
    for r in range(NSA_REP):
        qa_ref[r * Q_BLOCK:(r + 1) * Q_BLOCK, :] = jnp.concatenate([q_ref[0, r], msel], axis=1)
    qa = qa_ref[...]
    last = (q0 + Q_BLOCK - 1) // kb

    def logits(kt):
        k0 = pl.multiple_of(kt * kb, kb)
        return _dot_nt(qa, k_ref[0, 0, pl.ds(k0, kb), :])

    def v_tile(kt):
        return v_ref[0, 0, pl.ds(pl.multiple_of(kt * kb, kb), kb), :]

    sa_ref[...] = logits(0)

    def body(j, carry):
        sb_ref[...] = logits(2 * j + 1)
        carry = _flash_update(carry, sa_ref[...], v_tile(2 * j))
        sa_ref[...] = logits(2 * j + 2)
        return _flash_update(carry, sb_ref[...], v_tile(2 * j + 1))

    pairs = last // 2
    carry = lax.fori_loop(0, pairs, body, _flash_init(rows))
    row, col = _stacked_rows(rows, kb)
    causal = (last * kb + col) <= (q0 + row)

    def tail_odd(carry):
        carry = _flash_update(carry, sa_ref[...], v_tile(last - 1))
        return _flash_update(carry, jnp.where(causal, logits(last), NEG), v_tile(last))

    def tail_even(carry):
        return _flash_update(carry, jnp.where(causal, sa_ref[...], NEG), v_tile(last))

    _, acc = lax.cond(last - 2 * pairs == 1, tail_odd, tail_even, carry)
    o_ref[0] = _nsa_out(acc, _nsa_gates(sm_ref[0], 1, g))


def _nsa_sel(q2, msel, k2, v2, small, kb):
    B, H, S, qw = q2.shape
    G = N_NSA_KV
    nq = S // Q_BLOCK
    nb = msel.shape[-1]
    return pl.pallas_call(
        functools.partial(_nsa_sel_kernel, kb=kb),
        grid=(B, G, nq),
        in_specs=[pl.BlockSpec((1, NSA_REP, Q_BLOCK, qw), lambda b, g, i: (b, g, i, 0)),
                  pl.BlockSpec((1, 1, 1, Q_BLOCK, nb), lambda b, g, i: (b, g, i, 0, 0)),
                  pl.BlockSpec((1, 1, S, k2.shape[-1]), lambda b, g, i: (b, g, 0, 0)),
                  pl.BlockSpec((1, 1, S, 2 * LANES), lambda b, g, i: (b, g, 0, 0)),
                  pl.BlockSpec((1, Q_BLOCK, LANES), lambda b, g, i: (b, i, 0))],
        out_specs=pl.BlockSpec((1, Q_BLOCK, 2 * LANES), lambda b, g, i: (b, i, g)),
        out_shape=jax.ShapeDtypeStruct((B, S, D_NSA), F32),
        scratch_shapes=[pltpu.VMEM((NSA_REP * Q_BLOCK, qw + nb), BF16),
                        pltpu.VMEM((NSA_REP * Q_BLOCK, kb), F32),
                        pltpu.VMEM((NSA_REP * Q_BLOCK, kb), F32)],
        compiler_params=_cparams(("parallel", "parallel", "parallel")),
        name="nsa_selected",
    )(q2, msel, k2, v2, small)


def _nsa_win_kernel(q_ref, k_ref, v_ref, sm_ref, o_ref):
    g = pl.program_id(1)
    i = pl.program_id(2)
    kb = Q_BLOCK
    rows = NSA_REP * Q_BLOCK
    qa = q_ref[0].reshape(rows, q_ref.shape[-1])
    n_back = WINDOW // kb
    row, col = _stacked_rows(rows, kb)
    carry = _flash_init(rows)
    for d in range(n_back + 1):
        kt = i - n_back + d
        k0 = pl.multiple_of(jnp.maximum(kt, 0) * kb, kb)
        s = _dot_nt(qa, k_ref[0, 0, pl.ds(k0, kb), :])
        if d == 0:
            s = jnp.where((col > row) & (kt >= 0), s, NEG)
        elif d == n_back:
            s = jnp.where(col <= row, s, NEG)
        else:
            s = jnp.where(kt >= 0, s, NEG)
        carry = _flash_update(carry, s, v_ref[0, 0, pl.ds(k0, kb), :])
    o_ref[0] = _nsa_out(carry[1], _nsa_gates(sm_ref[0], 2, g))


def _nsa_win(q2, k2, v2, small):
    B, H, S, qw = q2.shape
    G = N_NSA_KV
    nq = S // Q_BLOCK
    return pl.pallas_call(
        _nsa_win_kernel,
        grid=(B, G, nq),
        in_specs=[pl.BlockSpec((1, NSA_REP, Q_BLOCK, qw), lambda b, g, i: (b, g, i, 0)),
                  pl.BlockSpec((1, 1, S, qw), lambda b, g, i: (b, g, 0, 0)),
                  pl.BlockSpec((1, 1, S, 2 * LANES), lambda b, g, i: (b, g, 0, 0)),
                  pl.BlockSpec((1, Q_BLOCK, LANES), lambda b, g, i: (b, i, 0))],
        out_specs=pl.BlockSpec((1, Q_BLOCK, 2 * LANES), lambda b, g, i: (b, i, g)),
        out_shape=jax.ShapeDtypeStruct((B, S, D_NSA), F32),
        compiler_params=_cparams(("parallel", "parallel", "parallel")),
        name="nsa_window",
    )(q2, k2, v2, small)


def _fox_update(carry, s, v_tile):
    m, acc = carry
    m_new = jnp.maximum(m, jnp.max(s, axis=1, keepdims=True))
    p = jnp.exp(s - m_new).astype(BF16)
    return m_new, jnp.exp(m - m_new) * acc + _dot(p, v_tile)


def _fox_kernel(q_ref, k_ref, v_ref, o_ref, *s_refs, tq, kb):
    i = pl.program_id(2)

    def logits(h, kt):
        k0 = pl.multiple_of(kt * kb, kb)
        return _dot_nt(q_ref[0, h], k_ref[0, h, pl.ds(k0, kb), :])

    def v_tile(h, kt):
        return v_ref[0, h, pl.ds(pl.multiple_of(kt * kb, kb), kb), :]

    for h in range(2):
        s_refs[2 * h][...] = logits(h, 0)

    def body(j, carries):
        out = []
        for h in range(2):
            sa_ref, sb_ref = s_refs[2 * h], s_refs[2 * h + 1]
            sb_ref[...] = logits(h, 2 * j + 1)
            c = _fox_update(carries[h], sa_ref[...], v_tile(h, 2 * j))
            sa_ref[...] = logits(h, 2 * j + 2)
            out.append(_fox_update(c, sb_ref[...], v_tile(h, 2 * j + 1)))
        return tuple(out)

    carries = lax.fori_loop(0, i, body, (_flash_init(tq), _flash_init(tq)))
    row = lax.broadcasted_iota(jnp.int32, (tq, kb), 0)
    col = lax.broadcasted_iota(jnp.int32, (tq, kb), 1)
    accs = []
    for h in range(2):
        s1 = logits(h, 2 * i + 1)
        c = _fox_update(carries[h], jnp.where(col <= row, s_refs[2 * h][...], NEG), v_tile(h, 2 * i))
        c = _fox_update(c, jnp.where(kb + col <= row, s1, NEG), v_tile(h, 2 * i + 1))
        accs.append(c[1])
    o_ref[0] = _pair_out(accs[0], accs[1])


def _fox(q2, k2, v2):
    B, H, S, qw = q2.shape
    tq, kb = min(FOX_TQ, S), min(FOX_KB, S)
    assert tq == 2 * kb
    return pl.pallas_call(
        functools.partial(_fox_kernel, tq=tq, kb=kb),
        grid=(B, H // 2, S // tq),
        in_specs=[pl.BlockSpec((1, 2, tq, qw), lambda b, h, i: (b, h, i, 0)),
                  pl.BlockSpec((1, 2, S, qw), lambda b, h, i: (b, h, 0, 0)),
                  pl.BlockSpec((1, 2, S, LANES), lambda b, h, i: (b, h, 0, 0))],
        out_specs=pl.BlockSpec((1, tq, LANES), lambda b, h, i: (b, i, h)),
        out_shape=jax.ShapeDtypeStruct((B, S, D_FOX), F32),
        scratch_shapes=[pltpu.VMEM((tq, kb), F32)] * 4,
        compiler_params=_cparams(("parallel", "parallel", "parallel")),
        name="fox_attention",
    )(q2, k2, v2)


def _logf_cumsum_kernel(x_ref, c_ref):
    x = x_ref[0]
    nr, nl = x.shape
    log_f = jnp.minimum(x, 0.0) - jnp.log1p(jnp.exp(-jnp.abs(x)))
    hp = lax.Precision.HIGHEST
    incl = (lax.broadcasted_iota(jnp.int32, (nl, nl), 0) <= lax.broadcasted_iota(jnp.int32, (nl, nl), 1))
    within = jnp.dot(log_f, incl.astype(F32), preferred_element_type=F32, precision=hp)
    totals = jnp.broadcast_to(within[:, nl - 1:nl], (nr, nl))
    before = (lax.broadcasted_iota(jnp.int32, (nr, nr), 1) < lax.broadcasted_iota(jnp.int32, (nr, nr), 0))
    c_ref[0] = within + jnp.dot(before.astype(F32), totals, preferred_element_type=F32, precision=hp)


def _logf_cumsum(logits):
    B, H, S = logits.shape
    x = logits.reshape(B * H, S // LANES, LANES)
    c = pl.pallas_call(
        _logf_cumsum_kernel,
        grid=(B * H,),
        in_specs=[pl.BlockSpec((1, S // LANES, LANES), lambda i: (i, 0, 0))],
        out_specs=pl.BlockSpec((1, S // LANES, LANES), lambda i: (i, 0, 0)),
        out_shape=jax.ShapeDtypeStruct(x.shape, F32),
        compiler_params=_cparams(("parallel",)),
        name="logf_cumsum",
    )(x)
    return c.reshape(B, H, S)


def _sb_kernel(q_ref, k_ref, v_ref, o_ref, *, tq, nh):
    i = pl.program_id(1)
    q0 = i * tq
    trow = q0 + lax.broadcasted_iota(jnp.int32, (tq, tq), 0)
    col = lax.broadcasted_iota(jnp.int32, (tq, tq), 1)
    rr = lax.broadcasted_iota(jnp.int32, (tq, tq), 0)
    upper = (rr > col).astype(BF16)

    def cond(state):
        kt, carries, _ = state
        alive = jnp.max(carries[0])
        for h in range(1, nh):
            alive = jnp.maximum(alive, jnp.max(carries[h]))
        return (kt >= 0) & (alive > SB_CUTOFF)

    def body(state):
        kt, carries, accs = state
        k0 = pl.multiple_of(kt * tq, tq)
        strict = (k0 + col) < trow
        new_c, new_a = [], []
        for h in range(nh):
            z = _dot_nt(q_ref[0, h], k_ref[0, h, pl.ds(k0, tq), :])
            log_beta = jnp.minimum(z, 0.0) - jnp.log1p(jnp.exp(-jnp.abs(z)))
            log_keep = jnp.where(strict, log_beta - z, 0.0)
            hi, lo = _split_bf16(log_keep)
            later = _dot(hi, upper) + _dot(lo, upper)
            a = jnp.where(strict, jnp.exp(log_beta + later + carries[h]), 0.0)
            new_a.append(accs[h] + _dot(a.astype(BF16), v_ref[0, h, pl.ds(k0, tq), :]))
            new_c.append(carries[h] + jnp.sum(log_keep, axis=1, keepdims=True))
        return kt - 1, tuple(new_c), tuple(new_a)

    state = (i, tuple(jnp.zeros((tq, 1), F32) for _ in range(nh)),
             tuple(jnp.zeros((tq, LANES), F32) for _ in range(nh)))
    _, _, accs = lax.while_loop(cond, body, state)
    o_ref[0] = jnp.concatenate([accs[2 * j] + accs[2 * j + 1] for j in range(nh // 2)], axis=1)


def _sb(q2, k2, v2):
    B, H, S, w = q2.shape
    tq = min(SB_TQ, S)
    return pl.pallas_call(
        functools.partial(_sb_kernel, tq=tq, nh=H),
        grid=(B, S // tq),
        in_specs=[pl.BlockSpec((1, H, tq, w), lambda b, i: (b, 0, i, 0)),
                  pl.BlockSpec((1, H, S, w), lambda b, i: (b, 0, 0, 0)),
                  pl.BlockSpec((1, H, S, w), lambda b, i: (b, 0, 0, 0))],
        out_specs=pl.BlockSpec((1, tq, D_SB), lambda b, i: (b, i, 0)),
        out_shape=jax.ShapeDtypeStruct((B, S, D_SB), F32),
        compiler_params=_cparams(("parallel", "parallel")),
        name="sb_attention",
    )(q2, k2, v2)


def _layer_norm(y, g, b):
    mu = jnp.mean(y, axis=1, keepdims=True)
    d = y - mu
    var = jnp.mean(d * d, axis=1, keepdims=True)
    return d * lax.rsqrt(var + LN_EPS) * g + b


def _rms(o, g):
    return o * lax.rsqrt(jnp.mean(o * o, axis=1, keepdims=True) + RMS_EPS) * g


def _post_kernel(oc_ref, osel_ref, ow_ref, of_ref, os_ref, x_ref, go_ref, wo_ref, lg_ref, lb_ref, rw_ref, rb_ref,
                 x1_ref, te_ref, tw_ref, *, alpha):
    o_nsa = oc_ref[...] + osel_ref[...] + ow_ref[...]
    n1 = _rms(o_nsa, go_ref[:, 0:D_NSA]).astype(BF16)
    n2 = _rms(of_ref[...], go_ref[:, D_NSA:D_NSA + D_FOX]).astype(BF16)
    n3 = _rms(os_ref[...], go_ref[:, D_NSA + D_FOX:]).astype(BF16)
    mix = (_dot(n1, wo_ref[0:D_NSA, :]) + _dot(n2, wo_ref[D_NSA:D_NSA + D_FOX, :])
           + _dot(n3, wo_ref[D_NSA + D_FOX:, :]))
    x1 = _layer_norm(alpha * x_ref[...] + mix, lg_ref[...], lb_ref[...])
    x1_ref[...] = x1
    logits = jnp.dot(x1, rw_ref[...], preferred_element_type=F32, precision=lax.Precision.HIGHEST) + rb_ref[...]
    tm, ne = logits.shape
    lane = lax.broadcasted_iota(jnp.int32, (tm, ne), 1).astype(F32)
    wide = lax.broadcasted_iota(jnp.int32, (tm, LANES), 1)
    top_e = jnp.zeros((tm, LANES), F32)
    top_l = jnp.full((tm, LANES), NEG, F32)
    cur = logits
    for k in range(TOP_K):
        mx = jnp.max(cur, axis=1, keepdims=True)
        idx = jnp.min(jnp.where(cur == mx, lane, float(ne)), axis=1, keepdims=True)
        top_e = jnp.where(wide == k, idx, top_e)
        top_l = jnp.where(wide == k, mx, top_l)
        cur = jnp.where(lane == idx, -BIG, cur)
    pe = jnp.exp(top_l - jnp.max(top_l, axis=1, keepdims=True))
    te_ref[...] = top_e.astype(jnp.int32)
    tw_ref[...] = pe / jnp.sum(pe, axis=1, keepdims=True)


def _post(o_c, o_s, o_w, o_fox, o_sb, x2d, g_out, w_out_bf, ln_g, ln_b, router_w, router_b, alpha):
    T, D = x2d.shape
    tm = POST_TM
    row = lambda w: pl.BlockSpec((tm, w), lambda i: (i, 0))
    full = lambda shape: pl.BlockSpec(shape, lambda i: (0, 0))
    return pl.pallas_call(
        functools.partial(_post_kernel, alpha=alpha),
        grid=(T // tm,),
        in_specs=[row(D_NSA), row(D_NSA), row(D_NSA), row(D_FOX), row(D_SB), row(D), full((1, D)), full((D, D)),
                  full((1, D)), full((1, D)), full((D, N_EXPERTS)), full((1, N_EXPERTS))],
        out_specs=[row(D), row(LANES), row(LANES)],
        out_shape=[jax.ShapeDtypeStruct((T, D), F32), jax.ShapeDtypeStruct((T, LANES), jnp.int32),
                   jax.ShapeDtypeStruct((T, LANES), F32)],
        compiler_params=_cparams(("parallel",)),
        name="outproj_ln_router",
    )(o_c, o_s, o_w, o_fox, o_sb, x2d, g_out.reshape(1, D), w_out_bf, ln_g.reshape(1, D), ln_b.reshape(1, D),
      router_w, router_b.reshape(1, N_EXPERTS))


def _row_gather_start(src_hbm, dst, sem, idx_ref, n_rows):
    def issue(r8, c):
        for u in range(8):
            r = r8 * 8 + u
            pltpu.make_async_copy(src_hbm.at[pl.ds(idx_ref[0, 0, r], 1)], dst.at[pl.ds(r, 1)],
                                  sem).start(priority=u % 2)
        return c
    lax.fori_loop(0, n_rows // 8, issue, 0)


def _row_gather_wait(src_hbm, dst, sem, n_rows):
    pltpu.make_async_copy(src_hbm.at[pl.ds(0, n_rows)], dst, sem).wait()


def _expert_kernel(blk_e_ref, nused_ref, idx_ref, idxn_ref, x_hbm, w_ref, wgu_ref, bgu_ref, wdn_ref, bdn_ref,
                   y_ref, xbuf, wgu_bf, wdn_bf, sem, *, tm, d_ff):
    i = pl.program_id(0)
    n_used = nused_ref[0]
    slot = i % 2

    @pl.when(i == 0)
    def _():
        _row_gather_start(x_hbm, xbuf.at[0], sem.at[0], idx_ref, tm)

    @pl.when(i <= n_used)
    def _():
        _row_gather_wait(x_hbm, xbuf.at[slot], sem.at[slot], tm)

    @pl.when(i < n_used)
    def _():
        changed = (i == 0) | (blk_e_ref[i] != blk_e_ref[jnp.maximum(i - 1, 0)])

        @pl.when(changed)
        def _():
            wgu_bf[...] = wgu_ref[0, 0].astype(BF16)
            wdn_bf[...] = wdn_ref[0, 0].astype(BF16)

        nxt = xbuf.at[1 - slot]
        for r in range(tm):
            pltpu.make_async_copy(x_hbm.at[pl.ds(idxn_ref[0, 0, r], 1)], nxt.at[pl.ds(r, 1)],
                                  sem.at[1 - slot]).start(priority=r % 2)
        xb = xbuf[slot].astype(BF16)
        gu = _dot(xb, wgu_bf[...]) + bgu_ref[0, 0]
        gate = jnp.minimum(gu[:, :d_ff], SWIGLU_LIMIT)
        up = jnp.clip(gu[:, d_ff:], -SWIGLU_LIMIT, SWIGLU_LIMIT)
        act = gate * _sigmoid(SWIGLU_ALPHA * gate) * (up + 1.0)
        y = _dot(act.astype(BF16), wdn_bf[...]) + bdn_ref[0, 0]
        y_ref[...] = y * w_ref[0]

    @pl.when(i >= n_used)
    def _():
        y_ref[...] = jnp.zeros_like(y_ref)


def _experts(x1, blk_e, n_used, buf_tok, buf_w, w_gu, b_gu, w_dn, b_dn, layer, tm):
    T, D = x1.shape
    nl, ne, _, d2 = w_gu.shape
    d_ff = d2 // 2
    n_blocks = blk_e.shape[0]
    idx = buf_tok.reshape(n_blocks, 1, tm)
    wcol = buf_w.reshape(n_blocks, tm, 1)
    last = n_blocks - 1
    grid_spec = pltpu.PrefetchScalarGridSpec(
        num_scalar_prefetch=2,
        grid=(n_blocks,),
        in_specs=[pl.BlockSpec((1, 1, tm), lambda i, e, n: (i, 0, 0), memory_space=pltpu.SMEM),
                  pl.BlockSpec((1, 1, tm), lambda i, e, n: (jnp.minimum(i + 1, last), 0, 0),
                               memory_space=pltpu.SMEM),
                  pl.BlockSpec(memory_space=pl.ANY),
                  pl.BlockSpec((1, tm, 1), lambda i, e, n: (i, 0, 0)),
                  pl.BlockSpec((1, 1, D, d2), lambda i, e, n: (layer, e[i], 0, 0)),
                  pl.BlockSpec((1, 1, 1, d2), lambda i, e, n: (layer, e[i], 0, 0)),
                  pl.BlockSpec((1, 1, d_ff, D), lambda i, e, n: (layer, e[i], 0, 0)),
                  pl.BlockSpec((1, 1, 1, D), lambda i, e, n: (layer, e[i], 0, 0))],
        out_specs=pl.BlockSpec((tm, D), lambda i, e, n: (i, 0)),
        scratch_shapes=[pltpu.VMEM((2, tm, D), F32), pltpu.VMEM((D, d2), BF16), pltpu.VMEM((d_ff, D), BF16),
                        pltpu.SemaphoreType.DMA((2,))],
    )
    return pl.pallas_call(
        functools.partial(_expert_kernel, tm=tm, d_ff=d_ff),
        grid_spec=grid_spec,
        out_shape=jax.ShapeDtypeStruct((n_blocks * tm, D), F32),
        compiler_params=_cparams(("arbitrary",), 58 * 1024 * 1024),
        name="moe_experts",
    )(blk_e, n_used, idx, idx, x1, wcol, w_gu, b_gu.reshape(nl, ne, 1, d2), w_dn, b_dn.reshape(nl, ne, 1, D))


def _combine_kernel(idx_ref, idxn_ref, y_hbm, x_ref, lg_ref, lb_ref, o_ref, ybuf, sem, *, tm, alpha):
    i = pl.program_id(0)
    n = pl.num_programs(0)
    slot = i % 2
    rows = TOP_K * tm

    @pl.when(i == 0)
    def _():
        _row_gather_start(y_hbm, ybuf.at[0], sem.at[0], idx_ref, rows)

    @pl.when(i + 1 < n)
    def _():
        _row_gather_start(y_hbm, ybuf.at[1 - slot], sem.at[1 - slot], idxn_ref, rows)

    _row_gather_wait(y_hbm, ybuf.at[slot], sem.at[slot], rows)
    ffn = ybuf[slot, 0:tm]
    for k in range(1, TOP_K):
        ffn = ffn + ybuf[slot, k * tm:(k + 1) * tm]
    o_ref[...] = _layer_norm(alpha * x_ref[...] + ffn, lg_ref[...], lb_ref[...])


def _combine(y_sorted, pos, x1, ln_g, ln_b, alpha):
    T, D = x1.shape
    tm = COMBINE_TM
    nt = T // tm
    idx = pos.reshape(nt, tm, TOP_K).transpose(0, 2, 1).reshape(nt, 1, TOP_K * tm)
    last = nt - 1
    return pl.pallas_call(
        functools.partial(_combine_kernel, tm=tm, alpha=alpha),
        grid=(nt,),
        in_specs=[pl.BlockSpec((1, 1, TOP_K * tm), lambda i: (i, 0, 0), memory_space=pltpu.SMEM),
                  pl.BlockSpec((1, 1, TOP_K * tm), lambda i: (jnp.minimum(i + 1, last), 0, 0),
                               memory_space=pltpu.SMEM),
                  pl.BlockSpec(memory_space=pl.ANY),
                  pl.BlockSpec((tm, D), lambda i: (i, 0)),
                  pl.BlockSpec((1, D), lambda i: (0, 0)),
                  pl.BlockSpec((1, D), lambda i: (0, 0))],
        out_specs=pl.BlockSpec((tm, D), lambda i: (i, 0)),
        out_shape=jax.ShapeDtypeStruct((T, D), F32),
        scratch_shapes=[pltpu.VMEM((2, TOP_K * tm, D), F32), pltpu.SemaphoreType.DMA((2,))],
        compiler_params=_cparams(("arbitrary",)),
        name="moe_combine_ln",
    )(idx, idx, y_sorted, x1, ln_g.reshape(1, D), ln_b.reshape(1, D))


def _route(top_e, top_w, tm):
    T = top_e.shape[0]
    M = T * TOP_K
    i32 = jnp.int32
    flat_e = top_e.reshape(M).astype(i32)
    flat_w = top_w.reshape(M)
    ar = jnp.arange(M, dtype=i32)
    experts = jnp.arange(N_EXPERTS, dtype=i32)
    skey, sorted_w = lax.sort((flat_e * M + ar, flat_w), num_keys=1)
    sorted_e = skey // M
    order = skey - sorted_e * M
    counts = jnp.sum((flat_e[:, None] == experts[None, :]).astype(i32), axis=0)
    start = jnp.cumsum(counts) - counts
    padded = (counts + tm - 1) // tm * tm
    pad_end = jnp.cumsum(padded)
    pad_start = pad_end - padded
    dest = pad_start[sorted_e] + ar - start[sorted_e]
    n_blocks = -(-M // tm) + N_EXPERTS
    blk_first = jnp.arange(n_blocks, dtype=i32) * tm
    blk_e = jnp.minimum(jnp.sum((pad_end[None, :] <= blk_first[:, None]).astype(i32), axis=1), N_EXPERTS - 1)
    within = (blk_first - pad_start[blk_e])[:, None] + jnp.arange(tm, dtype=i32)[None, :]
    valid = within < counts[blk_e][:, None]
    src_row = jnp.clip(start[blk_e][:, None] + within, 0, M - 1)
    buf_tok = jnp.where(valid, order[src_row] // TOP_K, 0).reshape(n_blocks * tm)
    buf_w = jnp.where(valid, sorted_w[src_row], 0.0).reshape(n_blocks * tm)
    _, pos = lax.sort((order, dest), num_keys=1)
    n_used = (pad_end[-1] // tm).astype(i32).reshape(1)
    return blk_e, n_used, buf_tok, buf_w, pos


class _Plan:
    def __init__(self):
        self.src, self.scale, self.const, self.outs = [], [], [], []

    def group(self, width, src_cols=(), at=0, scale=1.0, ones=()):
        src = np.full((width,), -1, np.int64)
        sc = np.zeros((width,), np.float32)
        const = np.zeros((width,), np.float32)
        src[at:at + len(src_cols)] = src_cols
        sc[at:at + len(src_cols)] = scale
        for col, val in ones:
            const[col] = val
        self.src.append(src)
        self.scale.append(sc)
        self.const.append(const)

    def out(self, n_heads, width, dtype):
        self.outs.append((n_heads, width, dtype))


def _cols(start):
    return np.arange(start, start + HEAD_DIM)


def _value_group(plan, src, odd):
    if odd:
        plan.group(LANES, _cols(src), at=HALF, ones=[(0, 1.0)])
    else:
        plan.group(LANES, _cols(src), ones=[(HALF, 1.0)])


def _plans(S):
    qs = HEAD_DIM ** -0.5
    slope = 2.0 ** (-8.0 * np.arange(1, N_NSA_HEADS + 1) / N_NSA_HEADS)
    a = _Plan()
    for g in range(N_NSA_KV):
        for hl in NSA_ROW_HEADS:
            h = g * NSA_REP + hl
            a.group(LANES, _cols(h * HEAD_DIM), scale=qs, ones=[(HALF, slope[h]), (HALF + 1, slope[h])])
    a.out(N_NSA_HEADS, LANES, BF16)
    kv = lambda branch, which, g: SRC_KV + branch * 256 + which * 128 + g * HEAD_DIM
    for which in range(2):
        for g in range(N_NSA_KV):
            a.group(LANES, _cols(kv(0, which, g)))
    a.out(2 * N_NSA_KV, LANES, F32)
    for g in range(N_NSA_KV):
        a.group(2 * LANES, _cols(kv(1, 0, g)))
    a.out(N_NSA_KV, 2 * LANES, BF16)
    for g in range(N_NSA_KV):
        _value_group(a, kv(1, 1, g), False)
        _value_group(a, kv(1, 1, g), True)
    a.out(N_NSA_KV, 2 * LANES, BF16)
    for g in range(N_NSA_KV):
        a.group(LANES, _cols(kv(2, 0, g)))
    a.out(N_NSA_KV, LANES, BF16)
    for g in range(N_NSA_KV):
        _value_group(a, kv(2, 1, g), False)
        _value_group(a, kv(2, 1, g), True)
    a.out(N_NSA_KV, 2 * LANES, BF16)
    a.group(LANES, np.concatenate([np.arange(SRC_GATE, SRC_GATE + 24), np.arange(SRC_LOGF, SRC_LOGF + 4)]))
    a.out(1, LANES, F32)

    n_a = sum(len(s) for s in a.src)
    extra = np.zeros((S, n_a), np.float32)
    pos = np.arange(S)
    pos_hi, pos_lo = pos // SEL_LEN * SEL_LEN, pos % SEL_LEN
    off_ksel = N_NSA_HEADS * LANES + 2 * N_NSA_KV * LANES
    off_kwin = off_ksel + N_NSA_KV * 2 * LANES + N_NSA_KV * 2 * LANES
    nb = S // SEL_LEN
    for g in range(N_NSA_KV):
        o = off_ksel + g * 2 * LANES
        extra[:, o + HALF], extra[:, o + HALF + 1] = pos_hi, pos_lo
        extra[pos, o + LANES + pos // SEL_LEN] = 1.0
        o = off_kwin + g * LANES
        extra[:, o + HALF], extra[:, o + HALF + 1] = pos_hi, pos_lo
    assert nb <= LANES

    b = _Plan()
    fox = lambda which, h: SRC_FOX + which * 256 + h * HEAD_DIM
    sb = lambda which, h: SRC_SB + which * 256 + h * HEAD_DIM
    for h in range(N_FOX_HEADS):
        b.group(LANES, _cols(fox(0, h)), scale=qs, ones=[(HALF, 1.0), (HALF + 1, 1.0), (HALF + 2, 1.0)])
    b.out(N_FOX_HEADS, LANES, BF16)
    for h in range(N_FOX_HEADS):
        b.group(LANES, _cols(fox(1, h)))
    b.out(N_FOX_HEADS, LANES, BF16)
    for h in range(N_FOX_HEADS):
        _value_group(b, fox(2, h), h % 2 == 1)
    b.out(N_FOX_HEADS, LANES, BF16)
    for h in range(N_SB_HEADS):
        b.group(LANES, _cols(sb(0, h)), scale=qs)
    b.out(N_SB_HEADS, LANES, BF16)
    for h in range(N_SB_HEADS):
        b.group(LANES, _cols(sb(1, h)))
    b.out(N_SB_HEADS, LANES, BF16)
    for h in range(N_SB_HEADS):
        b.group(LANES, _cols(sb(2, h)), at=HALF if h % 2 else 0)
    b.out(N_SB_HEADS, LANES, BF16)
    return a, jnp.asarray(extra, dtype=BF16), b


def _constants(S):
    plan_a, extra_a, plan_b = _plans(S)
    nc = S // CMP_STRIDE
    nb = S // SEL_LEN
    n_cmp = (S - CMP_LEN) // CMP_STRIDE + 1
    cmp_idx = np.arange(nc)[:, None] * CMP_STRIDE + np.arange(CMP_LEN)[None, :]
    ovl = (cmp_idx[:, :, None] // SEL_LEN == np.arange(nb)[None, None, :]).astype(np.float32).mean(axis=1)
    ovl[n_cmp:] = 0.0
    ovl = np.pad(ovl, ((0, 0), (0, LANES - nb)))
    cmp_end = np.arange(nc) * CMP_STRIDE + CMP_LEN - 1
    kcmp_aug = np.zeros((nc, LANES), np.float32)
    kcmp_aug[:, HALF] = cmp_end // SEL_LEN * SEL_LEN
    kcmp_aug[:, HALF + 1] = cmp_end % SEL_LEN
    return dict(a=plan_a, a_outs=tuple(plan_a.outs), extra_a=extra_a, b=plan_b, b_outs=tuple(plan_b.outs),
                ovl_t=jnp.asarray(ovl.T, dtype=BF16), kcmp_aug=jnp.asarray(kcmp_aug))


def _plan_weights(w_in, b_in, plan):
    src = np.concatenate(plan.src)
    scale = np.concatenate(plan.scale)
    const = jnp.asarray(np.concatenate(plan.const))
    wb = jnp.concatenate([w_in, b_in[None, :]], axis=0)
    pieces, c = [], 0
    while c < len(src):
        e = c + 1
        if src[c] < 0:
            while e < len(src) and src[e] < 0:
                e += 1
            pieces.append(jnp.zeros((wb.shape[0], e - c), F32))
        else:
            while e < len(src) and src[e] == src[e - 1] + 1 and scale[e] == scale[c]:
                e += 1
            pieces.append(wb[:, src[c]:src[c] + e - c] * float(scale[c]))
        c = e
    wb_p = jnp.concatenate(pieces, axis=1)
    return wb_p[:-1].astype(BF16), (wb_p[-1] + const).reshape(1, -1)


def _layer(x, cs, w_in, b_in, pos_k, pos_v, w1k, w2k, w1v, w2v, g_out, w_out, ln1_g, ln1_b,
           router_w, router_b, w_gu, b_gu, w_dn, b_dn, ln2_g, ln2_b, alpha, layer):
    B, S, D = x.shape
    T = B * S
    x2d = x.reshape(T, D)
    wa, ba = _plan_weights(w_in, b_in, cs["a"])
    wb, bb = _plan_weights(w_in, b_in, cs["b"])
    q2, kvc, ksel, vsel, kwin, vwin, small = _proj(x2d, wa, ba, cs["extra_a"], cs["a_outs"], B, S)
    fq, fk, fv, sq, sk, sv = _proj(x2d, wb, bb, None, cs["b_outs"], B, S)
    small = small.reshape(B, S, LANES)

    kc = kvc[:, 0:N_NSA_KV, :, 0:HEAD_DIM]
    vc = kvc[:, N_NSA_KV:, :, 0:HEAD_DIM]
    k_cmp, v_cmp = _compress(kc, vc, pos_k, pos_v, w1k, w2k, w1v, w2v, cs["kcmp_aug"])
    o_c, msel = _nsa_cmp(q2, k_cmp, v_cmp, cs["ovl_t"], small)
    o_s = _nsa_sel(q2, msel, ksel, vsel, small, min(SEL_KB, S))
    o_w = _nsa_win(q2, kwin, vwin, small)

    c = _logf_cumsum(small[..., 24:24 + N_FOX_HEADS].transpose(0, 2, 1))
    c_hi = _trunc_bf16(c)
    c_mid = _trunc_bf16(c - c_hi)
    c_lo = c - c_hi - c_mid
    c_aug = jnp.pad(-jnp.stack([c_hi, c_mid, c_lo], axis=-1), ((0, 0), (0, 0), (0, 0), (HALF, HALF - 3)))
    o_fox = _fox(fq, fk + c_aug.astype(BF16), fv)
    o_sb = _sb(sq, sk, sv)

    flat = lambda o: o.reshape(T, o.shape[-1])
    x1, te, tw = _post(flat(o_c), flat(o_s), flat(o_w), flat(o_fox), flat(o_sb), x2d, g_out, w_out.astype(BF16),
                       ln1_g, ln1_b, router_w, router_b, alpha)
    blk_e, n_used, buf_tok, buf_w, pos = _route(te[:, :TOP_K], tw[:, :TOP_K], MOE_TM)
    y_sorted = _experts(x1, blk_e, n_used, buf_tok, buf_w, w_gu, b_gu, w_dn, b_dn, layer, MOE_TM)
    out = _combine(y_sorted, pos, x1, ln2_g, ln2_b, alpha)
    return out.reshape(B, S, D)


def kernel(x, w_in, b_in, cmp_pos_k, cmp_pos_v, cmp_w1_k, cmp_w2_k, cmp_w1_v, cmp_w2_v, g_out, w_out,
           ln1_g, ln1_b, router_w, router_b, w_gate_up, b_gate_up, w_down, b_down, ln2_g, ln2_b):
    depth = w_in.shape[0]
    alpha = (2 * depth) ** 0.25
    consts = _constants(x.shape[1])
    for l in range(depth):
        x = _layer(x, consts, w_in[l], b_in[l], cmp_pos_k[l], cmp_pos_v[l], cmp_w1_k[l], cmp_w2_k[l],
                   cmp_w1_v[l], cmp_w2_v[l], g_out[l], w_out[l], ln1_g[l], ln1_b[l], router_w[l], router_b[l],
                   w_gate_up, b_gate_up, w_down, b_down, ln2_g[l], ln2_b[l], alpha, l)
    return x
```

```python
import functools

import jax
import jax.numpy as jnp
import numpy as np
from jax import lax
from jax.experimental import pallas as pl
from jax.experimental.pallas import tpu as pltpu

F32 = jnp.float32
BF16 = jnp.bfloat16

HEAD_DIM = 64
N_NSA_HEADS = 8
N_NSA_KV = 2
NSA_REP = 4
N_FOX_HEADS = 4
N_SB_HEADS = 4
D_NSA = 512
D_FOX = 256
D_SB = 256
CMP_LEN = 32
CMP_STRIDE = 16
CMP_HIDDEN = 128
SEL_LEN = 64
SEL_TOPK = 16
WINDOW = 512
Q_BLOCK = 128
N_EXPERTS = 32
TOP_K = 4
SWIGLU_LIMIT = 7.0
SWIGLU_ALPHA = 1.702
LN_EPS = 1e-5
RMS_EPS = 1e-6
NEG = -1e30
BIG = 3e38
SB_CUTOFF = -90.0
SKIP_T = 100.0

LANES = 128
HALF = LANES // 2
NSA_ROW_HEADS = (0, 2, 1, 3)

SRC_KV = 512
SRC_GATE = SRC_KV + 768
SRC_FOX = SRC_GATE + 24
SRC_LOGF = SRC_FOX + 768
SRC_SB = SRC_LOGF + 4

VMEM_LIMIT = 52 * 1024 * 1024

PROJ_TM = 512
CMP_SUB = 2
SEL_KB = 256
FOX_TQ = 512
FOX_KB = 256
SB_TQ = 128
POST_TM = 256
MOE_TM = 256
COMBINE_TM = 128


def _cparams(sem, vmem=VMEM_LIMIT):
    return pltpu.CompilerParams(dimension_semantics=sem, vmem_limit_bytes=vmem)


def _dot(a, b):
    return jnp.dot(a, b, preferred_element_type=F32)


def _dot_nt(a, b):
    return lax.dot_general(a, b, (((1,), (1,)), ((), ())), preferred_element_type=F32)


def _split_bf16(x):
    hi = x.astype(BF16)
    lo = (x - hi.astype(F32)).astype(BF16)
    return hi, lo


def _trunc_bf16(x):
    bits = lax.bitcast_convert_type(x, jnp.uint32) & jnp.uint32(0xFFFF0000)
    return lax.bitcast_convert_type(bits, F32)


def _sigmoid(x):
    return 1.0 / (1.0 + jnp.exp(-x))


def _lane_column(tile, idx):
    lane = lax.broadcasted_iota(jnp.int32, tile.shape, 1)
    return jnp.sum(jnp.where(lane == idx, tile, 0.0), axis=1, keepdims=True)


def _low_half(shape):
    return lax.broadcasted_iota(jnp.int32, shape, 1) < HALF


def _proj_kernel(*refs, outs, has_extra):
    x_ref, w_ref, b_ref = refs[:3]
    o_refs = refs[3 + has_extra:]
    y = _dot(x_ref[...].astype(BF16), w_ref[...]) + b_ref[...]
    if has_extra:
        y = y + refs[3][...].astype(F32)
    off = 0
    for o_ref, (nh, width, _) in zip(o_refs, outs):
        for h in range(nh):
            o_ref[0, h] = y[:, off:off + width].astype(o_ref.dtype)
            off += width


def _proj(x2d, w_bf, b_row, extra, outs, B, S):
    T, D = x2d.shape
    N = w_bf.shape[1]
    tm = min(PROJ_TM, S)
    spb = S // tm
    in_specs = [pl.BlockSpec((tm, D), lambda i: (i, 0)),
                pl.BlockSpec((D, N), lambda i: (0, 0)),
                pl.BlockSpec((1, N), lambda i: (0, 0))]
    args = [x2d, w_bf, b_row]
    if extra is not None:
        in_specs.append(pl.BlockSpec((tm, N), lambda i: (i % spb, 0)))
        args.append(extra)
    return pl.pallas_call(
        functools.partial(_proj_kernel, outs=outs, has_extra=extra is not None),
        grid=(T // tm,),
        in_specs=in_specs,
        out_specs=[pl.BlockSpec((1, nh, tm, w), lambda i: (i // spb, 0, i % spb, 0)) for nh, w, _ in outs],
        out_shape=[jax.ShapeDtypeStruct((B, nh, S, w), dt) for nh, w, dt in outs],
        compiler_params=_cparams(("parallel",)),
        name="proj",
    )(*args)


def _gelu_tanh(x):
    return 0.5 * x * (1.0 + jnp.tanh(0.7978845608028654 * (x + 0.044715 * (x * x * x))))


def _compress_kernel(k_ref, v_ref, pk_ref, pv_ref, w1k_ref, w2k_ref, w1v_ref, w2v_ref, ka_ref, ok_ref, ov_ref):
    nc = k_ref.shape[2]
    half = CMP_STRIDE * HEAD_DIM

    def one(r_ref, p_ref, w1_ref, w2_ref):
        r = r_ref[0, 0]
        a = (r + p_ref[0:1, :]).astype(BF16)
        b = (r + p_ref[1:2, :]).astype(BF16)
        ha = _dot(a, w1_ref[0:half, :])
        hb = _dot(b, w1_ref[half:2 * half, :])
        hid = _gelu_tanh(ha + pltpu.roll(hb, nc - 1, 0))
        return _dot(hid.astype(BF16), w2_ref[...])

    ok_ref[0, 0] = (one(k_ref, pk_ref, w1k_ref, w2k_ref) + ka_ref[...]).astype(ok_ref.dtype)
    ov_ref[0, 0] = one(v_ref, pv_ref, w1v_ref, w2v_ref).astype(ov_ref.dtype)


def _compress(kc, vc, pos_k, pos_v, w1k, w2k, w1v, w2v, kcmp_aug):
    B, G, S, dh = kc.shape
    nc = S // CMP_STRIDE
    width = CMP_STRIDE * dh
    kr = kc.reshape(B, G, nc, width)
    vr = vc.reshape(B, G, nc, width)
    pk = pos_k.reshape(2, width)
    pv = pos_v.reshape(2, width)
    w2k_p = jnp.pad(w2k, ((0, 0), (0, LANES - dh))).astype(BF16)
    w2v_p = jnp.concatenate([jnp.pad(w2v, ((0, 0), (0, LANES - dh))), jnp.pad(w2v, ((0, 0), (LANES - dh, 0)))],
                            axis=1).astype(BF16)
    kv_spec = pl.BlockSpec((1, 1, nc, width), lambda b, g: (b, g, 0, 0))
    full = lambda shape: pl.BlockSpec(shape, lambda b, g: tuple(0 for _ in shape))
    return pl.pallas_call(
        _compress_kernel,
        grid=(B, G),
        in_specs=[kv_spec, kv_spec, full((2, width)), full((2, width)),
                  full((2 * width, CMP_HIDDEN)), full((CMP_HIDDEN, LANES)),
                  full((2 * width, CMP_HIDDEN)), full((CMP_HIDDEN, 2 * LANES)), full((nc, LANES))],
        out_specs=[pl.BlockSpec((1, 1, nc, LANES), lambda b, g: (b, g, 0, 0)),
                   pl.BlockSpec((1, 1, nc, 2 * LANES), lambda b, g: (b, g, 0, 0))],
        out_shape=[jax.ShapeDtypeStruct((B, G, nc, LANES), BF16),
                   jax.ShapeDtypeStruct((B, G, nc, 2 * LANES), BF16)],
        compiler_params=_cparams(("parallel", "parallel")),
        name="nsa_compress",
    )(kr, vr, pk, pv, w1k.astype(BF16), w2k_p, w1v.astype(BF16), w2v_p, kcmp_aug)


def _nsa_gates(small, branch, g):
    return [_sigmoid(_lane_column(small, branch * N_NSA_HEADS + g * NSA_REP + h)) for h in NSA_ROW_HEADS]


def _nsa_cmp_kernel(q_ref, kc_ref, vc_ref, ovl_ref, sm_ref, o_ref, sel_ref, *, n_sel, n_sub):
    g = pl.program_id(1)
    i = pl.program_id(2)
    results = [_nsa_cmp_block(q_ref, kc_ref, vc_ref, ovl_ref, sm_ref, g, i * n_sub + u, u, n_sel)
               for u in range(n_sub)]
    for u, (o, msel) in enumerate(results):
        o_ref[0, u * Q_BLOCK:(u + 1) * Q_BLOCK, :] = o
        sel_ref[0, 0, u] = msel


def _nsa_cmp_block(q_ref, kc_ref, vc_ref, ovl_ref, sm_ref, g, qb, u, n_sel):
    q0 = qb * Q_BLOCK
    rows = slice(u * Q_BLOCK, (u + 1) * Q_BLOCK)
    nc = kc_ref.shape[2]
    nb = ovl_ref.shape[0]
    kc = kc_ref[0, 0]
    vc = vc_ref[0, 0]
    gates = _nsa_gates(sm_ref[0, rows, :], 0, g)
    t = q0 + lax.broadcasted_iota(jnp.int32, (Q_BLOCK, nc), 0)
    cmp_end = lax.broadcasted_iota(jnp.int32, (Q_BLOCK, nc), 1) * CMP_STRIDE + (CMP_LEN - 1)
    mask = cmp_end <= t
    psum = jnp.zeros((Q_BLOCK, nc), F32)
    outs = []
    for r in range(NSA_REP):
        s = jnp.where(mask, _dot_nt(q_ref[0, r, rows, :], kc), NEG)
        m = jnp.max(s, axis=1, keepdims=True)
        p = jnp.where(mask, jnp.exp(s - m), 0.0)
        l = jnp.sum(p, axis=1, keepdims=True)
        pn = p / jnp.where(l > 0.0, l, 1.0)
        v_half = vc[:, 0:LANES] if r < 2 else vc[:, LANES:2 * LANES]
        outs.append(_dot(pn.astype(BF16), v_half) * gates[r])
        psum = psum + pn
    o = jnp.concatenate([outs[0] + outs[2], outs[1] + outs[3]], axis=1)
    hi, lo = _split_bf16(psum)
    ovl = ovl_ref[...]
    imp = _dot_nt(ovl, hi) + _dot_nt(ovl, lo)
    jf = lax.broadcasted_iota(jnp.int32, (nb, Q_BLOCK), 0)
    tq = q0 + lax.broadcasted_iota(jnp.int32, (nb, Q_BLOCK), 1)
    cur = tq >> 6
    valid = jf <= cur
    forced = valid & ((jf == 0) | (jf >= cur - 1))
    v = jnp.where(valid & ~forced, imp, -BIG)
    jff = jf.astype(F32)
    picked = jnp.where(forced, 1.0, 0.0)
    for _ in range(max(n_sel - 3, 0)):
        mx = jnp.max(v, axis=0, keepdims=True)
        idx = jnp.min(jnp.where(v == mx, jff, float(nb)), axis=0, keepdims=True)
        hit = jff == idx
        picked = jnp.where(hit, 1.0, picked)
        v = jnp.where(hit, -BIG, v)
    msel = jnp.where(valid & (picked > 0.5), 0.0, NEG)
    return o, msel.T.astype(BF16)


def _nsa_cmp(q2, kcmp, vcmp, ovl_t, small):
    B, H, S, qw = q2.shape
    G = N_NSA_KV
    nq = S // Q_BLOCK
    nc = kcmp.shape[2]
    nb = ovl_t.shape[0]
    n_sel = min(SEL_TOPK, S // SEL_LEN)
    n_sub = CMP_SUB
    return pl.pallas_call(
        functools.partial(_nsa_cmp_kernel, n_sel=n_sel, n_sub=n_sub),
        grid=(B, G, nq // n_sub),
        in_specs=[pl.BlockSpec((1, NSA_REP, n_sub * Q_BLOCK, qw), lambda b, g, i: (b, g, i, 0)),
                  pl.BlockSpec((1, 1, nc, LANES), lambda b, g, i: (b, g, 0, 0)),
                  pl.BlockSpec((1, 1, nc, 2 * LANES), lambda b, g, i: (b, g, 0, 0)),
                  pl.BlockSpec((nb, nc), lambda b, g, i: (0, 0)),
                  pl.BlockSpec((1, n_sub * Q_BLOCK, LANES), lambda b, g, i: (b, i, 0))],
        out_specs=[pl.BlockSpec((1, n_sub * Q_BLOCK, 2 * LANES), lambda b, g, i: (b, i, g)),
                   pl.BlockSpec((1, 1, n_sub, Q_BLOCK, nb), lambda b, g, i: (b, g, i, 0, 0))],
        out_shape=[jax.ShapeDtypeStruct((B, S, D_NSA), F32),
                   jax.ShapeDtypeStruct((B, G, nq, Q_BLOCK, nb), BF16)],
        compiler_params=_cparams(("parallel", "parallel", "parallel")),
        name="nsa_cmp_select",
    )(q2, kcmp, vcmp, ovl_t, small)


def _flash_init(rows):
    return jnp.full((rows, 1), NEG, F32), jnp.zeros((rows, LANES), F32)


def _flash_update(carry, s, v_tile):
    m, acc = carry
    half = s.shape[0] // 2
    m_new = jnp.maximum(m, jnp.max(s, axis=1, keepdims=True))
    p = jnp.exp(s - m_new).astype(BF16)
    pv = jnp.concatenate([_dot(p[:half], v_tile[:, 0:LANES]), _dot(p[half:], v_tile[:, LANES:2 * LANES])], axis=0)
    return m_new, jnp.exp(m - m_new) * acc + pv


def _pair_out(acc_even, acc_odd):
    even = acc_even / acc_even[:, HALF:HALF + 1]
    odd = acc_odd / acc_odd[:, 0:1]
    return jnp.where(_low_half(acc_even.shape), even, odd)


def _nsa_out(acc, gates):
    c = [acc[r * Q_BLOCK:(r + 1) * Q_BLOCK] for r in range(NSA_REP)]
    low = _low_half((Q_BLOCK, LANES))
    pair_a = _pair_out(c[0], c[2]) * jnp.where(low, gates[0], gates[2])
    pair_b = _pair_out(c[1], c[3]) * jnp.where(low, gates[1], gates[3])
    return jnp.concatenate([pair_a, pair_b], axis=1)


def _stacked_rows(rows, kb):
    row = lax.broadcasted_iota(jnp.int32, (rows, kb), 0) & (Q_BLOCK - 1)
    col = lax.broadcasted_iota(jnp.int32, (rows, kb), 1)
    return row, col


def _nsa_sel_kernel(kmax_ref, q_ref, msel_ref, k_ref, v_ref, sm_ref, o_ref, qa_ref, sa_ref, sb_ref, *, kb):
    g = pl.program_id(1)
    i = pl.program_id(2)
    q0 = i * Q_BLOCK
    rows = NSA_REP * Q_BLOCK
    msel = msel_ref[0, 0, 0]
    for r in range(NSA_REP):
        qa_ref[r * Q_BLOCK:(r + 1) * Q_BLOCK, :] = jnp.concatenate([q_ref[0, r], msel], axis=1)
    qa = qa_ref[...]
    last = (q0 + Q_BLOCK - 1) // kb

    def logits(kt):
        k0 = pl.multiple_of(kt * kb, kb)
        return _dot_nt(qa, k_ref[0, 0, pl.ds(k0, kb), :])

    def v_tile(kt):
        return v_ref[0, 0, pl.ds(pl.multiple_of(kt * kb, kb), kb), :]

    reach = _query_norm(qa) * kmax_ref[pl.program_id(0) * N_NSA_KV + g]
    slope = qa[:, HEAD_DIM:HEAD_DIM + 1].astype(F32)
    row, col = _stacked_rows(rows, kb)
    causal = (last * kb + col) <= (q0 + row)
    carry = _flash_update(_flash_init(rows), jnp.where(causal, logits(last), NEG), v_tile(last))
    odd = last & 1
    single = jnp.maximum(last - 1, 0)
    carry = lax.cond(odd == 1, lambda c: _flash_update(c, logits(single), v_tile(single)), lambda c: c, carry)
    top0 = last - 1 - odd
    sa_ref[...] = logits(jnp.maximum(top0, 0))

    def cond(state):
        top, carry = state
        last_pos = (jnp.maximum(top, 0) * kb + kb - 1).astype(F32)
        return (top >= 0) & (jnp.max(reach + slope * last_pos - carry[0]) > -SKIP_T)

    def body(state):
        top, carry = state
        sb_ref[...] = logits(top - 1)
        carry = _flash_update(carry, sa_ref[...], v_tile(top))
        sa_ref[...] = logits(jnp.maximum(top - 2, 0))
        return top - 2, _flash_update(carry, sb_ref[...], v_tile(top - 1))

    _, (_, acc) = lax.while_loop(cond, body, (top0, carry))
    o_ref[0] = _nsa_out(acc, _nsa_gates(sm_ref[0], 1, g))


def _nsa_sel(q2, msel, k2, v2, small, kmax, kb):
    B, H, S, qw = q2.shape
    G = N_NSA_KV
    nq = S // Q_BLOCK
    nb = msel.shape[-1]
    grid_spec = pltpu.PrefetchScalarGridSpec(
        num_scalar_prefetch=1,
        grid=(B, G, nq),
        in_specs=[pl.BlockSpec((1, NSA_REP, Q_BLOCK, qw), lambda b, g, i, km: (b, g, i, 0)),
                  pl.BlockSpec((1, 1, 1, Q_BLOCK, nb), lambda b, g, i, km: (b, g, i, 0, 0)),
                  pl.BlockSpec((1, 1, S, k2.shape[-1]), lambda b, g, i, km: (b, g, 0, 0)),
                  pl.BlockSpec((1, 1, S, 2 * LANES), lambda b, g, i, km: (b, g, 0, 0)),
                  pl.BlockSpec((1, Q_BLOCK, LANES), lambda b, g, i, km: (b, i, 0))],
        out_specs=pl.BlockSpec((1, Q_BLOCK, 2 * LANES), lambda b, g, i, km: (b, i, g)),
        scratch_shapes=[pltpu.VMEM((NSA_REP * Q_BLOCK, qw + nb), BF16),
                        pltpu.VMEM((NSA_REP * Q_BLOCK, kb), F32),
                        pltpu.VMEM((NSA_REP * Q_BLOCK, kb), F32)],
    )
    return pl.pallas_call(
        functools.partial(_nsa_sel_kernel, kb=kb),
        grid_spec=grid_spec,
        out_shape=jax.ShapeDtypeStruct((B, S, D_NSA), F32),
        compiler_params=_cparams(("parallel", "parallel", "parallel")),
        name="nsa_selected",
    )(kmax, q2, msel, k2, v2, small)


def _nsa_win_kernel(q_ref, k_ref, v_ref, sm_ref, o_ref):
    g = pl.program_id(1)
    i = pl.program_id(2)
    kb = Q_BLOCK
    rows = NSA_REP * Q_BLOCK
    qa = q_ref[0].reshape(rows, q_ref.shape[-1])
    n_back = WINDOW // kb
    row, col = _stacked_rows(rows, kb)
    carry = _flash_init(rows)
    for d in range(n_back + 1):
        kt = i - n_back + d
        k0 = pl.multiple_of(jnp.maximum(kt, 0) * kb, kb)
        s = _dot_nt(qa, k_ref[0, 0, pl.ds(k0, kb), :])
        if d == 0:
            s = jnp.where((col > row) & (kt >= 0), s, NEG)
        elif d == n_back:
            s = jnp.where(col <= row, s, NEG)
        else:
            s = jnp.where(kt >= 0, s, NEG)
        carry = _flash_update(carry, s, v_ref[0, 0, pl.ds(k0, kb), :])
    o_ref[0] = _nsa_out(carry[1], _nsa_gates(sm_ref[0], 2, g))


def _nsa_win(q2, k2, v2, small):
    B, H, S, qw = q2.shape
    G = N_NSA_KV
    nq = S // Q_BLOCK
    return pl.pallas_call(
        _nsa_win_kernel,
        grid=(B, G, nq),
        in_specs=[pl.BlockSpec((1, NSA_REP, Q_BLOCK, qw), lambda b, g, i: (b, g, i, 0)),
                  pl.BlockSpec((1, 1, S, qw), lambda b, g, i: (b, g, 0, 0)),
                  pl.BlockSpec((1, 1, S, 2 * LANES), lambda b, g, i: (b, g, 0, 0)),
                  pl.BlockSpec((1, Q_BLOCK, LANES), lambda b, g, i: (b, i, 0))],
        out_specs=pl.BlockSpec((1, Q_BLOCK, 2 * LANES), lambda b, g, i: (b, i, g)),
        out_shape=jax.ShapeDtypeStruct((B, S, D_NSA), F32),
        compiler_params=_cparams(("parallel", "parallel", "parallel")),
        name="nsa_window",
    )(q2, k2, v2, small)


def _fox_update(carry, s, v_tile):
    m, acc = carry
    m_new = jnp.maximum(m, jnp.max(s, axis=1, keepdims=True))
    p = jnp.exp(s - m_new).astype(BF16)
    return m_new, jnp.exp(m - m_new) * acc + _dot(p, v_tile)


def _query_norm(q):
    qf = q.astype(F32)
    lane = lax.broadcasted_iota(jnp.int32, qf.shape, 1)
    return jnp.sqrt(jnp.sum(jnp.where(lane < HEAD_DIM, qf * qf, 0.0), axis=1, keepdims=True))


def _fox_kernel(kmax_ref, tb_ref, q_ref, k_ref, v_ref, o_ref, *s_refs, tq, kb, n_heads, n_tiles):
    b = pl.program_id(0)
    hp = pl.program_id(1)
    i = pl.program_id(2)
    heads = [b * n_heads + hp * 2 + h for h in range(2)]

    def logits(h, kt):
        k0 = pl.multiple_of(kt * kb, kb)
        return _dot_nt(q_ref[0, h], k_ref[0, h, pl.ds(k0, kb), :])

    def v_tile(h, kt):
        return v_ref[0, h, pl.ds(pl.multiple_of(kt * kb, kb), kb), :]

    row = lax.broadcasted_iota(jnp.int32, (tq, kb), 0)
    col = lax.broadcasted_iota(jnp.int32, (tq, kb), 1)
    reach, carries = [], []
    for h in range(2):
        reach.append(_query_norm(q_ref[0, h]) * kmax_ref[heads[h]])
        c = _fox_update(_flash_init(tq), jnp.where(col <= row, logits(h, 2 * i), NEG), v_tile(h, 2 * i))
        carries.append(_fox_update(c, jnp.where(kb + col <= row, logits(h, 2 * i + 1), NEG), v_tile(h, 2 * i + 1)))
        s_refs[2 * h][...] = logits(h, jnp.maximum(2 * i - 1, 0))

    def cond(state):
        top, carries = state
        t = jnp.maximum(top, 0)
        alive = jnp.max(reach[0] + tb_ref[heads[0] * n_tiles + t] - carries[0][0])
        alive = jnp.maximum(alive, jnp.max(reach[1] + tb_ref[heads[1] * n_tiles + t] - carries[1][0]))
        return (top >= 0) & (alive > -SKIP_T)

    def body(state):
        top, carries = state
        out = []
        for h in range(2):
            sa_ref, sb_ref = s_refs[2 * h], s_refs[2 * h + 1]
            sb_ref[...] = logits(h, top - 1)
            c = _fox_update(carries[h], sa_ref[...], v_tile(h, top))
            sa_ref[...] = logits(h, jnp.maximum(top - 2, 0))
            out.append(_fox_update(c, sb_ref[...], v_tile(h, top - 1)))
        return top - 2, tuple(out)

    _, carries = lax.while_loop(cond, body, (2 * i - 1, tuple(carries)))
    o_ref[0] = _pair_out(carries[0][1], carries[1][1])


def _fox(q2, k2, v2, kmax, tile_bias):
    B, H, S, qw = q2.shape
    tq, kb = min(FOX_TQ, S), min(FOX_KB, S)
    assert tq == 2 * kb
    grid_spec = pltpu.PrefetchScalarGridSpec(
        num_scalar_prefetch=2,
        grid=(B, H // 2, S // tq),
        in_specs=[pl.BlockSpec((1, 2, tq, qw), lambda b, h, i, km, tb: (b, h, i, 0)),
                  pl.BlockSpec((1, 2, S, qw), lambda b, h, i, km, tb: (b, h, 0, 0)),
                  pl.BlockSpec((1, 2, S, LANES), lambda b, h, i, km, tb: (b, h, 0, 0))],
        out_specs=pl.BlockSpec((1, tq, LANES), lambda b, h, i, km, tb: (b, i, h)),
        scratch_shapes=[pltpu.VMEM((tq, kb), F32)] * 4,
    )
    return pl.pallas_call(
        functools.partial(_fox_kernel, tq=tq, kb=kb, n_heads=H, n_tiles=S // kb),
        grid_spec=grid_spec,
        out_shape=jax.ShapeDtypeStruct((B, S, D_FOX), F32),
        compiler_params=_cparams(("parallel", "parallel", "parallel")),
        name="fox_attention",
    )(kmax, tile_bias, q2, k2, v2)


def _logf_cumsum_kernel(x_ref, c_ref):
    x = x_ref[0]
    nr, nl = x.shape
    log_f = jnp.minimum(x, 0.0) - jnp.log1p(jnp.exp(-jnp.abs(x)))
    hp = lax.Precision.HIGHEST
    incl = (lax.broadcasted_iota(jnp.int32, (nl, nl), 0) <= lax.broadcasted_iota(jnp.int32, (nl, nl), 1))
    within = jnp.dot(log_f, incl.astype(F32), preferred_element_type=F32, precision=hp)
    totals = jnp.broadcast_to(within[:, nl - 1:nl], (nr, nl))
    before = (lax.broadcasted_iota(jnp.int32, (nr, nr), 1) < lax.broadcasted_iota(jnp.int32, (nr, nr), 0))
    c_ref[0] = within + jnp.dot(before.astype(F32), totals, preferred_element_type=F32, precision=hp)


def _logf_cumsum(logits):
    B, H, S = logits.shape
    x = logits.reshape(B * H, S // LANES, LANES)
    c = pl.pallas_call(
        _logf_cumsum_kernel,
        grid=(B * H,),
        in_specs=[pl.BlockSpec((1, S // LANES, LANES), lambda i: (i, 0, 0))],
        out_specs=pl.BlockSpec((1, S // LANES, LANES), lambda i: (i, 0, 0)),
        out_shape=jax.ShapeDtypeStruct(x.shape, F32),
        compiler_params=_cparams(("parallel",)),
        name="logf_cumsum",
    )(x)
    return c.reshape(B, H, S)


def _sb_kernel(q_ref, k_ref, v_ref, o_ref, *, tq, nh):
    i = pl.program_id(1)
    q0 = i * tq
    trow = q0 + lax.broadcasted_iota(jnp.int32, (tq, tq), 0)
    col = lax.broadcasted_iota(jnp.int32, (tq, tq), 1)
    rr = lax.broadcasted_iota(jnp.int32, (tq, tq), 0)
    upper = (rr > col).astype(BF16)

    def cond(state):
        kt, carries, _ = state
        alive = jnp.max(carries[0])
        for h in range(1, nh):
            alive = jnp.maximum(alive, jnp.max(carries[h]))
        return (kt >= 0) & (alive > SB_CUTOFF)

    def body(state):
        kt, carries, accs = state
        k0 = pl.multiple_of(kt * tq, tq)
        strict = (k0 + col) < trow
        new_c, new_a = [], []
        for h in range(nh):
            z = _dot_nt(q_ref[0, h], k_ref[0, h, pl.ds(k0, tq), :])
            log_beta = jnp.minimum(z, 0.0) - jnp.log1p(jnp.exp(-jnp.abs(z)))
            log_keep = jnp.where(strict, log_beta - z, 0.0)
            hi, lo = _split_bf16(log_keep)
            later = _dot(hi, upper) + _dot(lo, upper)
            a = jnp.where(strict, jnp.exp(log_beta + later + carries[h]), 0.0)
            new_a.append(accs[h] + _dot(a.astype(BF16), v_ref[0, h, pl.ds(k0, tq), :]))
            new_c.append(carries[h] + jnp.sum(log_keep, axis=1, keepdims=True))
        return kt - 1, tuple(new_c), tuple(new_a)

    state = (i, tuple(jnp.zeros((tq, 1), F32) for _ in range(nh)),
             tuple(jnp.zeros((tq, LANES), F32) for _ in range(nh)))
    _, _, accs = lax.while_loop(cond, body, state)
    o_ref[0] = jnp.concatenate([accs[2 * j] + accs[2 * j + 1] for j in range(nh // 2)], axis=1)


def _sb(q2, k2, v2):
    B, H, S, w = q2.shape
    tq = min(SB_TQ, S)
    return pl.pallas_call(
        functools.partial(_sb_kernel, tq=tq, nh=H),
        grid=(B, S // tq),
        in_specs=[pl.BlockSpec((1, H, tq, w), lambda b, i: (b, 0, i, 0)),
                  pl.BlockSpec((1, H, S, w), lambda b, i: (b, 0, 0, 0)),
                  pl.BlockSpec((1, H, S, w), lambda b, i: (b, 0, 0, 0))],
        out_specs=pl.BlockSpec((1, tq, D_SB), lambda b, i: (b, i, 0)),
        out_shape=jax.ShapeDtypeStruct((B, S, D_SB), F32),
        compiler_params=_cparams(("parallel", "parallel")),
        name="sb_attention",
    )(q2, k2, v2)


def _layer_norm(y, g, b):
    mu = jnp.mean(y, axis=1, keepdims=True)
    d = y - mu
    var = jnp.mean(d * d, axis=1, keepdims=True)
    return d * lax.rsqrt(var + LN_EPS) * g + b


def _rms(o, g):
    return o * lax.rsqrt(jnp.mean(o * o, axis=1, keepdims=True) + RMS_EPS) * g


def _post_kernel(oc_ref, osel_ref, ow_ref, of_ref, os_ref, x_ref, go_ref, wo_ref, lg_ref, lb_ref, rw_ref, rb_ref,
                 x1_ref, te_ref, tw_ref, *, alpha):
    o_nsa = oc_ref[...] + osel_ref[...] + ow_ref[...]
    n1 = _rms(o_nsa, go_ref[:, 0:D_NSA]).astype(BF16)
    n2 = _rms(of_ref[...], go_ref[:, D_NSA:D_NSA + D_FOX]).astype(BF16)
    n3 = _rms(os_ref[...], go_ref[:, D_NSA + D_FOX:]).astype(BF16)
    mix = (_dot(n1, wo_ref[0:D_NSA, :]) + _dot(n2, wo_ref[D_NSA:D_NSA + D_FOX, :])
           + _dot(n3, wo_ref[D_NSA + D_FOX:, :]))
    x1 = _layer_norm(alpha * x_ref[...] + mix, lg_ref[...], lb_ref[...])
    x1_ref[...] = x1
    logits = jnp.dot(x1, rw_ref[...], preferred_element_type=F32, precision=lax.Precision.HIGHEST) + rb_ref[...]
    tm, ne = logits.shape
    lane = lax.broadcasted_iota(jnp.int32, (tm, ne), 1).astype(F32)
    wide = lax.broadcasted_iota(jnp.int32, (tm, LANES), 1)
    top_e = jnp.zeros((tm, LANES), F32)
    top_l = jnp.full((tm, LANES), NEG, F32)
    cur = logits
    for k in range(TOP_K):
        mx = jnp.max(cur, axis=1, keepdims=True)
        idx = jnp.min(jnp.where(cur == mx, lane, float(ne)), axis=1, keepdims=True)
        top_e = jnp.where(wide == k, idx, top_e)
        top_l = jnp.where(wide == k, mx, top_l)
        cur = jnp.where(lane == idx, -BIG, cur)
    pe = jnp.exp(top_l - jnp.max(top_l, axis=1, keepdims=True))
    te_ref[...] = top_e.astype(jnp.int32)
    tw_ref[...] = pe / jnp.sum(pe, axis=1, keepdims=True)


def _post(o_c, o_s, o_w, o_fox, o_sb, x2d, g_out, w_out_bf, ln_g, ln_b, router_w, router_b, alpha):
    T, D = x2d.shape
    tm = POST_TM
    row = lambda w: pl.BlockSpec((tm, w), lambda i: (i, 0))
    full = lambda shape: pl.BlockSpec(shape, lambda i: (0, 0))
    return pl.pallas_call(
        functools.partial(_post_kernel, alpha=alpha),
        grid=(T // tm,),
        in_specs=[row(D_NSA), row(D_NSA), row(D_NSA), row(D_FOX), row(D_SB), row(D), full((1, D)), full((D, D)),
                  full((1, D)), full((1, D)), full((D, N_EXPERTS)), full((1, N_EXPERTS))],
        out_specs=[row(D), row(LANES), row(LANES)],
        out_shape=[jax.ShapeDtypeStruct((T, D), F32), jax.ShapeDtypeStruct((T, LANES), jnp.int32),
                   jax.ShapeDtypeStruct((T, LANES), F32)],
        compiler_params=_cparams(("parallel",)),
        name="outproj_ln_router",
    )(o_c, o_s, o_w, o_fox, o_sb, x2d, g_out.reshape(1, D), w_out_bf, ln_g.reshape(1, D), ln_b.reshape(1, D),
      router_w, router_b.reshape(1, N_EXPERTS))


def _row_gather_start(src_hbm, dst, sem, idx_ref, n_rows):
    def issue(r8, c):
        for u in range(8):
            r = r8 * 8 + u
            pltpu.make_async_copy(src_hbm.at[pl.ds(idx_ref[0, 0, r], 1)], dst.at[pl.ds(r, 1)],
                                  sem).start(priority=u % 2)
        return c
    lax.fori_loop(0, n_rows // 8, issue, 0)


def _row_gather_wait(src_hbm, dst, sem, n_rows):
    pltpu.make_async_copy(src_hbm.at[pl.ds(0, n_rows)], dst, sem).wait()


def _expert_kernel(blk_e_ref, nused_ref, idx_ref, idxn_ref, x_hbm, w_ref, wgu_ref, bgu_ref, wdn_ref, bdn_ref,
                   y_ref, xbuf, wgu_bf, wdn_bf, sem, *, tm, d_ff):
    i = pl.program_id(0)
    n_used = nused_ref[0]
    slot = i % 2

    @pl.when(i == 0)
    def _():
        _row_gather_start(x_hbm, xbuf.at[0], sem.at[0], idx_ref, tm)

    @pl.when(i <= n_used)
    def _():
        _row_gather_wait(x_hbm, xbuf.at[slot], sem.at[slot], tm)

    @pl.when(i < n_used)
    def _():
        changed = (i == 0) | (blk_e_ref[i] != blk_e_ref[jnp.maximum(i - 1, 0)])

        @pl.when(changed)
        def _():
            wgu_bf[...] = wgu_ref[0, 0].astype(BF16)
            wdn_bf[...] = wdn_ref[0, 0].astype(BF16)

        nxt = xbuf.at[1 - slot]
        for r in range(tm):
            pltpu.make_async_copy(x_hbm.at[pl.ds(idxn_ref[0, 0, r], 1)], nxt.at[pl.ds(r, 1)],
                                  sem.at[1 - slot]).start(priority=r % 2)
        xb = xbuf[slot].astype(BF16)
        gu = _dot(xb, wgu_bf[...]) + bgu_ref[0, 0]
        gate = jnp.minimum(gu[:, :d_ff], SWIGLU_LIMIT)
        up = jnp.clip(gu[:, d_ff:], -SWIGLU_LIMIT, SWIGLU_LIMIT)
        act = gate * _sigmoid(SWIGLU_ALPHA * gate) * (up + 1.0)
        y = _dot(act.astype(BF16), wdn_bf[...]) + bdn_ref[0, 0]
        y_ref[...] = y * w_ref[0]

    @pl.when(i >= n_used)
    def _():
        y_ref[...] = jnp.zeros_like(y_ref)


def _experts(x1, blk_e, n_used, buf_tok, buf_w, w_gu, b_gu, w_dn, b_dn, layer, tm):
    T, D = x1.shape
    nl, ne, _, d2 = w_gu.shape
    d_ff = d2 // 2
    n_blocks = blk_e.shape[0]
    idx = buf_tok.reshape(n_blocks, 1, tm)
    wcol = buf_w.reshape(n_blocks, tm, 1)
    last = n_blocks - 1
    grid_spec = pltpu.PrefetchScalarGridSpec(
        num_scalar_prefetch=2,
        grid=(n_blocks,),
        in_specs=[pl.BlockSpec((1, 1, tm), lambda i, e, n: (i, 0, 0), memory_space=pltpu.SMEM),
                  pl.BlockSpec((1, 1, tm), lambda i, e, n: (jnp.minimum(i + 1, last), 0, 0),
                               memory_space=pltpu.SMEM),
                  pl.BlockSpec(memory_space=pl.ANY),
                  pl.BlockSpec((1, tm, 1), lambda i, e, n: (i, 0, 0)),
                  pl.BlockSpec((1, 1, D, d2), lambda i, e, n: (layer, e[i], 0, 0)),
                  pl.BlockSpec((1, 1, 1, d2), lambda i, e, n: (layer, e[i], 0, 0)),
                  pl.BlockSpec((1, 1, d_ff, D), lambda i, e, n: (layer, e[i], 0, 0)),
                  pl.BlockSpec((1, 1, 1, D), lambda i, e, n: (layer, e[i], 0, 0))],
        out_specs=pl.BlockSpec((tm, D), lambda i, e, n: (i, 0)),
        scratch_shapes=[pltpu.VMEM((2, tm, D), F32), pltpu.VMEM((D, d2), BF16), pltpu.VMEM((d_ff, D), BF16),
                        pltpu.SemaphoreType.DMA((2,))],
    )
    return pl.pallas_call(
        functools.partial(_expert_kernel, tm=tm, d_ff=d_ff),
        grid_spec=grid_spec,
        out_shape=jax.ShapeDtypeStruct((n_blocks * tm, D), F32),
        compiler_params=_cparams(("arbitrary",), 58 * 1024 * 1024),
        name="moe_experts",
    )(blk_e, n_used, idx, idx, x1, wcol, w_gu, b_gu.reshape(nl, ne, 1, d2), w_dn, b_dn.reshape(nl, ne, 1, D))


def _combine_kernel(idx_ref, idxn_ref, y_hbm, x_ref, lg_ref, lb_ref, o_ref, ybuf, sem, *, tm, alpha):
    i = pl.program_id(0)
    n = pl.num_programs(0)
    slot = i % 2
    rows = TOP_K * tm

    @pl.when(i == 0)
    def _():
        _row_gather_start(y_hbm, ybuf.at[0], sem.at[0], idx_ref, rows)

    @pl.when(i + 1 < n)
    def _():
        _row_gather_start(y_hbm, ybuf.at[1 - slot], sem.at[1 - slot], idxn_ref, rows)

    _row_gather_wait(y_hbm, ybuf.at[slot], sem.at[slot], rows)
    ffn = ybuf[slot, 0:tm]
    for k in range(1, TOP_K):
        ffn = ffn + ybuf[slot, k * tm:(k + 1) * tm]
    o_ref[...] = _layer_norm(alpha * x_ref[...] + ffn, lg_ref[...], lb_ref[...])


def _combine(y_sorted, pos, x1, ln_g, ln_b, alpha):
    T, D = x1.shape
    tm = COMBINE_TM
    nt = T // tm
    idx = pos.reshape(nt, tm, TOP_K).transpose(0, 2, 1).reshape(nt, 1, TOP_K * tm)
    last = nt - 1
    return pl.pallas_call(
        functools.partial(_combine_kernel, tm=tm, alpha=alpha),
        grid=(nt,),
        in_specs=[pl.BlockSpec((1, 1, TOP_K * tm), lambda i: (i, 0, 0), memory_space=pltpu.SMEM),
                  pl.BlockSpec((1, 1, TOP_K * tm), lambda i: (jnp.minimum(i + 1, last), 0, 0),
                               memory_space=pltpu.SMEM),
                  pl.BlockSpec(memory_space=pl.ANY),
                  pl.BlockSpec((tm, D), lambda i: (i, 0)),
                  pl.BlockSpec((1, D), lambda i: (0, 0)),
                  pl.BlockSpec((1, D), lambda i: (0, 0))],
        out_specs=pl.BlockSpec((tm, D), lambda i: (i, 0)),
        out_shape=jax.ShapeDtypeStruct((T, D), F32),
        scratch_shapes=[pltpu.VMEM((2, TOP_K * tm, D), F32), pltpu.SemaphoreType.DMA((2,))],
        compiler_params=_cparams(("arbitrary",)),
        name="moe_combine_ln",
    )(idx, idx, y_sorted, x1, ln_g.reshape(1, D), ln_b.reshape(1, D))


def _route(top_e, top_w, tm):
    T = top_e.shape[0]
    M = T * TOP_K
    i32 = jnp.int32
    flat_e = top_e.reshape(M).astype(i32)
    flat_w = top_w.reshape(M)
    ar = jnp.arange(M, dtype=i32)
    experts = jnp.arange(N_EXPERTS, dtype=i32)
    skey, sorted_w = lax.sort((flat_e * M + ar, flat_w), num_keys=1)
    sorted_e = skey // M
    order = skey - sorted_e * M
    counts = jnp.sum((flat_e[:, None] == experts[None, :]).astype(i32), axis=0)
    start = jnp.cumsum(counts) - counts
    padded = (counts + tm - 1) // tm * tm
    pad_end = jnp.cumsum(padded)
    pad_start = pad_end - padded
    dest = pad_start[sorted_e] + ar - start[sorted_e]
    n_blocks = -(-M // tm) + N_EXPERTS
    blk_first = jnp.arange(n_blocks, dtype=i32) * tm
    blk_e = jnp.minimum(jnp.sum((pad_end[None, :] <= blk_first[:, None]).astype(i32), axis=1), N_EXPERTS - 1)
    within = (blk_first - pad_start[blk_e])[:, None] + jnp.arange(tm, dtype=i32)[None, :]
    valid = within < counts[blk_e][:, None]
    src_row = jnp.clip(start[blk_e][:, None] + within, 0, M - 1)
    buf_tok = jnp.where(valid, order[src_row] // TOP_K, 0).reshape(n_blocks * tm)
    buf_w = jnp.where(valid, sorted_w[src_row], 0.0).reshape(n_blocks * tm)
    _, pos = lax.sort((order, dest), num_keys=1)
    n_used = (pad_end[-1] // tm).astype(i32).reshape(1)
    return blk_e, n_used, buf_tok, buf_w, pos


class _Plan:
    def __init__(self):
        self.src, self.scale, self.const, self.outs = [], [], [], []

    def group(self, width, src_cols=(), at=0, scale=1.0, ones=()):
        src = np.full((width,), -1, np.int64)
        sc = np.zeros((width,), np.float32)
        const = np.zeros((width,), np.float32)
        src[at:at + len(src_cols)] = src_cols
        sc[at:at + len(src_cols)] = scale
        for col, val in ones:
            const[col] = val
        self.src.append(src)
        self.scale.append(sc)
        self.const.append(const)

    def out(self, n_heads, width, dtype):
        self.outs.append((n_heads, width, dtype))


def _cols(start):
    return np.arange(start, start + HEAD_DIM)


def _value_group(plan, src, odd):
    if odd:
        plan.group(LANES, _cols(src), at=HALF, ones=[(0, 1.0)])
    else:
        plan.group(LANES, _cols(src), ones=[(HALF, 1.0)])


def _plans(S):
    qs = HEAD_DIM ** -0.5
    slope = 2.0 ** (-8.0 * np.arange(1, N_NSA_HEADS + 1) / N_NSA_HEADS)
    a = _Plan()
    for g in range(N_NSA_KV):
        for hl in NSA_ROW_HEADS:
            h = g * NSA_REP + hl
            a.group(LANES, _cols(h * HEAD_DIM), scale=qs, ones=[(HALF, slope[h]), (HALF + 1, slope[h])])
    a.out(N_NSA_HEADS, LANES, BF16)
    kv = lambda branch, which, g: SRC_KV + branch * 256 + which * 128 + g * HEAD_DIM
    for which in range(2):
        for g in range(N_NSA_KV):
            a.group(LANES, _cols(kv(0, which, g)))
    a.out(2 * N_NSA_KV, LANES, F32)
    for g in range(N_NSA_KV):
        a.group(2 * LANES, _cols(kv(1, 0, g)))
    a.out(N_NSA_KV, 2 * LANES, BF16)
    for g in range(N_NSA_KV):
        _value_group(a, kv(1, 1, g), False)
        _value_group(a, kv(1, 1, g), True)
    a.out(N_NSA_KV, 2 * LANES, BF16)
    for g in range(N_NSA_KV):
        a.group(LANES, _cols(kv(2, 0, g)))
    a.out(N_NSA_KV, LANES, BF16)
    for g in range(N_NSA_KV):
        _value_group(a, kv(2, 1, g), False)
        _value_group(a, kv(2, 1, g), True)
    a.out(N_NSA_KV, 2 * LANES, BF16)
    a.group(LANES, np.concatenate([np.arange(SRC_GATE, SRC_GATE + 24), np.arange(SRC_LOGF, SRC_LOGF + 4)]))
    a.out(1, LANES, F32)

    n_a = sum(len(s) for s in a.src)
    extra = np.zeros((S, n_a), np.float32)
    pos = np.arange(S)
    pos_hi, pos_lo = pos // SEL_LEN * SEL_LEN, pos % SEL_LEN
    off_ksel = N_NSA_HEADS * LANES + 2 * N_NSA_KV * LANES
    off_kwin = off_ksel + N_NSA_KV * 2 * LANES + N_NSA_KV * 2 * LANES
    nb = S // SEL_LEN
    for g in range(N_NSA_KV):
        o = off_ksel + g * 2 * LANES
        extra[:, o + HALF], extra[:, o + HALF + 1] = pos_hi, pos_lo
        extra[pos, o + LANES + pos // SEL_LEN] = 1.0
        o = off_kwin + g * LANES
        extra[:, o + HALF], extra[:, o + HALF + 1] = pos_hi, pos_lo
    assert nb <= LANES

    b = _Plan()
    fox = lambda which, h: SRC_FOX + which * 256 + h * HEAD_DIM
    sb = lambda which, h: SRC_SB + which * 256 + h * HEAD_DIM
    for h in range(N_FOX_HEADS):
        b.group(LANES, _cols(fox(0, h)), scale=qs, ones=[(HALF, 1.0), (HALF + 1, 1.0), (HALF + 2, 1.0)])
    b.out(N_FOX_HEADS, LANES, BF16)
    for h in range(N_FOX_HEADS):
        b.group(LANES, _cols(fox(1, h)))
    b.out(N_FOX_HEADS, LANES, BF16)
    for h in range(N_FOX_HEADS):
        _value_group(b, fox(2, h), h % 2 == 1)
    b.out(N_FOX_HEADS, LANES, BF16)
    for h in range(N_SB_HEADS):
        b.group(LANES, _cols(sb(0, h)), scale=qs)
    b.out(N_SB_HEADS, LANES, BF16)
    for h in range(N_SB_HEADS):
        b.group(LANES, _cols(sb(1, h)))
    b.out(N_SB_HEADS, LANES, BF16)
    for h in range(N_SB_HEADS):
        b.group(LANES, _cols(sb(2, h)), at=HALF if h % 2 else 0)
    b.out(N_SB_HEADS, LANES, BF16)
    return a, jnp.asarray(extra, dtype=BF16), b


def _constants(S):
    plan_a, extra_a, plan_b = _plans(S)
    nc = S // CMP_STRIDE
    nb = S // SEL_LEN
    n_cmp = (S - CMP_LEN) // CMP_STRIDE + 1
    cmp_idx = np.arange(nc)[:, None] * CMP_STRIDE + np.arange(CMP_LEN)[None, :]
    ovl = (cmp_idx[:, :, None] // SEL_LEN == np.arange(nb)[None, None, :]).astype(np.float32).mean(axis=1)
    ovl[n_cmp:] = 0.0
    ovl = np.pad(ovl, ((0, 0), (0, LANES - nb)))
    cmp_end = np.arange(nc) * CMP_STRIDE + CMP_LEN - 1
    kcmp_aug = np.zeros((nc, LANES), np.float32)
    kcmp_aug[:, HALF] = cmp_end // SEL_LEN * SEL_LEN
    kcmp_aug[:, HALF + 1] = cmp_end % SEL_LEN
    return dict(a=plan_a, a_outs=tuple(plan_a.outs), extra_a=extra_a, b=plan_b, b_outs=tuple(plan_b.outs),
                ovl_t=jnp.asarray(ovl.T, dtype=BF16), kcmp_aug=jnp.asarray(kcmp_aug))


def _plan_weights(w_in, b_in, plan):
    src = np.concatenate(plan.src)
    scale = np.concatenate(plan.scale)
    const = jnp.asarray(np.concatenate(plan.const))
    wb = jnp.concatenate([w_in, b_in[None, :]], axis=0)
    pieces, c = [], 0
    while c < len(src):
        e = c + 1
        if src[c] < 0:
            while e < len(src) and src[e] < 0:
                e += 1
            pieces.append(jnp.zeros((wb.shape[0], e - c), F32))
        else:
            while e < len(src) and src[e] == src[e - 1] + 1 and scale[e] == scale[c]:
                e += 1
            pieces.append(wb[:, src[c]:src[c] + e - c] * float(scale[c]))
        c = e
    wb_p = jnp.concatenate(pieces, axis=1)
    return wb_p[:-1].astype(BF16), (wb_p[-1] + const).reshape(1, -1)


def _key_norm_max(k2):
    kf = k2[..., 0:HEAD_DIM].astype(F32)
    return jnp.sqrt(jnp.max(jnp.sum(kf * kf, axis=-1), axis=-1)).reshape(-1)


def _layer(x, cs, w_in, b_in, pos_k, pos_v, w1k, w2k, w1v, w2v, g_out, w_out, ln1_g, ln1_b,
           router_w, router_b, w_gu, b_gu, w_dn, b_dn, ln2_g, ln2_b, alpha, layer):
    B, S, D = x.shape
    T = B * S
    x2d = x.reshape(T, D)
    wa, ba = _plan_weights(w_in, b_in, cs["a"])
    wb, bb = _plan_weights(w_in, b_in, cs["b"])
    q2, kvc, ksel, vsel, kwin, vwin, small = _proj(x2d, wa, ba, cs["extra_a"], cs["a_outs"], B, S)
    fq, fk, fv, sq, sk, sv = _proj(x2d, wb, bb, None, cs["b_outs"], B, S)
    small = small.reshape(B, S, LANES)

    kc = kvc[:, 0:N_NSA_KV, :, 0:HEAD_DIM]
    vc = kvc[:, N_NSA_KV:, :, 0:HEAD_DIM]
    k_cmp, v_cmp = _compress(kc, vc, pos_k, pos_v, w1k, w2k, w1v, w2v, cs["kcmp_aug"])
    o_c, msel = _nsa_cmp(q2, k_cmp, v_cmp, cs["ovl_t"], small)
    o_s = _nsa_sel(q2, msel, ksel, vsel, small, _key_norm_max(ksel), min(SEL_KB, S))
    o_w = _nsa_win(q2, kwin, vwin, small)

    c = _logf_cumsum(small[..., 24:24 + N_FOX_HEADS].transpose(0, 2, 1))
    c_hi = _trunc_bf16(c)
    c_mid = _trunc_bf16(c - c_hi)
    c_lo = c - c_hi - c_mid
    c_aug = jnp.pad(-jnp.stack([c_hi, c_mid, c_lo], axis=-1), ((0, 0), (0, 0), (0, 0), (HALF, HALF - 3)))
    kb = min(FOX_KB, S)
    tile_bias = lax.cummax(jnp.max((-c).reshape(B, N_FOX_HEADS, S // kb, kb), axis=-1), axis=2)
    o_fox = _fox(fq, fk + c_aug.astype(BF16), fv, _key_norm_max(fk), tile_bias.reshape(-1))
    o_sb = _sb(sq, sk, sv)

    flat = lambda o: o.reshape(T, o.shape[-1])
    x1, te, tw = _post(flat(o_c), flat(o_s), flat(o_w), flat(o_fox), flat(o_sb), x2d, g_out, w_out.astype(BF16),
                       ln1_g, ln1_b, router_w, router_b, alpha)
    blk_e, n_used, buf_tok, buf_w, pos = _route(te[:, :TOP_K], tw[:, :TOP_K], MOE_TM)
    y_sorted = _experts(x1, blk_e, n_used, buf_tok, buf_w, w_gu, b_gu, w_dn, b_dn, layer, MOE_TM)
    out = _combine(y_sorted, pos, x1, ln2_g, ln2_b, alpha)
    return out.reshape(B, S, D)


def kernel(x, w_in, b_in, cmp_pos_k, cmp_pos_v, cmp_w1_k, cmp_w2_k, cmp_w1_v, cmp_w2_v, g_out, w_out,
           ln1_g, ln1_b, router_w, router_b, w_gate_up, b_gate_up, w_down, b_down, ln2_g, ln2_b):
    depth = w_in.shape[0]
    alpha = (2 * depth) ** 0.25
    consts = _constants(x.shape[1])
    for l in range(depth):
        x = _layer(x, consts, w_in[l], b_in[l], cmp_pos_k[l], cmp_pos_v[l], cmp_w1_k[l], cmp_w2_k[l],
                   cmp_w1_v[l], cmp_w2_v[l], g_out[l], w_out[l], ln1_g[l], ln1_b[l], router_w[l], router_b[l],
                   w_gate_up, b_gate_up, w_down, b_down, ln2_g[l], ln2_b[l], alpha, l)
    return x
```

```python
import functools

import jax
import jax.numpy as jnp
import numpy as np
from jax import lax
from jax.experimental import pallas as pl
from jax.experimental.pallas import tpu as pltpu

F32 = jnp.float32
BF16 = jnp.bfloat16

HEAD_DIM = 64
N_NSA_HEADS = 8
N_NSA_KV = 2
NSA_REP = 4
N_FOX_HEADS = 4
N_SB_HEADS = 4
D_NSA = 512
D_FOX = 256
D_SB = 256
CMP_LEN = 32
CMP_STRIDE = 16
CMP_HIDDEN = 128
SEL_LEN = 64
SEL_TOPK = 16
WINDOW = 512
Q_BLOCK = 128
N_EXPERTS = 32
TOP_K = 4
SWIGLU_LIMIT = 7.0
SWIGLU_ALPHA = 1.702
LN_EPS = 1e-5
RMS_EPS = 1e-6
NEG = -1e30
BIG = 3e38
SB_CUTOFF = -90.0
SKIP_T = 100.0

LANES = 128
HALF = LANES // 2
NSA_ROW_HEADS = (0, 2, 1, 3)

SRC_KV = 512
SRC_GATE = SRC_KV + 768
SRC_FOX = SRC_GATE + 24
SRC_LOGF = SRC_FOX + 768
SRC_SB = SRC_LOGF + 4

VMEM_LIMIT = 52 * 1024 * 1024

PROJ_TM = 512
CMP_SUB = 2
SEL_KB = 256
FOX_TQ = 512
FOX_KB = 256
SB_TQ = 128
POST_TM = 256
MOE_TM = 256
COMBINE_TM = 128


def _cparams(sem, vmem=VMEM_LIMIT):
    return pltpu.CompilerParams(dimension_semantics=sem, vmem_limit_bytes=vmem)


def _dot(a, b):
    return jnp.dot(a, b, preferred_element_type=F32)


def _dot_nt(a, b):
    return lax.dot_general(a, b, (((1,), (1,)), ((), ())), preferred_element_type=F32)


def _split_bf16(x):
    hi = x.astype(BF16)
    lo = (x - hi.astype(F32)).astype(BF16)
    return hi, lo


def _trunc_bf16(x):
    bits = lax.bitcast_convert_type(x, jnp.uint32) & jnp.uint32(0xFFFF0000)
    return lax.bitcast_convert_type(bits, F32)


def _sigmoid(x):
    return 1.0 / (1.0 + jnp.exp(-x))


def _lane_column(tile, idx):
    lane = lax.broadcasted_iota(jnp.int32, tile.shape, 1)
    return jnp.sum(jnp.where(lane == idx, tile, 0.0), axis=1, keepdims=True)


def _low_half(shape):
    return lax.broadcasted_iota(jnp.int32, shape, 1) < HALF


def _proj_kernel(*refs, outs, has_extra):
    x_ref, w_ref, b_ref = refs[:3]
    o_refs = refs[3 + has_extra:]
    y = _dot(x_ref[...].astype(BF16), w_ref[...]) + b_ref[...]
    if has_extra:
        y = y + refs[3][...].astype(F32)
    off = 0
    for o_ref, (nh, width, _) in zip(o_refs, outs):
        for h in range(nh):
            o_ref[0, h] = y[:, off:off + width].astype(o_ref.dtype)
            off += width


def _proj(x2d, w_bf, b_row, extra, outs, B, S):
    T, D = x2d.shape
    N = w_bf.shape[1]
    tm = min(PROJ_TM, S)
    spb = S // tm
    in_specs = [pl.BlockSpec((tm, D), lambda i: (i, 0)),
                pl.BlockSpec((D, N), lambda i: (0, 0)),
                pl.BlockSpec((1, N), lambda i: (0, 0))]
    args = [x2d, w_bf, b_row]
    if extra is not None:
        in_specs.append(pl.BlockSpec((tm, N), lambda i: (i % spb, 0)))
        args.append(extra)
    return pl.pallas_call(
        functools.partial(_proj_kernel, outs=outs, has_extra=extra is not None),
        grid=(T // tm,),
        in_specs=in_specs,
        out_specs=[pl.BlockSpec((1, nh, tm, w), lambda i: (i // spb, 0, i % spb, 0)) for nh, w, _ in outs],
        out_shape=[jax.ShapeDtypeStruct((B, nh, S, w), dt) for nh, w, dt in outs],
        compiler_params=_cparams(("parallel",)),
        name="proj",
    )(*args)


def _gelu_tanh(x):
    return 0.5 * x * (1.0 + jnp.tanh(0.7978845608028654 * (x + 0.044715 * (x * x * x))))


def _compress_kernel(k_ref, v_ref, pk_ref, pv_ref, w1k_ref, w2k_ref, w1v_ref, w2v_ref, ka_ref, ok_ref, ov_ref):
    nc = k_ref.shape[2]
    half = CMP_STRIDE * HEAD_DIM

    def one(r_ref, p_ref, w1_ref, w2_ref):
        r = r_ref[0, 0]
        a = (r + p_ref[0:1, :]).astype(BF16)
        b = (r + p_ref[1:2, :]).astype(BF16)
        ha = _dot(a, w1_ref[0:half, :])
        hb = _dot(b, w1_ref[half:2 * half, :])
        hid = _gelu_tanh(ha + pltpu.roll(hb, nc - 1, 0))
        return _dot(hid.astype(BF16), w2_ref[...])

    ok_ref[0, 0] = (one(k_ref, pk_ref, w1k_ref, w2k_ref) + ka_ref[...]).astype(ok_ref.dtype)
    ov_ref[0, 0] = one(v_ref, pv_ref, w1v_ref, w2v_ref).astype(ov_ref.dtype)


def _compress(kc, vc, pos_k, pos_v, w1k, w2k, w1v, w2v, kcmp_aug):
    B, G, S, dh = kc.shape
    nc = S // CMP_STRIDE
    width = CMP_STRIDE * dh
    kr = kc.reshape(B, G, nc, width)
    vr = vc.reshape(B, G, nc, width)
    pk = pos_k.reshape(2, width)
    pv = pos_v.reshape(2, width)
    w2k_p = jnp.pad(w2k, ((0, 0), (0, LANES - dh))).astype(BF16)
    w2v_p = jnp.concatenate([jnp.pad(w2v, ((0, 0), (0, LANES - dh))), jnp.pad(w2v, ((0, 0), (LANES - dh, 0)))],
                            axis=1).astype(BF16)
    kv_spec = pl.BlockSpec((1, 1, nc, width), lambda b, g: (b, g, 0, 0))
    full = lambda shape: pl.BlockSpec(shape, lambda b, g: tuple(0 for _ in shape))
    return pl.pallas_call(
        _compress_kernel,
        grid=(B, G),
        in_specs=[kv_spec, kv_spec, full((2, width)), full((2, width)),
                  full((2 * width, CMP_HIDDEN)), full((CMP_HIDDEN, LANES)),
                  full((2 * width, CMP_HIDDEN)), full((CMP_HIDDEN, 2 * LANES)), full((nc, LANES))],
        out_specs=[pl.BlockSpec((1, 1, nc, LANES), lambda b, g: (b, g, 0, 0)),
                   pl.BlockSpec((1, 1, nc, 2 * LANES), lambda b, g: (b, g, 0, 0))],
        out_shape=[jax.ShapeDtypeStruct((B, G, nc, LANES), BF16),
                   jax.ShapeDtypeStruct((B, G, nc, 2 * LANES), BF16)],
        compiler_params=_cparams(("parallel", "parallel")),
        name="nsa_compress",
    )(kr, vr, pk, pv, w1k.astype(BF16), w2k_p, w1v.astype(BF16), w2v_p, kcmp_aug)


def _nsa_gates(small, branch, g):
    return [_sigmoid(_lane_column(small, branch * N_NSA_HEADS + g * NSA_REP + h)) for h in NSA_ROW_HEADS]


def _nsa_cmp_kernel(q_ref, kc_ref, vc_ref, ovl_ref, sm_ref, o_ref, sel_ref, *, n_sel, n_sub):
    g = pl.program_id(1)
    i = pl.program_id(2)
    results = [_nsa_cmp_block(q_ref, kc_ref, vc_ref, ovl_ref, sm_ref, g, i * n_sub + u, u, n_sel)
               for u in range(n_sub)]
    for u, (o, msel) in enumerate(results):
        o_ref[0, u * Q_BLOCK:(u + 1) * Q_BLOCK, :] = o
        sel_ref[0, 0, u] = msel


def _nsa_cmp_block(q_ref, kc_ref, vc_ref, ovl_ref, sm_ref, g, qb, u, n_sel):
    q0 = qb * Q_BLOCK
    rows = slice(u * Q_BLOCK, (u + 1) * Q_BLOCK)
    nc = kc_ref.shape[2]
    nb = ovl_ref.shape[0]
    kc = kc_ref[0, 0]
    vc = vc_ref[0, 0]
    gates = _nsa_gates(sm_ref[0, rows, :], 0, g)
    t = q0 + lax.broadcasted_iota(jnp.int32, (Q_BLOCK, nc), 0)
    cmp_end = lax.broadcasted_iota(jnp.int32, (Q_BLOCK, nc), 1) * CMP_STRIDE + (CMP_LEN - 1)
    mask = cmp_end <= t
    psum = jnp.zeros((Q_BLOCK, nc), F32)
    outs = []
    for r in range(NSA_REP):
        s = jnp.where(mask, _dot_nt(q_ref[0, r, rows, :], kc), NEG)
        m = jnp.max(s, axis=1, keepdims=True)
        p = jnp.where(mask, jnp.exp(s - m), 0.0)
        l = jnp.sum(p, axis=1, keepdims=True)
        pn = p / jnp.where(l > 0.0, l, 1.0)
        v_half = vc[:, 0:LANES] if r < 2 else vc[:, LANES:2 * LANES]
        outs.append(_dot(pn.astype(BF16), v_half) * gates[r])
        psum = psum + pn
    o = jnp.concatenate([outs[0] + outs[2], outs[1] + outs[3]], axis=1)
    hi, lo = _split_bf16(psum)
    ovl = ovl_ref[...]
    imp = _dot_nt(ovl, hi) + _dot_nt(ovl, lo)
    jf = lax.broadcasted_iota(jnp.int32, (nb, Q_BLOCK), 0)
    tq = q0 + lax.broadcasted_iota(jnp.int32, (nb, Q_BLOCK), 1)
    cur = tq >> 6
    valid = jf <= cur
    forced = valid & ((jf == 0) | (jf >= cur - 1))
    v = jnp.where(valid & ~forced, imp, -BIG)
    jff = jf.astype(F32)
    picked = jnp.where(forced, 1.0, 0.0)
    for _ in range(max(n_sel - 3, 0)):
        mx = jnp.max(v, axis=0, keepdims=True)
        idx = jnp.min(jnp.where(v == mx, jff, float(nb)), axis=0, keepdims=True)
        hit = jff == idx
        picked = jnp.where(hit, 1.0, picked)
        v = jnp.where(hit, -BIG, v)
    msel = jnp.where(valid & (picked > 0.5), 0.0, NEG)
    return o, msel.T.astype(BF16)


def _nsa_cmp(q2, kcmp, vcmp, ovl_t, small):
    B, H, S, qw = q2.shape
    G = N_NSA_KV
    nq = S // Q_BLOCK
    nc = kcmp.shape[2]
    nb = ovl_t.shape[0]
    n_sel = min(SEL_TOPK, S // SEL_LEN)
    n_sub = CMP_SUB
    return pl.pallas_call(
        functools.partial(_nsa_cmp_kernel, n_sel=n_sel, n_sub=n_sub),
        grid=(B, G, nq // n_sub),
        in_specs=[pl.BlockSpec((1, NSA_REP, n_sub * Q_BLOCK, qw), lambda b, g, i: (b, g, i, 0)),
                  pl.BlockSpec((1, 1, nc, LANES), lambda b, g, i: (b, g, 0, 0)),
                  pl.BlockSpec((1, 1, nc, 2 * LANES), lambda b, g, i: (b, g, 0, 0)),
                  pl.BlockSpec((nb, nc), lambda b, g, i: (0, 0)),
                  pl.BlockSpec((1, n_sub * Q_BLOCK, LANES), lambda b, g, i: (b, i, 0))],
        out_specs=[pl.BlockSpec((1, n_sub * Q_BLOCK, 2 * LANES), lambda b, g, i: (b, i, g)),
                   pl.BlockSpec((1, 1, n_sub, Q_BLOCK, nb), lambda b, g, i: (b, g, i, 0, 0))],
        out_shape=[jax.ShapeDtypeStruct((B, S, D_NSA), F32),
                   jax.ShapeDtypeStruct((B, G, nq, Q_BLOCK, nb), BF16)],
        compiler_params=_cparams(("parallel", "parallel", "parallel")),
        name="nsa_cmp_select",
    )(q2, kcmp, vcmp, ovl_t, small)


def _flash_init(rows):
    return jnp.full((rows, 1), NEG, F32), jnp.zeros((rows, LANES), F32)


def _flash_update(carry, s, v_tile):
    m, acc = carry
    half = s.shape[0] // 2
    m_new = jnp.maximum(m, jnp.max(s, axis=1, keepdims=True))
    p = jnp.exp(s - m_new).astype(BF16)
    pv = jnp.concatenate([_dot(p[:half], v_tile[:, 0:LANES]), _dot(p[half:], v_tile[:, LANES:2 * LANES])], axis=0)
    return m_new, jnp.exp(m - m_new) * acc + pv


def _pair_out(acc_even, acc_odd):
    even = acc_even / acc_even[:, HALF:HALF + 1]
    odd = acc_odd / acc_odd[:, 0:1]
    return jnp.where(_low_half(acc_even.shape), even, odd)


def _nsa_out(acc, gates):
    c = [acc[r * Q_BLOCK:(r + 1) * Q_BLOCK] for r in range(NSA_REP)]
    low = _low_half((Q_BLOCK, LANES))
    pair_a = _pair_out(c[0], c[2]) * jnp.where(low, gates[0], gates[2])
    pair_b = _pair_out(c[1], c[3]) * jnp.where(low, gates[1], gates[3])
    return jnp.concatenate([pair_a, pair_b], axis=1)


def _stacked_rows(rows, kb):
    row = lax.broadcasted_iota(jnp.int32, (rows, kb), 0) & (Q_BLOCK - 1)
    col = lax.broadcasted_iota(jnp.int32, (rows, kb), 1)
    return row, col


def _nsa_sel_kernel(kmax_ref, q_ref, msel_ref, k_ref, v_ref, sm_ref, o_ref, qa_ref, sa_ref, sb_ref, *, kb):
    g = pl.program_id(1)
    i = pl.program_id(2)
    q0 = i * Q_BLOCK
    rows = NSA_REP * Q_BLOCK
    msel = msel_ref[0, 0, 0]
    for r in range(NSA_REP):
        qa_ref[r * Q_BLOCK:(r + 1) * Q_BLOCK, :] = jnp.concatenate([q_ref[0, r], msel], axis=1)
    qa = qa_ref[...]
    last = (q0 + Q_BLOCK - 1) // kb

    def logits(kt):
        k0 = pl.multiple_of(kt * kb, kb)
        return _dot_nt(qa, k_ref[0, 0, pl.ds(k0, kb), :])

    def v_tile(kt):
        return v_ref[0, 0, pl.ds(pl.multiple_of(kt * kb, kb), kb), :]

    reach = _query_norm(qa) * kmax_ref[pl.program_id(0) * N_NSA_KV + g]
    slope = qa[:, HEAD_DIM:HEAD_DIM + 1].astype(F32)
    row, col = _stacked_rows(rows, kb)
    causal = (last * kb + col) <= (q0 + row)
    carry = _flash_update(_flash_init(rows), jnp.where(causal, logits(last), NEG), v_tile(last))
    odd = last & 1
    single = jnp.maximum(last - 1, 0)
    carry = lax.cond(odd == 1, lambda c: _flash_update(c, logits(single), v_tile(single)), lambda c: c, carry)
    top0 = last - 1 - odd
    sa_ref[...] = logits(jnp.maximum(top0, 0))

    def margin(top, m):
        last_pos = (jnp.maximum(top, 0) * kb + kb - 1).astype(F32)
        return jnp.max(reach + slope * last_pos - m)

    def cond(state):
        top, ahead, _ = state
        return (top >= 0) & (ahead > -SKIP_T)

    def body(state):
        top, _, carry = state
        ahead = margin(top - 2, carry[0])
        sb_ref[...] = logits(top - 1)
        carry = _flash_update(carry, sa_ref[...], v_tile(top))
        sa_ref[...] = logits(jnp.maximum(top - 2, 0))
        return top - 2, ahead, _flash_update(carry, sb_ref[...], v_tile(top - 1))

    _, _, (_, acc) = lax.while_loop(cond, body, (top0, margin(top0, carry[0]), carry))
    o_ref[0] = _nsa_out(acc, _nsa_gates(sm_ref[0], 1, g))


def _nsa_sel(q2, msel, k2, v2, small, kmax, kb):
    B, H, S, qw = q2.shape
    G = N_NSA_KV
    nq = S // Q_BLOCK
    nb = msel.shape[-1]
    grid_spec = pltpu.PrefetchScalarGridSpec(
        num_scalar_prefetch=1,
        grid=(B, G, nq),
        in_specs=[pl.BlockSpec((1, NSA_REP, Q_BLOCK, qw), lambda b, g, i, km: (b, g, i, 0)),
                  pl.BlockSpec((1, 1, 1, Q_BLOCK, nb), lambda b, g, i, km: (b, g, i, 0, 0)),
                  pl.BlockSpec((1, 1, S, k2.shape[-1]), lambda b, g, i, km: (b, g, 0, 0)),
                  pl.BlockSpec((1, 1, S, 2 * LANES), lambda b, g, i, km: (b, g, 0, 0)),
                  pl.BlockSpec((1, Q_BLOCK, LANES), lambda b, g, i, km: (b, i, 0))],
        out_specs=pl.BlockSpec((1, Q_BLOCK, 2 * LANES), lambda b, g, i, km: (b, i, g)),
        scratch_shapes=[pltpu.VMEM((NSA_REP * Q_BLOCK, qw + nb), BF16),
                        pltpu.VMEM((NSA_REP * Q_BLOCK, kb), F32),
                        pltpu.VMEM((NSA_REP * Q_BLOCK, kb), F32)],
    )
    return pl.pallas_call(
        functools.partial(_nsa_sel_kernel, kb=kb),
        grid_spec=grid_spec,
        out_shape=jax.ShapeDtypeStruct((B, S, D_NSA), F32),
        compiler_params=_cparams(("parallel", "parallel", "parallel")),
        name="nsa_selected",
    )(kmax, q2, msel, k2, v2, small)


def _nsa_win_kernel(q_ref, k_ref, v_ref, sm_ref, o_ref):
    g = pl.program_id(1)
    i = pl.program_id(2)
    kb = Q_BLOCK
    rows = NSA_REP * Q_BLOCK
    qa = q_ref[0].reshape(rows, q_ref.shape[-1])
    n_back = WINDOW // kb
    row, col = _stacked_rows(rows, kb)
    carry = _flash_init(rows)
    for d in range(n_back + 1):
        kt = i - n_back + d
        k0 = pl.multiple_of(jnp.maximum(kt, 0) * kb, kb)
        s = _dot_nt(qa, k_ref[0, 0, pl.ds(k0, kb), :])
        if d == 0:
            s = jnp.where((col > row) & (kt >= 0), s, NEG)
        elif d == n_back:
            s = jnp.where(col <= row, s, NEG)
        else:
            s = jnp.where(kt >= 0, s, NEG)
        carry = _flash_update(carry, s, v_ref[0, 0, pl.ds(k0, kb), :])
    o_ref[0] = _nsa_out(carry[1], _nsa_gates(sm_ref[0], 2, g))


def _nsa_win(q2, k2, v2, small):
    B, H, S, qw = q2.shape
    G = N_NSA_KV
    nq = S // Q_BLOCK
    return pl.pallas_call(
        _nsa_win_kernel,
        grid=(B, G, nq),
        in_specs=[pl.BlockSpec((1, NSA_REP, Q_BLOCK, qw), lambda b, g, i: (b, g, i, 0)),
                  pl.BlockSpec((1, 1, S, qw), lambda b, g, i: (b, g, 0, 0)),
                  pl.BlockSpec((1, 1, S, 2 * LANES), lambda b, g, i: (b, g, 0, 0)),
                  pl.BlockSpec((1, Q_BLOCK, LANES), lambda b, g, i: (b, i, 0))],
        out_specs=pl.BlockSpec((1, Q_BLOCK, 2 * LANES), lambda b, g, i: (b, i, g)),
        out_shape=jax.ShapeDtypeStruct((B, S, D_NSA), F32),
        compiler_params=_cparams(("parallel", "parallel", "parallel")),
        name="nsa_window",
    )(q2, k2, v2, small)


def _fox_update(carry, s, v_tile):
    m, acc = carry
    m_new = jnp.maximum(m, jnp.max(s, axis=1, keepdims=True))
    p = jnp.exp(s - m_new).astype(BF16)
    return m_new, jnp.exp(m - m_new) * acc + _dot(p, v_tile)


def _query_norm(q):
    qf = q.astype(F32)
    lane = lax.broadcasted_iota(jnp.int32, qf.shape, 1)
    return jnp.sqrt(jnp.sum(jnp.where(lane < HEAD_DIM, qf * qf, 0.0), axis=1, keepdims=True))


def _fox_kernel(kmax_ref, tb_ref, q_ref, k_ref, v_ref, o_ref, *s_refs, tq, kb, n_heads, n_tiles):
    b = pl.program_id(0)
    hp = pl.program_id(1)
    i = pl.program_id(2)
    heads = [b * n_heads + hp * 2 + h for h in range(2)]

    def logits(h, kt):
        k0 = pl.multiple_of(kt * kb, kb)
        return _dot_nt(q_ref[0, h], k_ref[0, h, pl.ds(k0, kb), :])

    def v_tile(h, kt):
        return v_ref[0, h, pl.ds(pl.multiple_of(kt * kb, kb), kb), :]

    row = lax.broadcasted_iota(jnp.int32, (tq, kb), 0)
    col = lax.broadcasted_iota(jnp.int32, (tq, kb), 1)
    reach, carries = [], []
    for h in range(2):
        reach.append(_query_norm(q_ref[0, h]) * kmax_ref[heads[h]])
        c = _fox_update(_flash_init(tq), jnp.where(col <= row, logits(h, 2 * i), NEG), v_tile(h, 2 * i))
        carries.append(_fox_update(c, jnp.where(kb + col <= row, logits(h, 2 * i + 1), NEG), v_tile(h, 2 * i + 1)))
        s_refs[2 * h][...] = logits(h, jnp.maximum(2 * i - 1, 0))

    def margin(top, carries):
        t = jnp.maximum(top, 0)
        a = jnp.max(reach[0] + tb_ref[heads[0] * n_tiles + t] - carries[0][0])
        return jnp.maximum(a, jnp.max(reach[1] + tb_ref[heads[1] * n_tiles + t] - carries[1][0]))

    def cond(state):
        top, ahead, _ = state
        return (top >= 0) & (ahead > -SKIP_T)

    def body(state):
        top, _, carries = state
        ahead = margin(top - 2, carries)
        out = []
        for h in range(2):
            sa_ref, sb_ref = s_refs[2 * h], s_refs[2 * h + 1]
            sb_ref[...] = logits(h, top - 1)
            c = _fox_update(carries[h], sa_ref[...], v_tile(h, top))
            sa_ref[...] = logits(h, jnp.maximum(top - 2, 0))
            out.append(_fox_update(c, sb_ref[...], v_tile(h, top - 1)))
        return top - 2, ahead, tuple(out)

    _, _, carries = lax.while_loop(cond, body, (2 * i - 1, margin(2 * i - 1, carries), tuple(carries)))
    o_ref[0] = _pair_out(carries[0][1], carries[1][1])


def _fox(q2, k2, v2, kmax, tile_bias):
    B, H, S, qw = q2.shape
    tq, kb = min(FOX_TQ, S), min(FOX_KB, S)
    assert tq == 2 * kb
    grid_spec = pltpu.PrefetchScalarGridSpec(
        num_scalar_prefetch=2,
        grid=(B, H // 2, S // tq),
        in_specs=[pl.BlockSpec((1, 2, tq, qw), lambda b, h, i, km, tb: (b, h, i, 0)),
                  pl.BlockSpec((1, 2, S, qw), lambda b, h, i, km, tb: (b, h, 0, 0)),
                  pl.BlockSpec((1, 2, S, LANES), lambda b, h, i, km, tb: (b, h, 0, 0))],
        out_specs=pl.BlockSpec((1, tq, LANES), lambda b, h, i, km, tb: (b, i, h)),
        scratch_shapes=[pltpu.VMEM((tq, kb), F32)] * 4,
    )
    return pl.pallas_call(
        functools.partial(_fox_kernel, tq=tq, kb=kb, n_heads=H, n_tiles=S // kb),
        grid_spec=grid_spec,
        out_shape=jax.ShapeDtypeStruct((B, S, D_FOX), F32),
        compiler_params=_cparams(("parallel", "parallel", "parallel")),
        name="fox_attention",
    )(kmax, tile_bias, q2, k2, v2)


def _logf_cumsum_kernel(x_ref, c_ref):
    x = x_ref[0]
    nr, nl = x.shape
    log_f = jnp.minimum(x, 0.0) - jnp.log1p(jnp.exp(-jnp.abs(x)))
    hp = lax.Precision.HIGHEST
    incl = (lax.broadcasted_iota(jnp.int32, (nl, nl), 0) <= lax.broadcasted_iota(jnp.int32, (nl, nl), 1))
    within = jnp.dot(log_f, incl.astype(F32), preferred_element_type=F32, precision=hp)
    totals = jnp.broadcast_to(within[:, nl - 1:nl], (nr, nl))
    before = (lax.broadcasted_iota(jnp.int32, (nr, nr), 1) < lax.broadcasted_iota(jnp.int32, (nr, nr), 0))
    c_ref[0] = within + jnp.dot(before.astype(F32), totals, preferred_element_type=F32, precision=hp)


def _logf_cumsum(logits):
    B, H, S = logits.shape
    x = logits.reshape(B * H, S // LANES, LANES)
    c = pl.pallas_call(
        _logf_cumsum_kernel,
        grid=(B * H,),
        in_specs=[pl.BlockSpec((1, S // LANES, LANES), lambda i: (i, 0, 0))],
        out_specs=pl.BlockSpec((1, S // LANES, LANES), lambda i: (i, 0, 0)),
        out_shape=jax.ShapeDtypeStruct(x.shape, F32),
        compiler_params=_cparams(("parallel",)),
        name="logf_cumsum",
    )(x)
    return c.reshape(B, H, S)


def _sb_kernel(q_ref, k_ref, v_ref, o_ref, *, tq, nh):
    i = pl.program_id(1)
    q0 = i * tq
    trow = q0 + lax.broadcasted_iota(jnp.int32, (tq, tq), 0)
    col = lax.broadcasted_iota(jnp.int32, (tq, tq), 1)
    rr = lax.broadcasted_iota(jnp.int32, (tq, tq), 0)
    upper = (rr > col).astype(BF16)

    def cond(state):
        kt, carries, _ = state
        alive = jnp.max(carries[0])
        for h in range(1, nh):
            alive = jnp.maximum(alive, jnp.max(carries[h]))
        return (kt >= 0) & (alive > SB_CUTOFF)

    def body(state):
        kt, carries, accs = state
        k0 = pl.multiple_of(kt * tq, tq)
        strict = (k0 + col) < trow
        new_c, new_a = [], []
        for h in range(nh):
            z = _dot_nt(q_ref[0, h], k_ref[0, h, pl.ds(k0, tq), :])
            log_beta = jnp.minimum(z, 0.0) - jnp.log1p(jnp.exp(-jnp.abs(z)))
            log_keep = jnp.where(strict, log_beta - z, 0.0)
            hi, lo = _split_bf16(log_keep)
            later = _dot(hi, upper) + _dot(lo, upper)
            a = jnp.where(strict, jnp.exp(log_beta + later + carries[h]), 0.0)
            new_a.append(accs[h] + _dot(a.astype(BF16), v_ref[0, h, pl.ds(k0, tq), :]))
            new_c.append(carries[h] + jnp.sum(log_keep, axis=1, keepdims=True))
        return kt - 1, tuple(new_c), tuple(new_a)

    state = (i, tuple(jnp.zeros((tq, 1), F32) for _ in range(nh)),
             tuple(jnp.zeros((tq, LANES), F32) for _ in range(nh)))
    _, _, accs = lax.while_loop(cond, body, state)
    o_ref[0] = jnp.concatenate([accs[2 * j] + accs[2 * j + 1] for j in range(nh // 2)], axis=1)


def _sb(q2, k2, v2):
    B, H, S, w = q2.shape
    tq = min(SB_TQ, S)
    return pl.pallas_call(
        functools.partial(_sb_kernel, tq=tq, nh=H),
        grid=(B, S // tq),
        in_specs=[pl.BlockSpec((1, H, tq, w), lambda b, i: (b, 0, i, 0)),
                  pl.BlockSpec((1, H, S, w), lambda b, i: (b, 0, 0, 0)),
                  pl.BlockSpec((1, H, S, w), lambda b, i: (b, 0, 0, 0))],
        out_specs=pl.BlockSpec((1, tq, D_SB), lambda b, i: (b, i, 0)),
        out_shape=jax.ShapeDtypeStruct((B, S, D_SB), F32),
        compiler_params=_cparams(("parallel", "parallel")),
        name="sb_attention",
    )(q2, k2, v2)


def _layer_norm(y, g, b):
    mu = jnp.mean(y, axis=1, keepdims=True)
    d = y - mu
    var = jnp.mean(d * d, axis=1, keepdims=True)
    return d * lax.rsqrt(var + LN_EPS) * g + b


def _rms(o, g):
    return o * lax.rsqrt(jnp.mean(o * o, axis=1, keepdims=True) + RMS_EPS) * g


def _post_kernel(oc_ref, osel_ref, ow_ref, of_ref, os_ref, x_ref, go_ref, wo_ref, lg_ref, lb_ref, rw_ref, rb_ref,
                 x1_ref, te_ref, tw_ref, *, alpha):
    o_nsa = oc_ref[...] + osel_ref[...] + ow_ref[...]
    n1 = _rms(o_nsa, go_ref[:, 0:D_NSA]).astype(BF16)
    n2 = _rms(of_ref[...], go_ref[:, D_NSA:D_NSA + D_FOX]).astype(BF16)
    n3 = _rms(os_ref[...], go_ref[:, D_NSA + D_FOX:]).astype(BF16)
    mix = (_dot(n1, wo_ref[0:D_NSA, :]) + _dot(n2, wo_ref[D_NSA:D_NSA + D_FOX, :])
           + _dot(n3, wo_ref[D_NSA + D_FOX:, :]))
    x1 = _layer_norm(alpha * x_ref[...] + mix, lg_ref[...], lb_ref[...])
    x1_ref[...] = x1
    logits = jnp.dot(x1, rw_ref[...], preferred_element_type=F32, precision=lax.Precision.HIGHEST) + rb_ref[...]
    tm, ne = logits.shape
    lane = lax.broadcasted_iota(jnp.int32, (tm, ne), 1).astype(F32)
    wide = lax.broadcasted_iota(jnp.int32, (tm, LANES), 1)
    top_e = jnp.zeros((tm, LANES), F32)
    top_l = jnp.full((tm, LANES), NEG, F32)
    cur = logits
    for k in range(TOP_K):
        mx = jnp.max(cur, axis=1, keepdims=True)
        idx = jnp.min(jnp.where(cur == mx, lane, float(ne)), axis=1, keepdims=True)
        top_e = jnp.where(wide == k, idx, top_e)
        top_l = jnp.where(wide == k, mx, top_l)
        cur = jnp.where(lane == idx, -BIG, cur)
    pe = jnp.exp(top_l - jnp.max(top_l, axis=1, keepdims=True))
    te_ref[...] = top_e.astype(jnp.int32)
    tw_ref[...] = pe / jnp.sum(pe, axis=1, keepdims=True)


def _post(o_c, o_s, o_w, o_fox, o_sb, x2d, g_out, w_out_bf, ln_g, ln_b, router_w, router_b, alpha):
    T, D = x2d.shape
    tm = POST_TM
    row = lambda w: pl.BlockSpec((tm, w), lambda i: (i, 0))
    full = lambda shape: pl.BlockSpec(shape, lambda i: (0, 0))
    return pl.pallas_call(
        functools.partial(_post_kernel, alpha=alpha),
        grid=(T // tm,),
        in_specs=[row(D_NSA), row(D_NSA), row(D_NSA), row(D_FOX), row(D_SB), row(D), full((1, D)), full((D, D)),
                  full((1, D)), full((1, D)), full((D, N_EXPERTS)), full((1, N_EXPERTS))],
        out_specs=[row(D), row(LANES), row(LANES)],
        out_shape=[jax.ShapeDtypeStruct((T, D), F32), jax.ShapeDtypeStruct((T, LANES), jnp.int32),
                   jax.ShapeDtypeStruct((T, LANES), F32)],
        compiler_params=_cparams(("parallel",)),
        name="outproj_ln_router",
    )(o_c, o_s, o_w, o_fox, o_sb, x2d, g_out.reshape(1, D), w_out_bf, ln_g.reshape(1, D), ln_b.reshape(1, D),
      router_w, router_b.reshape(1, N_EXPERTS))


def _row_gather_start(src_hbm, dst, sem, idx_ref, n_rows):
    def issue(r8, c):
        for u in range(8):
            r = r8 * 8 + u
            pltpu.make_async_copy(src_hbm.at[pl.ds(idx_ref[0, 0, r], 1)], dst.at[pl.ds(r, 1)],
                                  sem).start(priority=u % 2)
        return c
    lax.fori_loop(0, n_rows // 8, issue, 0)


def _row_gather_wait(src_hbm, dst, sem, n_rows):
    pltpu.make_async_copy(src_hbm.at[pl.ds(0, n_rows)], dst, sem).wait()


def _expert_kernel(blk_e_ref, nused_ref, idx_ref, idxn_ref, x_hbm, w_ref, wgu_ref, bgu_ref, wdn_ref, bdn_ref,
                   y_ref, xbuf, wgu_bf, wdn_bf, sem, *, tm, d_ff):
    i = pl.program_id(0)
    n_used = nused_ref[0]
    slot = i % 2

    @pl.when(i == 0)
    def _():
        _row_gather_start(x_hbm, xbuf.at[0], sem.at[0], idx_ref, tm)

    @pl.when(i <= n_used)
    def _():
        _row_gather_wait(x_hbm, xbuf.at[slot], sem.at[slot], tm)

    @pl.when(i < n_used)
    def _():
        changed = (i == 0) | (blk_e_ref[i] != blk_e_ref[jnp.maximum(i - 1, 0)])

        @pl.when(changed)
        def _():
            wgu_bf[...] = wgu_ref[0, 0].astype(BF16)
            wdn_bf[...] = wdn_ref[0, 0].astype(BF16)

        nxt = xbuf.at[1 - slot]
        for r in range(tm):
            pltpu.make_async_copy(x_hbm.at[pl.ds(idxn_ref[0, 0, r], 1)], nxt.at[pl.ds(r, 1)],
                                  sem.at[1 - slot]).start(priority=r % 2)
        xb = xbuf[slot].astype(BF16)
        gu = _dot(xb, wgu_bf[...]) + bgu_ref[0, 0]
        gate = jnp.minimum(gu[:, :d_ff], SWIGLU_LIMIT)
        up = jnp.clip(gu[:, d_ff:], -SWIGLU_LIMIT, SWIGLU_LIMIT)
        act = gate * _sigmoid(SWIGLU_ALPHA * gate) * (up + 1.0)
        y = _dot(act.astype(BF16), wdn_bf[...]) + bdn_ref[0, 0]
        y_ref[...] = y * w_ref[0]

    @pl.when(i >= n_used)
    def _():
        y_ref[...] = jnp.zeros_like(y_ref)


def _experts(x1, blk_e, n_used, buf_tok, buf_w, w_gu, b_gu, w_dn, b_dn, layer, tm):
    T, D = x1.shape
    nl, ne, _, d2 = w_gu.shape
    d_ff = d2 // 2
    n_blocks = blk_e.shape[0]
    idx = buf_tok.reshape(n_blocks, 1, tm)
    wcol = buf_w.reshape(n_blocks, tm, 1)
    last = n_blocks - 1
    grid_spec = pltpu.PrefetchScalarGridSpec(
        num_scalar_prefetch=2,
        grid=(n_blocks,),
        in_specs=[pl.BlockSpec((1, 1, tm), lambda i, e, n: (i, 0, 0), memory_space=pltpu.SMEM),
                  pl.BlockSpec((1, 1, tm), lambda i, e, n: (jnp.minimum(i + 1, last), 0, 0),
                               memory_space=pltpu.SMEM),
                  pl.BlockSpec(memory_space=pl.ANY),
                  pl.BlockSpec((1, tm, 1), lambda i, e, n: (i, 0, 0)),
                  pl.BlockSpec((1, 1, D, d2), lambda i, e, n: (layer, e[i], 0, 0)),
                  pl.BlockSpec((1, 1, 1, d2), lambda i, e, n: (layer, e[i], 0, 0)),
                  pl.BlockSpec((1, 1, d_ff, D), lambda i, e, n: (layer, e[i], 0, 0)),
                  pl.BlockSpec((1, 1, 1, D), lambda i, e, n: (layer, e[i], 0, 0))],
        out_specs=pl.BlockSpec((tm, D), lambda i, e, n: (i, 0)),
        scratch_shapes=[pltpu.VMEM((2, tm, D), F32), pltpu.VMEM((D, d2), BF16), pltpu.VMEM((d_ff, D), BF16),
                        pltpu.SemaphoreType.DMA((2,))],
    )
    return pl.pallas_call(
        functools.partial(_expert_kernel, tm=tm, d_ff=d_ff),
        grid_spec=grid_spec,
        out_shape=jax.ShapeDtypeStruct((n_blocks * tm, D), F32),
        compiler_params=_cparams(("arbitrary",), 58 * 1024 * 1024),
        name="moe_experts",
    )(blk_e, n_used, idx, idx, x1, wcol, w_gu, b_gu.reshape(nl, ne, 1, d2), w_dn, b_dn.reshape(nl, ne, 1, D))


def _combine_kernel(idx_ref, idxn_ref, y_hbm, x_ref, lg_ref, lb_ref, o_ref, ybuf, sem, *, tm, alpha):
    i = pl.program_id(0)
    n = pl.num_programs(0)
    slot = i % 2
    rows = TOP_K * tm

    @pl.when(i == 0)
    def _():
        _row_gather_start(y_hbm, ybuf.at[0], sem.at[0], idx_ref, rows)

    @pl.when(i + 1 < n)
    def _():
        nxt = ybuf.at[1 - slot]
        for r in range(rows):
            pltpu.make_async_copy(y_hbm.at[pl.ds(idxn_ref[0, 0, r], 1)], nxt.at[pl.ds(r, 1)],
                                  sem.at[1 - slot]).start(priority=r % 2)

    _row_gather_wait(y_hbm, ybuf.at[slot], sem.at[slot], rows)
    ffn = ybuf[slot, 0:tm]
    for k in range(1, TOP_K):
        ffn = ffn + ybuf[slot, k * tm:(k + 1) * tm]
    o_ref[...] = _layer_norm(alpha * x_ref[...] + ffn, lg_ref[...], lb_ref[...])


def _combine(y_sorted, pos, x1, ln_g, ln_b, alpha):
    T, D = x1.shape
    tm = COMBINE_TM
    nt = T // tm
    idx = pos.reshape(nt, tm, TOP_K).transpose(0, 2, 1).reshape(nt, 1, TOP_K * tm)
    last = nt - 1
    return pl.pallas_call(
        functools.partial(_combine_kernel, tm=tm, alpha=alpha),
        grid=(nt,),
        in_specs=[pl.BlockSpec((1, 1, TOP_K * tm), lambda i: (i, 0, 0), memory_space=pltpu.SMEM),
                  pl.BlockSpec((1, 1, TOP_K * tm), lambda i: (jnp.minimum(i + 1, last), 0, 0),
                               memory_space=pltpu.SMEM),
                  pl.BlockSpec(memory_space=pl.ANY),
                  pl.BlockSpec((tm, D), lambda i: (i, 0)),
                  pl.BlockSpec((1, D), lambda i: (0, 0)),
                  pl.BlockSpec((1, D), lambda i: (0, 0))],
        out_specs=pl.BlockSpec((tm, D), lambda i: (i, 0)),
        out_shape=jax.ShapeDtypeStruct((T, D), F32),
        scratch_shapes=[pltpu.VMEM((2, TOP_K * tm, D), F32), pltpu.SemaphoreType.DMA((2,))],
        compiler_params=_cparams(("arbitrary",)),
        name="moe_combine_ln",
    )(idx, idx, y_sorted, x1, ln_g.reshape(1, D), ln_b.reshape(1, D))


def _route(top_e, top_w, tm):
    T = top_e.shape[0]
    M = T * TOP_K
    i32 = jnp.int32
    flat_e = top_e.reshape(M).astype(i32)
    flat_w = top_w.reshape(M)
    ar = jnp.arange(M, dtype=i32)
    experts = jnp.arange(N_EXPERTS, dtype=i32)
    skey, sorted_w = lax.sort((flat_e * M + ar, flat_w), num_keys=1)
    sorted_e = skey // M
    order = skey - sorted_e * M
    counts = jnp.sum((flat_e[:, None] == experts[None, :]).astype(i32), axis=0)
    start = jnp.cumsum(counts) - counts
    padded = (counts + tm - 1) // tm * tm
    pad_end = jnp.cumsum(padded)
    pad_start = pad_end - padded
    dest = pad_start[sorted_e] + ar - start[sorted_e]
    n_blocks = -(-M // tm) + N_EXPERTS
    blk_first = jnp.arange(n_blocks, dtype=i32) * tm
    blk_e = jnp.minimum(jnp.sum((pad_end[None, :] <= blk_first[:, None]).astype(i32), axis=1), N_EXPERTS - 1)
    within = (blk_first - pad_start[blk_e])[:, None] + jnp.arange(tm, dtype=i32)[None, :]
    valid = within < counts[blk_e][:, None]
    src_row = jnp.clip(start[blk_e][:, None] + within, 0, M - 1)
    buf_tok = jnp.where(valid, order[src_row] // TOP_K, 0).reshape(n_blocks * tm)
    buf_w = jnp.where(valid, sorted_w[src_row], 0.0).reshape(n_blocks * tm)
    _, pos = lax.sort((order, dest), num_keys=1)
    n_used = (pad_end[-1] // tm).astype(i32).reshape(1)
    return blk_e, n_used, buf_tok, buf_w, pos


class _Plan:
    def __init__(self):
        self.src, self.scale, self.const, self.outs = [], [], [], []

    def group(self, width, src_cols=(), at=0, scale=1.0, ones=()):
        src = np.full((width,), -1, np.int64)
        sc = np.zeros((width,), np.float32)
        const = np.zeros((width,), np.float32)
        src[at:at + len(src_cols)] = src_cols
        sc[at:at + len(src_cols)] = scale
        for col, val in ones:
            const[col] = val
        self.src.append(src)
        self.scale.append(sc)
        self.const.append(const)

    def out(self, n_heads, width, dtype):
        self.outs.append((n_heads, width, dtype))


def _cols(start):
    return np.arange(start, start + HEAD_DIM)


def _value_group(plan, src, odd):
    if odd:
        plan.group(LANES, _cols(src), at=HALF, ones=[(0, 1.0)])
    else:
        plan.group(LANES, _cols(src), ones=[(HALF, 1.0)])


def _plans(S):
    qs = HEAD_DIM ** -0.5
    slope = 2.0 ** (-8.0 * np.arange(1, N_NSA_HEADS + 1) / N_NSA_HEADS)
    a = _Plan()
    for g in range(N_NSA_KV):
        for hl in NSA_ROW_HEADS:
            h = g * NSA_REP + hl
            a.group(LANES, _cols(h * HEAD_DIM), scale=qs, ones=[(HALF, slope[h]), (HALF + 1, slope[h])])
    a.out(N_NSA_HEADS, LANES, BF16)
    kv = lambda branch, which, g: SRC_KV + branch * 256 + which * 128 + g * HEAD_DIM
    for which in range(2):
        for g in range(N_NSA_KV):
            a.group(LANES, _cols(kv(0, which, g)))
    a.out(2 * N_NSA_KV, LANES, F32)
    for g in range(N_NSA_KV):
        a.group(2 * LANES, _cols(kv(1, 0, g)))
    a.out(N_NSA_KV, 2 * LANES, BF16)
    for g in range(N_NSA_KV):
        _value_group(a, kv(1, 1, g), False)
        _value_group(a, kv(1, 1, g), True)
    a.out(N_NSA_KV, 2 * LANES, BF16)
    for g in range(N_NSA_KV):
        a.group(LANES, _cols(kv(2, 0, g)))
    a.out(N_NSA_KV, LANES, BF16)
    for g in range(N_NSA_KV):
        _value_group(a, kv(2, 1, g), False)
        _value_group(a, kv(2, 1, g), True)
    a.out(N_NSA_KV, 2 * LANES, BF16)
    a.group(LANES, np.concatenate([np.arange(SRC_GATE, SRC_GATE + 24), np.arange(SRC_LOGF, SRC_LOGF + 4)]))
    a.out(1, LANES, F32)

    n_a = sum(len(s) for s in a.src)
    extra = np.zeros((S, n_a), np.float32)
    pos = np.arange(S)
    pos_hi, pos_lo = pos // SEL_LEN * SEL_LEN, pos % SEL_LEN
    off_ksel = N_NSA_HEADS * LANES + 2 * N_NSA_KV * LANES
    off_kwin = off_ksel + N_NSA_KV * 2 * LANES + N_NSA_KV * 2 * LANES
    nb = S // SEL_LEN
    for g in range(N_NSA_KV):
        o = off_ksel + g * 2 * LANES
        extra[:, o + HALF], extra[:, o + HALF + 1] = pos_hi, pos_lo
        extra[pos, o + LANES + pos // SEL_LEN] = 1.0
        o = off_kwin + g * LANES
        extra[:, o + HALF], extra[:, o + HALF + 1] = pos_hi, pos_lo
    assert nb <= LANES

    b = _Plan()
    fox = lambda which, h: SRC_FOX + which * 256 + h * HEAD_DIM
    sb = lambda which, h: SRC_SB + which * 256 + h * HEAD_DIM
    for h in range(N_FOX_HEADS):
        b.group(LANES, _cols(fox(0, h)), scale=qs, ones=[(HALF, 1.0), (HALF + 1, 1.0), (HALF + 2, 1.0)])
    b.out(N_FOX_HEADS, LANES, BF16)
    for h in range(N_FOX_HEADS):
        b.group(LANES, _cols(fox(1, h)))
    b.out(N_FOX_HEADS, LANES, BF16)
    for h in range(N_FOX_HEADS):
        _value_group(b, fox(2, h), h % 2 == 1)
    b.out(N_FOX_HEADS, LANES, BF16)
    for h in range(N_SB_HEADS):
        b.group(LANES, _cols(sb(0, h)), scale=qs)
    b.out(N_SB_HEADS, LANES, BF16)
    for h in range(N_SB_HEADS):
        b.group(LANES, _cols(sb(1, h)))
    b.out(N_SB_HEADS, LANES, BF16)
    for h in range(N_SB_HEADS):
        b.group(LANES, _cols(sb(2, h)), at=HALF if h % 2 else 0)
    b.out(N_SB_HEADS, LANES, BF16)
    return a, jnp.asarray(extra, dtype=BF16), b


def _constants(S):
    plan_a, extra_a, plan_b = _plans(S)
    nc = S // CMP_STRIDE
    nb = S // SEL_LEN
    n_cmp = (S - CMP_LEN) // CMP_STRIDE + 1
    cmp_idx = np.arange(nc)[:, None] * CMP_STRIDE + np.arange(CMP_LEN)[None, :]
    ovl = (cmp_idx[:, :, None] // SEL_LEN == np.arange(nb)[None, None, :]).astype(np.float32).mean(axis=1)
    ovl[n_cmp:] = 0.0
    ovl = np.pad(ovl, ((0, 0), (0, LANES - nb)))
    cmp_end = np.arange(nc) * CMP_STRIDE + CMP_LEN - 1
    kcmp_aug = np.zeros((nc, LANES), np.float32)
    kcmp_aug[:, HALF] = cmp_end // SEL_LEN * SEL_LEN
    kcmp_aug[:, HALF + 1] = cmp_end % SEL_LEN
    return dict(a=plan_a, a_outs=tuple(plan_a.outs), extra_a=extra_a, b=plan_b, b_outs=tuple(plan_b.outs),
                ovl_t=jnp.asarray(ovl.T, dtype=BF16), kcmp_aug=jnp.asarray(kcmp_aug))


def _plan_weights(w_in, b_in, plan):
    src = np.concatenate(plan.src)
    scale = np.concatenate(plan.scale)
    const = jnp.asarray(np.concatenate(plan.const))
    wb = jnp.concatenate([w_in, b_in[None, :]], axis=0)
    pieces, c = [], 0
    while c < len(src):
        e = c + 1
        if src[c] < 0:
            while e < len(src) and src[e] < 0:
                e += 1
            pieces.append(jnp.zeros((wb.shape[0], e - c), F32))
        else:
            while e < len(src) and src[e] == src[e - 1] + 1 and scale[e] == scale[c]:
                e += 1
            pieces.append(wb[:, src[c]:src[c] + e - c] * float(scale[c]))
        c = e
    wb_p = jnp.concatenate(pieces, axis=1)
    return wb_p[:-1].astype(BF16), (wb_p[-1] + const).reshape(1, -1)


def _key_norm_max(k2):
    kf = k2[..., 0:HEAD_DIM].astype(F32)
    return jnp.sqrt(jnp.max(jnp.sum(kf * kf, axis=-1), axis=-1)).reshape(-1)


def _layer(x, cs, w_in, b_in, pos_k, pos_v, w1k, w2k, w1v, w2v, g_out, w_out, ln1_g, ln1_b,
           router_w, router_b, w_gu, b_gu, w_dn, b_dn, ln2_g, ln2_b, alpha, layer):
    B, S, D = x.shape
    T = B * S
    x2d = x.reshape(T, D)
    wa, ba = _plan_weights(w_in, b_in, cs["a"])
    wb, bb = _plan_weights(w_in, b_in, cs["b"])
    q2, kvc, ksel, vsel, kwin, vwin, small = _proj(x2d, wa, ba, cs["extra_a"], cs["a_outs"], B, S)
    fq, fk, fv, sq, sk, sv = _proj(x2d, wb, bb, None, cs["b_outs"], B, S)
    small = small.reshape(B, S, LANES)

    kc = kvc[:, 0:N_NSA_KV, :, 0:HEAD_DIM]
    vc = kvc[:, N_NSA_KV:, :, 0:HEAD_DIM]
    k_cmp, v_cmp = _compress(kc, vc, pos_k, pos_v, w1k, w2k, w1v, w2v, cs["kcmp_aug"])
    o_c, msel = _nsa_cmp(q2, k_cmp, v_cmp, cs["ovl_t"], small)
    o_s = _nsa_sel(q2, msel, ksel, vsel, small, _key_norm_max(ksel), min(SEL_KB, S))
    o_w = _nsa_win(q2, kwin, vwin, small)

    c = _logf_cumsum(small[..., 24:24 + N_FOX_HEADS].transpose(0, 2, 1))
    c_hi = _trunc_bf16(c)
    c_mid = _trunc_bf16(c - c_hi)
    c_lo = c - c_hi - c_mid
    c_aug = jnp.pad(-jnp.stack([c_hi, c_mid, c_lo], axis=-1), ((0, 0), (0, 0), (0, 0), (HALF, HALF - 3)))
    kb = min(FOX_KB, S)
    tile_bias = lax.cummax(jnp.max((-c).reshape(B, N_FOX_HEADS, S // kb, kb), axis=-1), axis=2)
    o_fox = _fox(fq, fk + c_aug.astype(BF16), fv, _key_norm_max(fk), tile_bias.reshape(-1))
    o_sb = _sb(sq, sk, sv)

    flat = lambda o: o.reshape(T, o.shape[-1])
    x1, te, tw = _post(flat(o_c), flat(o_s), flat(o_w), flat(o_fox), flat(o_sb), x2d, g_out, w_out.astype(BF16),
                       ln1_g, ln1_b, router_w, router_b, alpha)
    blk_e, n_used, buf_tok, buf_w, pos = _route(te[:, :TOP_K], tw[:, :TOP_K], MOE_TM)
    y_sorted = _experts(x1, blk_e, n_used, buf_tok, buf_w, w_gu, b_gu, w_dn, b_dn, layer, MOE_TM)
    out = _combine(y_sorted, pos, x1, ln2_g, ln2_b, alpha)
    return out.reshape(B, S, D)


def kernel(x, w_in, b_in, cmp_pos_k, cmp_pos_v, cmp_w1_k, cmp_w2_k, cmp_w1_v, cmp_w2_v, g_out, w_out,
           ln1_g, ln1_b, router_w, router_b, w_gate_up, b_gate_up, w_down, b_down, ln2_g, ln2_b):
    depth = w_in.shape[0]
    alpha = (2 * depth) ** 0.25
    consts = _constants(x.shape[1])
    for l in range(depth):
        x = _layer(x, consts, w_in[l], b_in[l], cmp_pos_k[l], cmp_pos_v[l], cmp_w1_k[l], cmp_w2_k[l],
                   cmp_w1_v[l], cmp_w2_v[l], g_out[l], w_out[l], ln1_g[l], ln1_b[l], router_w[l], router_b[l],
                   w_gate_up, b_gate_up, w_down, b_down, ln2_g[l], ln2_b[l], alpha, l)
    return x
```

```python
import functools

import jax
import jax.numpy as jnp
import numpy as np
from jax import lax
from jax.experimental import pallas as pl
from jax.experimental.pallas import tpu as pltpu

F32 = jnp.float32
BF16 = jnp.bfloat16

HEAD_DIM = 64
N_NSA_HEADS = 8
N_NSA_KV = 2
NSA_REP = 4
N_FOX_HEADS = 4
N_SB_HEADS = 4
D_NSA = 512
D_FOX = 256
D_SB = 256
CMP_LEN = 32
CMP_STRIDE = 16
CMP_HIDDEN = 128
SEL_LEN = 64
SEL_TOPK = 16
WINDOW = 512
Q_BLOCK = 128
N_EXPERTS = 32
TOP_K = 4
SWIGLU_LIMIT = 7.0
SWIGLU_ALPHA = 1.702
LN_EPS = 1e-5
RMS_EPS = 1e-6
NEG = -1e30
BIG = 3e38
SB_CUTOFF = -90.0
SKIP_T = 100.0

LANES = 128
HALF = LANES // 2
NSA_ROW_HEADS = (0, 2, 1, 3)

SRC_KV = 512
SRC_GATE = SRC_KV + 768
SRC_FOX = SRC_GATE + 24
SRC_LOGF = SRC_FOX + 768
SRC_SB = SRC_LOGF + 4

VMEM_LIMIT = 52 * 1024 * 1024

PROJ_TM = 512
CMP_SUB = 4
SEL_KB = 256
FOX_TQ = 512
FOX_KB = 256
SB_TQ = 128
POST_TM = 256
MOE_TM = 256
COMBINE_TM = 128


def _cparams(sem, vmem=VMEM_LIMIT):
    return pltpu.CompilerParams(dimension_semantics=sem, vmem_limit_bytes=vmem)


def _dot(a, b):
    return jnp.dot(a, b, preferred_element_type=F32)


def _dot_nt(a, b):
    return lax.dot_general(a, b, (((1,), (1,)), ((), ())), preferred_element_type=F32)


def _split_bf16(x):
    hi = x.astype(BF16)
    lo = (x - hi.astype(F32)).astype(BF16)
    return hi, lo


def _trunc_bf16(x):
    bits = lax.bitcast_convert_type(x, jnp.uint32) & jnp.uint32(0xFFFF0000)
    return lax.bitcast_convert_type(bits, F32)


def _sigmoid(x):
    return 1.0 / (1.0 + jnp.exp(-x))


def _lane_column(tile, idx):
    lane = lax.broadcasted_iota(jnp.int32, tile.shape, 1)
    return jnp.sum(jnp.where(lane == idx, tile, 0.0), axis=1, keepdims=True)


def _low_half(shape):
    return lax.broadcasted_iota(jnp.int32, shape, 1) < HALF


def _proj_kernel(*refs, outs, has_extra):
    x_ref, w_ref, b_ref = refs[:3]
    o_refs = refs[3 + has_extra:]
    y = _dot(x_ref[...].astype(BF16), w_ref[...]) + b_ref[...]
    if has_extra:
        y = y + refs[3][...].astype(F32)
    off = 0
    for o_ref, (nh, width, _) in zip(o_refs, outs):
        for h in range(nh):
            o_ref[0, h] = y[:, off:off + width].astype(o_ref.dtype)
            off += width


def _proj(x2d, w_bf, b_row, extra, outs, B, S):
    T, D = x2d.shape
    N = w_bf.shape[1]
    tm = min(PROJ_TM, S)
    spb = S // tm
    in_specs = [pl.BlockSpec((tm, D), lambda i: (i, 0)),
                pl.BlockSpec((D, N), lambda i: (0, 0)),
                pl.BlockSpec((1, N), lambda i: (0, 0))]
    args = [x2d, w_bf, b_row]
    if extra is not None:
        in_specs.append(pl.BlockSpec((tm, N), lambda i: (i % spb, 0)))
        args.append(extra)
    return pl.pallas_call(
        functools.partial(_proj_kernel, outs=outs, has_extra=extra is not None),
        grid=(T // tm,),
        in_specs=in_specs,
        out_specs=[pl.BlockSpec((1, nh, tm, w), lambda i: (i // spb, 0, i % spb, 0)) for nh, w, _ in outs],
        out_shape=[jax.ShapeDtypeStruct((B, nh, S, w), dt) for nh, w, dt in outs],
        compiler_params=_cparams(("parallel",)),
        name="proj",
    )(*args)


def _gelu_tanh(x):
    return 0.5 * x * (1.0 + jnp.tanh(0.7978845608028654 * (x + 0.044715 * (x * x * x))))


def _compress_kernel(k_ref, v_ref, pk_ref, pv_ref, w1k_ref, w2k_ref, w1v_ref, w2v_ref, ka_ref, ok_ref, ov_ref):
    nc = k_ref.shape[2]
    half = CMP_STRIDE * HEAD_DIM

    def one(r_ref, p_ref, w1_ref, w2_ref):
        r = r_ref[0, 0]
        a = (r + p_ref[0:1, :]).astype(BF16)
        b = (r + p_ref[1:2, :]).astype(BF16)
        ha = _dot(a, w1_ref[0:half, :])
        hb = _dot(b, w1_ref[half:2 * half, :])
        hid = _gelu_tanh(ha + pltpu.roll(hb, nc - 1, 0))
        return _dot(hid.astype(BF16), w2_ref[...])

    ok_ref[0, 0] = (one(k_ref, pk_ref, w1k_ref, w2k_ref) + ka_ref[...]).astype(ok_ref.dtype)
    ov_ref[0, 0] = one(v_ref, pv_ref, w1v_ref, w2v_ref).astype(ov_ref.dtype)


def _compress(kc, vc, pos_k, pos_v, w1k, w2k, w1v, w2v, kcmp_aug):
    B, G, S, dh = kc.shape
    nc = S // CMP_STRIDE
    width = CMP_STRIDE * dh
    kr = kc.reshape(B, G, nc, width)
    vr = vc.reshape(B, G, nc, width)
    pk = pos_k.reshape(2, width)
    pv = pos_v.reshape(2, width)
    w2k_p = jnp.pad(w2k, ((0, 0), (0, LANES - dh))).astype(BF16)
    w2v_p = jnp.concatenate([jnp.pad(w2v, ((0, 0), (0, LANES - dh))), jnp.pad(w2v, ((0, 0), (LANES - dh, 0)))],
                            axis=1).astype(BF16)
    kv_spec = pl.BlockSpec((1, 1, nc, width), lambda b, g: (b, g, 0, 0))
    full = lambda shape: pl.BlockSpec(shape, lambda b, g: tuple(0 for _ in shape))
    return pl.pallas_call(
        _compress_kernel,
        grid=(B, G),
        in_specs=[kv_spec, kv_spec, full((2, width)), full((2, width)),
                  full((2 * width, CMP_HIDDEN)), full((CMP_HIDDEN, LANES)),
                  full((2 * width, CMP_HIDDEN)), full((CMP_HIDDEN, 2 * LANES)), full((nc, LANES))],
        out_specs=[pl.BlockSpec((1, 1, nc, LANES), lambda b, g: (b, g, 0, 0)),
                   pl.BlockSpec((1, 1, nc, 2 * LANES), lambda b, g: (b, g, 0, 0))],
        out_shape=[jax.ShapeDtypeStruct((B, G, nc, LANES), BF16),
                   jax.ShapeDtypeStruct((B, G, nc, 2 * LANES), BF16)],
        compiler_params=_cparams(("parallel", "parallel")),
        name="nsa_compress",
    )(kr, vr, pk, pv, w1k.astype(BF16), w2k_p, w1v.astype(BF16), w2v_p, kcmp_aug)


def _nsa_gates(small, branch, g):
    return [_sigmoid(_lane_column(small, branch * N_NSA_HEADS + g * NSA_REP + h)) for h in NSA_ROW_HEADS]


def _nsa_cmp_kernel(q_ref, kc_ref, vc_ref, ovl_ref, sm_ref, o_ref, sel_ref, *, n_sel, n_sub):
    g = pl.program_id(1)
    i = pl.program_id(2)
    results = [_nsa_cmp_block(q_ref, kc_ref, vc_ref, ovl_ref, sm_ref, g, i * n_sub + u, u, n_sel)
               for u in range(n_sub)]
    for u, (o, msel) in enumerate(results):
        o_ref[0, u * Q_BLOCK:(u + 1) * Q_BLOCK, :] = o
        sel_ref[0, 0, u] = msel


def _nsa_cmp_block(q_ref, kc_ref, vc_ref, ovl_ref, sm_ref, g, qb, u, n_sel):
    q0 = qb * Q_BLOCK
    rows = slice(u * Q_BLOCK, (u + 1) * Q_BLOCK)
    nc = kc_ref.shape[2]
    nb = ovl_ref.shape[0]
    kc = kc_ref[0, 0]
    vc = vc_ref[0, 0]
    gates = _nsa_gates(sm_ref[0, rows, :], 0, g)
    t = q0 + lax.broadcasted_iota(jnp.int32, (Q_BLOCK, nc), 0)
    cmp_end = lax.broadcasted_iota(jnp.int32, (Q_BLOCK, nc), 1) * CMP_STRIDE + (CMP_LEN - 1)
    mask = cmp_end <= t
    psum = jnp.zeros((Q_BLOCK, nc), F32)
    outs = []
    for r in range(NSA_REP):
        s = jnp.where(mask, _dot_nt(q_ref[0, r, rows, :], kc), NEG)
        m = jnp.max(s, axis=1, keepdims=True)
        p = jnp.where(mask, jnp.exp(s - m), 0.0)
        l = jnp.sum(p, axis=1, keepdims=True)
        pn = p / jnp.where(l > 0.0, l, 1.0)
        v_half = vc[:, 0:LANES] if r < 2 else vc[:, LANES:2 * LANES]
        outs.append(_dot(pn.astype(BF16), v_half) * gates[r])
        psum = psum + pn
    o = jnp.concatenate([outs[0] + outs[2], outs[1] + outs[3]], axis=1)
    hi, lo = _split_bf16(psum)
    ovl = ovl_ref[...]
    imp = _dot_nt(ovl, hi) + _dot_nt(ovl, lo)
    jf = lax.broadcasted_iota(jnp.int32, (nb, Q_BLOCK), 0)
    tq = q0 + lax.broadcasted_iota(jnp.int32, (nb, Q_BLOCK), 1)
    cur = tq >> 6
    valid = jf <= cur
    forced = valid & ((jf == 0) | (jf >= cur - 1))
    v = jnp.where(valid & ~forced, imp, -BIG)
    jff = jf.astype(F32)
    picked = jnp.where(forced, 1.0, 0.0)
    for _ in range(max(n_sel - 3, 0)):
        mx = jnp.max(v, axis=0, keepdims=True)
        idx = jnp.min(jnp.where(v == mx, jff, float(nb)), axis=0, keepdims=True)
        hit = jff == idx
        picked = jnp.where(hit, 1.0, picked)
        v = jnp.where(hit, -BIG, v)
    msel = jnp.where(valid & (picked > 0.5), 0.0, NEG)
    return o, msel.T.astype(BF16)


def _nsa_cmp(q2, kcmp, vcmp, ovl_t, small):
    B, H, S, qw = q2.shape
    G = N_NSA_KV
    nq = S // Q_BLOCK
    nc = kcmp.shape[2]
    nb = ovl_t.shape[0]
    n_sel = min(SEL_TOPK, S // SEL_LEN)
    n_sub = CMP_SUB
    return pl.pallas_call(
        functools.partial(_nsa_cmp_kernel, n_sel=n_sel, n_sub=n_sub),
        grid=(B, G, nq // n_sub),
        in_specs=[pl.BlockSpec((1, NSA_REP, n_sub * Q_BLOCK, qw), lambda b, g, i: (b, g, i, 0)),
                  pl.BlockSpec((1, 1, nc, LANES), lambda b, g, i: (b, g, 0, 0)),
                  pl.BlockSpec((1, 1, nc, 2 * LANES), lambda b, g, i: (b, g, 0, 0)),
                  pl.BlockSpec((nb, nc), lambda b, g, i: (0, 0)),
                  pl.BlockSpec((1, n_sub * Q_BLOCK, LANES), lambda b, g, i: (b, i, 0))],
        out_specs=[pl.BlockSpec((1, n_sub * Q_BLOCK, 2 * LANES), lambda b, g, i: (b, i, g)),
                   pl.BlockSpec((1, 1, n_sub, Q_BLOCK, nb), lambda b, g, i: (b, g, i, 0, 0))],
        out_shape=[jax.ShapeDtypeStruct((B, S, D_NSA), F32),
                   jax.ShapeDtypeStruct((B, G, nq, Q_BLOCK, nb), BF16)],
        compiler_params=_cparams(("parallel", "parallel", "parallel")),
        name="nsa_cmp_select",
    )(q2, kcmp, vcmp, ovl_t, small)


def _flash_init(rows):
    return jnp.full((rows, 1), NEG, F32), jnp.zeros((rows, LANES), F32)


def _flash_update(carry, s, v_tile):
    m, acc = carry
    half = s.shape[0] // 2
    m_new = jnp.maximum(m, jnp.max(s, axis=1, keepdims=True))
    p = jnp.exp(s - m_new).astype(BF16)
    pv = jnp.concatenate([_dot(p[:half], v_tile[:, 0:LANES]), _dot(p[half:], v_tile[:, LANES:2 * LANES])], axis=0)
    return m_new, jnp.exp(m - m_new) * acc + pv


def _pair_out(acc_even, acc_odd):
    even = acc_even / acc_even[:, HALF:HALF + 1]
    odd = acc_odd / acc_odd[:, 0:1]
    return jnp.where(_low_half(acc_even.shape), even, odd)


def _nsa_out(acc, gates):
    c = [acc[r * Q_BLOCK:(r + 1) * Q_BLOCK] for r in range(NSA_REP)]
    low = _low_half((Q_BLOCK, LANES))
    pair_a = _pair_out(c[0], c[2]) * jnp.where(low, gates[0], gates[2])
    pair_b = _pair_out(c[1], c[3]) * jnp.where(low, gates[1], gates[3])
    return jnp.concatenate([pair_a, pair_b], axis=1)


def _stacked_rows(rows, kb):
    row = lax.broadcasted_iota(jnp.int32, (rows, kb), 0) & (Q_BLOCK - 1)
    col = lax.broadcasted_iota(jnp.int32, (rows, kb), 1)
    return row, col


def _nsa_sel_kernel(kmax_ref, q_ref, msel_ref, k_ref, v_ref, sm_ref, o_ref, qa_ref, sa_ref, sb_ref, *, kb):
    g = pl.program_id(1)
    i = pl.program_id(2)
    q0 = i * Q_BLOCK
    rows = NSA_REP * Q_BLOCK
    msel = msel_ref[0, 0, 0]
    for r in range(NSA_REP):
        qa_ref[r * Q_BLOCK:(r + 1) * Q_BLOCK, :] = jnp.concatenate([q_ref[0, r], msel], axis=1)
    qa = qa_ref[...]
    last = (q0 + Q_BLOCK - 1) // kb

    def logits(kt):
        k0 = pl.multiple_of(kt * kb, kb)
        return _dot_nt(qa, k_ref[0, 0, pl.ds(k0, kb), :])

    def v_tile(kt):
        return v_ref[0, 0, pl.ds(pl.multiple_of(kt * kb, kb), kb), :]

    reach = _query_norm(qa) * kmax_ref[pl.program_id(0) * N_NSA_KV + g]
    slope = qa[:, HEAD_DIM:HEAD_DIM + 1].astype(F32)
    row, col = _stacked_rows(rows, kb)
    causal = (last * kb + col) <= (q0 + row)
    sa_ref[...] = jnp.where(causal, logits(last), NEG)

    def margin(top, m):
        last_pos = (jnp.maximum(top, 0) * kb + kb - 1).astype(F32)
        return jnp.max(reach + slope * last_pos - m)

    def cond(state):
        top, ahead, _ = state
        return (top >= 0) & (ahead > -SKIP_T)

    def body(state):
        top, _, carry = state
        ahead = margin(top - 2, carry[0])
        lower = jnp.maximum(top - 1, 0)
        sb_ref[...] = logits(lower) + jnp.where(top >= 1, 0.0, NEG)
        carry = _flash_update(carry, sa_ref[...], v_tile(top))
        sa_ref[...] = logits(jnp.maximum(top - 2, 0))
        return top - 2, ahead, _flash_update(carry, sb_ref[...], v_tile(lower))

    _, _, (_, acc) = lax.while_loop(cond, body, (last, jnp.float32(0.0), _flash_init(rows)))
    o_ref[0] = _nsa_out(acc, _nsa_gates(sm_ref[0], 1, g))


def _nsa_sel(q2, msel, k2, v2, small, kmax, kb):
    B, H, S, qw = q2.shape
    G = N_NSA_KV
    nq = S // Q_BLOCK
    nb = msel.shape[-1]
    grid_spec = pltpu.PrefetchScalarGridSpec(
        num_scalar_prefetch=1,
        grid=(B, G, nq),
        in_specs=[pl.BlockSpec((1, NSA_REP, Q_BLOCK, qw), lambda b, g, i, km: (b, g, i, 0)),
                  pl.BlockSpec((1, 1, 1, Q_BLOCK, nb), lambda b, g, i, km: (b, g, i, 0, 0)),
                  pl.BlockSpec((1, 1, S, k2.shape[-1]), lambda b, g, i, km: (b, g, 0, 0)),
                  pl.BlockSpec((1, 1, S, 2 * LANES), lambda b, g, i, km: (b, g, 0, 0)),
                  pl.BlockSpec((1, Q_BLOCK, LANES), lambda b, g, i, km: (b, i, 0))],
        out_specs=pl.BlockSpec((1, Q_BLOCK, 2 * LANES), lambda b, g, i, km: (b, i, g)),
        scratch_shapes=[pltpu.VMEM((NSA_REP * Q_BLOCK, qw + nb), BF16),
                        pltpu.VMEM((NSA_REP * Q_BLOCK, kb), F32),
                        pltpu.VMEM((NSA_REP * Q_BLOCK, kb), F32)],
    )
    return pl.pallas_call(
        functools.partial(_nsa_sel_kernel, kb=kb),
        grid_spec=grid_spec,
        out_shape=jax.ShapeDtypeStruct((B, S, D_NSA), F32),
        compiler_params=_cparams(("parallel", "parallel", "parallel")),
        name="nsa_selected",
    )(kmax, q2, msel, k2, v2, small)


def _nsa_win_kernel(q_ref, k_ref, v_ref, sm_ref, o_ref):
    g = pl.program_id(1)
    i = pl.program_id(2)
    kb = Q_BLOCK
    rows = NSA_REP * Q_BLOCK
    qa = q_ref[0].reshape(rows, q_ref.shape[-1])
    n_back = WINDOW // kb
    row, col = _stacked_rows(rows, kb)
    carry = _flash_init(rows)
    for d in range(n_back + 1):
        kt = i - n_back + d
        k0 = pl.multiple_of(jnp.maximum(kt, 0) * kb, kb)
        s = _dot_nt(qa, k_ref[0, 0, pl.ds(k0, kb), :])
        if d == 0:
            s = jnp.where((col > row) & (kt >= 0), s, NEG)
        elif d == n_back:
            s = jnp.where(col <= row, s, NEG)
        else:
            s = jnp.where(kt >= 0, s, NEG)
        carry = _flash_update(carry, s, v_ref[0, 0, pl.ds(k0, kb), :])
    o_ref[0] = _nsa_out(carry[1], _nsa_gates(sm_ref[0], 2, g))


def _nsa_win(q2, k2, v2, small):
    B, H, S, qw = q2.shape
    G = N_NSA_KV
    nq = S // Q_BLOCK
    return pl.pallas_call(
        _nsa_win_kernel,
        grid=(B, G, nq),
        in_specs=[pl.BlockSpec((1, NSA_REP, Q_BLOCK, qw), lambda b, g, i: (b, g, i, 0)),
                  pl.BlockSpec((1, 1, S, qw), lambda b, g, i: (b, g, 0, 0)),
                  pl.BlockSpec((1, 1, S, 2 * LANES), lambda b, g, i: (b, g, 0, 0)),
                  pl.BlockSpec((1, Q_BLOCK, LANES), lambda b, g, i: (b, i, 0))],
        out_specs=pl.BlockSpec((1, Q_BLOCK, 2 * LANES), lambda b, g, i: (b, i, g)),
        out_shape=jax.ShapeDtypeStruct((B, S, D_NSA), F32),
        compiler_params=_cparams(("parallel", "parallel", "parallel")),
        name="nsa_window",
    )(q2, k2, v2, small)


def _fox_update(carry, s, v_tile):
    m, acc = carry
    m_new = jnp.maximum(m, jnp.max(s, axis=1, keepdims=True))
    p = jnp.exp(s - m_new).astype(BF16)
    return m_new, jnp.exp(m - m_new) * acc + _dot(p, v_tile)


def _query_norm(q):
    qf = q.astype(F32)
    lane = lax.broadcasted_iota(jnp.int32, qf.shape, 1)
    return jnp.sqrt(jnp.sum(jnp.where(lane < HEAD_DIM, qf * qf, 0.0), axis=1, keepdims=True))


def _fox_kernel(kmax_ref, tb_ref, q_ref, k_ref, v_ref, o_ref, *s_refs, tq, kb, n_heads, n_tiles):
    b = pl.program_id(0)
    hp = pl.program_id(1)
    i = pl.program_id(2)
    heads = [b * n_heads + hp * 2 + h for h in range(2)]

    def logits(h, kt):
        k0 = pl.multiple_of(kt * kb, kb)
        return _dot_nt(q_ref[0, h], k_ref[0, h, pl.ds(k0, kb), :])

    def v_tile(h, kt):
        return v_ref[0, h, pl.ds(pl.multiple_of(kt * kb, kb), kb), :]

    row = lax.broadcasted_iota(jnp.int32, (tq, kb), 0)
    col = lax.broadcasted_iota(jnp.int32, (tq, kb), 1)
    reach, carries = [], []
    for h in range(2):
        reach.append(_query_norm(q_ref[0, h]) * kmax_ref[heads[h]])
        c = _fox_update(_flash_init(tq), jnp.where(col <= row, logits(h, 2 * i), NEG), v_tile(h, 2 * i))
        carries.append(_fox_update(c, jnp.where(kb + col <= row, logits(h, 2 * i + 1), NEG), v_tile(h, 2 * i + 1)))
        s_refs[2 * h][...] = logits(h, jnp.maximum(2 * i - 1, 0))

    def margin(top, carries):
        t = jnp.maximum(top, 0)
        a = jnp.max(reach[0] + tb_ref[heads[0] * n_tiles + t] - carries[0][0])
        return jnp.maximum(a, jnp.max(reach[1] + tb_ref[heads[1] * n_tiles + t] - carries[1][0]))

    def cond(state):
        top, ahead, _ = state
        return (top >= 0) & (ahead > -SKIP_T)

    def body(state):
        top, _, carries = state
        ahead = margin(top - 2, carries)
        out = []
        for h in range(2):
            sa_ref, sb_ref = s_refs[2 * h], s_refs[2 * h + 1]
            sb_ref[...] = logits(h, top - 1)
            c = _fox_update(carries[h], sa_ref[...], v_tile(h, top))
            sa_ref[...] = logits(h, jnp.maximum(top - 2, 0))
            out.append(_fox_update(c, sb_ref[...], v_tile(h, top - 1)))
        return top - 2, ahead, tuple(out)

    _, _, carries = lax.while_loop(cond, body, (2 * i - 1, margin(2 * i - 1, carries), tuple(carries)))
    o_ref[0] = _pair_out(carries[0][1], carries[1][1])


def _fox(q2, k2, v2, kmax, tile_bias):
    B, H, S, qw = q2.shape
    tq, kb = min(FOX_TQ, S), min(FOX_KB, S)
    assert tq == 2 * kb
    grid_spec = pltpu.PrefetchScalarGridSpec(
        num_scalar_prefetch=2,
        grid=(B, H // 2, S // tq),
        in_specs=[pl.BlockSpec((1, 2, tq, qw), lambda b, h, i, km, tb: (b, h, i, 0)),
                  pl.BlockSpec((1, 2, S, qw), lambda b, h, i, km, tb: (b, h, 0, 0)),
                  pl.BlockSpec((1, 2, S, LANES), lambda b, h, i, km, tb: (b, h, 0, 0))],
        out_specs=pl.BlockSpec((1, tq, LANES), lambda b, h, i, km, tb: (b, i, h)),
        scratch_shapes=[pltpu.VMEM((tq, kb), F32)] * 4,
    )
    return pl.pallas_call(
        functools.partial(_fox_kernel, tq=tq, kb=kb, n_heads=H, n_tiles=S // kb),
        grid_spec=grid_spec,
        out_shape=jax.ShapeDtypeStruct((B, S, D_FOX), F32),
        compiler_params=_cparams(("parallel", "parallel", "parallel")),
        name="fox_attention",
    )(kmax, tile_bias, q2, k2, v2)


def _logf_cumsum_kernel(x_ref, c_ref):
    x = x_ref[0]
    nr, nl = x.shape
    log_f = jnp.minimum(x, 0.0) - jnp.log1p(jnp.exp(-jnp.abs(x)))
    hp = lax.Precision.HIGHEST
    incl = (lax.broadcasted_iota(jnp.int32, (nl, nl), 0) <= lax.broadcasted_iota(jnp.int32, (nl, nl), 1))
    within = jnp.dot(log_f, incl.astype(F32), preferred_element_type=F32, precision=hp)
    totals = jnp.broadcast_to(within[:, nl - 1:nl], (nr, nl))
    before = (lax.broadcasted_iota(jnp.int32, (nr, nr), 1) < lax.broadcasted_iota(jnp.int32, (nr, nr), 0))
    c_ref[0] = within + jnp.dot(before.astype(F32), totals, preferred_element_type=F32, precision=hp)


def _logf_cumsum(logits):
    B, H, S = logits.shape
    x = logits.reshape(B * H, S // LANES, LANES)
    c = pl.pallas_call(
        _logf_cumsum_kernel,
        grid=(B * H,),
        in_specs=[pl.BlockSpec((1, S // LANES, LANES), lambda i: (i, 0, 0))],
        out_specs=pl.BlockSpec((1, S // LANES, LANES), lambda i: (i, 0, 0)),
        out_shape=jax.ShapeDtypeStruct(x.shape, F32),
        compiler_params=_cparams(("parallel",)),
        name="logf_cumsum",
    )(x)
    return c.reshape(B, H, S)


def _sb_kernel(q_ref, k_ref, v_ref, o_ref, *, tq, nh):
    i = pl.program_id(1)
    q0 = i * tq
    trow = q0 + lax.broadcasted_iota(jnp.int32, (tq, tq), 0)
    col = lax.broadcasted_iota(jnp.int32, (tq, tq), 1)
    rr = lax.broadcasted_iota(jnp.int32, (tq, tq), 0)
    upper = (rr > col).astype(BF16)

    def cond(state):
        kt, carries, _ = state
        alive = jnp.max(carries[0])
        for h in range(1, nh):
            alive = jnp.maximum(alive, jnp.max(carries[h]))
        return (kt >= 0) & (alive > SB_CUTOFF)

    def body(state):
        kt, carries, accs = state
        k0 = pl.multiple_of(kt * tq, tq)
        strict = (k0 + col) < trow
        new_c, new_a = [], []
        for h in range(nh):
            z = _dot_nt(q_ref[0, h], k_ref[0, h, pl.ds(k0, tq), :])
            log_beta = jnp.minimum(z, 0.0) - jnp.log1p(jnp.exp(-jnp.abs(z)))
            log_keep = jnp.where(strict, log_beta - z, 0.0)
            hi, lo = _split_bf16(log_keep)
            later = _dot(hi, upper) + _dot(lo, upper)
            a = jnp.where(strict, jnp.exp(log_beta + later + carries[h]), 0.0)
            new_a.append(accs[h] + _dot(a.astype(BF16), v_ref[0, h, pl.ds(k0, tq), :]))
            new_c.append(carries[h] + jnp.sum(log_keep, axis=1, keepdims=True))
        return kt - 1, tuple(new_c), tuple(new_a)

    state = (i, tuple(jnp.zeros((tq, 1), F32) for _ in range(nh)),
             tuple(jnp.zeros((tq, LANES), F32) for _ in range(nh)))
    _, _, accs = lax.while_loop(cond, body, state)
    o_ref[0] = jnp.concatenate([accs[2 * j] + accs[2 * j + 1] for j in range(nh // 2)], axis=1)


def _sb(q2, k2, v2):
    B, H, S, w = q2.shape
    tq = min(SB_TQ, S)
    return pl.pallas_call(
        functools.partial(_sb_kernel, tq=tq, nh=H),
        grid=(B, S // tq),
        in_specs=[pl.BlockSpec((1, H, tq, w), lambda b, i: (b, 0, i, 0)),
                  pl.BlockSpec((1, H, S, w), lambda b, i: (b, 0, 0, 0)),
                  pl.BlockSpec((1, H, S, w), lambda b, i: (b, 0, 0, 0))],
        out_specs=pl.BlockSpec((1, tq, D_SB), lambda b, i: (b, i, 0)),
        out_shape=jax.ShapeDtypeStruct((B, S, D_SB), F32),
        compiler_params=_cparams(("parallel", "parallel")),
        name="sb_attention",
    )(q2, k2, v2)


def _layer_norm(y, g, b):
    mu = jnp.mean(y, axis=1, keepdims=True)
    d = y - mu
    var = jnp.mean(d * d, axis=1, keepdims=True)
    return d * lax.rsqrt(var + LN_EPS) * g + b


def _rms(o, g):
    return o * lax.rsqrt(jnp.mean(o * o, axis=1, keepdims=True) + RMS_EPS) * g


def _post_kernel(oc_ref, osel_ref, ow_ref, of_ref, os_ref, x_ref, go_ref, wo_ref, lg_ref, lb_ref, rw_ref, rb_ref,
                 x1_ref, te_ref, tw_ref, *, alpha):
    o_nsa = oc_ref[...] + osel_ref[...] + ow_ref[...]
    n1 = _rms(o_nsa, go_ref[:, 0:D_NSA]).astype(BF16)
    n2 = _rms(of_ref[...], go_ref[:, D_NSA:D_NSA + D_FOX]).astype(BF16)
    n3 = _rms(os_ref[...], go_ref[:, D_NSA + D_FOX:]).astype(BF16)
    mix = (_dot(n1, wo_ref[0:D_NSA, :]) + _dot(n2, wo_ref[D_NSA:D_NSA + D_FOX, :])
           + _dot(n3, wo_ref[D_NSA + D_FOX:, :]))
    x1 = _layer_norm(alpha * x_ref[...] + mix, lg_ref[...], lb_ref[...])
    x1_ref[...] = x1
    logits = jnp.dot(x1, rw_ref[...], preferred_element_type=F32, precision=lax.Precision.HIGHEST) + rb_ref[...]
    tm, ne = logits.shape
    lane = lax.broadcasted_iota(jnp.int32, (tm, ne), 1).astype(F32)
    wide = lax.broadcasted_iota(jnp.int32, (tm, LANES), 1)
    top_e = jnp.zeros((tm, LANES), F32)
    top_l = jnp.full((tm, LANES), NEG, F32)
    cur = logits
    for k in range(TOP_K):
        mx = jnp.max(cur, axis=1, keepdims=True)
        idx = jnp.min(jnp.where(cur == mx, lane, float(ne)), axis=1, keepdims=True)
        top_e = jnp.where(wide == k, idx, top_e)
        top_l = jnp.where(wide == k, mx, top_l)
        cur = jnp.where(lane == idx, -BIG, cur)
    pe = jnp.exp(top_l - jnp.max(top_l, axis=1, keepdims=True))
    te_ref[...] = top_e.astype(jnp.int32)
    tw_ref[...] = pe / jnp.sum(pe, axis=1, keepdims=True)


def _post(o_c, o_s, o_w, o_fox, o_sb, x2d, g_out, w_out_bf, ln_g, ln_b, router_w, router_b, alpha):
    T, D = x2d.shape
    tm = POST_TM
    row = lambda w: pl.BlockSpec((tm, w), lambda i: (i, 0))
    full = lambda shape: pl.BlockSpec(shape, lambda i: (0, 0))
    return pl.pallas_call(
        functools.partial(_post_kernel, alpha=alpha),
        grid=(T // tm,),
        in_specs=[row(D_NSA), row(D_NSA), row(D_NSA), row(D_FOX), row(D_SB), row(D), full((1, D)), full((D, D)),
                  full((1, D)), full((1, D)), full((D, N_EXPERTS)), full((1, N_EXPERTS))],
        out_specs=[row(D), row(LANES), row(LANES)],
        out_shape=[jax.ShapeDtypeStruct((T, D), F32), jax.ShapeDtypeStruct((T, LANES), jnp.int32),
                   jax.ShapeDtypeStruct((T, LANES), F32)],
        compiler_params=_cparams(("parallel",)),
        name="outproj_ln_router",
    )(o_c, o_s, o_w, o_fox, o_sb, x2d, g_out.reshape(1, D), w_out_bf, ln_g.reshape(1, D), ln_b.reshape(1, D),
      router_w, router_b.reshape(1, N_EXPERTS))


def _row_gather_start(src_hbm, dst, sem, idx_ref, n_rows):
    def issue(r8, c):
        for u in range(8):
            r = r8 * 8 + u
            pltpu.make_async_copy(src_hbm.at[pl.ds(idx_ref[0, 0, r], 1)], dst.at[pl.ds(r, 1)],
                                  sem).start(priority=u % 2)
        return c
    lax.fori_loop(0, n_rows // 8, issue, 0)


def _row_gather_wait(src_hbm, dst, sem, n_rows):
    pltpu.make_async_copy(src_hbm.at[pl.ds(0, n_rows)], dst, sem).wait()


def _expert_kernel(blk_e_ref, nused_ref, idx_ref, idxn_ref, x_hbm, w_ref, wgu_ref, bgu_ref, wdn_ref, bdn_ref,
                   y_ref, xbuf, wgu_bf, wdn_bf, sem, *, tm, d_ff):
    i = pl.program_id(0)
    n_used = nused_ref[0]
    slot = i % 2

    @pl.when(i == 0)
    def _():
        _row_gather_start(x_hbm, xbuf.at[0], sem.at[0], idx_ref, tm)

    @pl.when(i <= n_used)
    def _():
        _row_gather_wait(x_hbm, xbuf.at[slot], sem.at[slot], tm)

    @pl.when(i < n_used)
    def _():
        changed = (i == 0) | (blk_e_ref[i] != blk_e_ref[jnp.maximum(i - 1, 0)])

        @pl.when(changed)
        def _():
            wgu_bf[...] = wgu_ref[0, 0].astype(BF16)
            wdn_bf[...] = wdn_ref[0, 0].astype(BF16)

        nxt = xbuf.at[1 - slot]
        for r in range(tm):
            pltpu.make_async_copy(x_hbm.at[pl.ds(idxn_ref[0, 0, r], 1)], nxt.at[pl.ds(r, 1)],
                                  sem.at[1 - slot]).start(priority=r % 2)
        xb = xbuf[slot].astype(BF16)
        gu = _dot(xb, wgu_bf[...]) + bgu_ref[0, 0]
        gate = jnp.minimum(gu[:, :d_ff], SWIGLU_LIMIT)
        up = jnp.clip(gu[:, d_ff:], -SWIGLU_LIMIT, SWIGLU_LIMIT)
        act = gate * _sigmoid(SWIGLU_ALPHA * gate) * (up + 1.0)
        y = _dot(act.astype(BF16), wdn_bf[...]) + bdn_ref[0, 0]
        y_ref[...] = y * w_ref[0]

    @pl.when(i >= n_used)
    def _():
        y_ref[...] = jnp.zeros_like(y_ref)


def _experts(x1, blk_e, n_used, buf_tok, buf_w, w_gu, b_gu, w_dn, b_dn, layer, tm):
    T, D = x1.shape
    nl, ne, _, d2 = w_gu.shape
    d_ff = d2 // 2
    n_blocks = blk_e.shape[0]
    idx = buf_tok.reshape(n_blocks, 1, tm)
    wcol = buf_w.reshape(n_blocks, tm, 1)
    last = n_blocks - 1
    grid_spec = pltpu.PrefetchScalarGridSpec(
        num_scalar_prefetch=2,
        grid=(n_blocks,),
        in_specs=[pl.BlockSpec((1, 1, tm), lambda i, e, n: (i, 0, 0), memory_space=pltpu.SMEM),
                  pl.BlockSpec((1, 1, tm), lambda i, e, n: (jnp.minimum(i + 1, last), 0, 0),
                               memory_space=pltpu.SMEM),
                  pl.BlockSpec(memory_space=pl.ANY),
                  pl.BlockSpec((1, tm, 1), lambda i, e, n: (i, 0, 0)),
                  pl.BlockSpec((1, 1, D, d2), lambda i, e, n: (layer, e[i], 0, 0)),
                  pl.BlockSpec((1, 1, 1, d2), lambda i, e, n: (layer, e[i], 0, 0)),
                  pl.BlockSpec((1, 1, d_ff, D), lambda i, e, n: (layer, e[i], 0, 0)),
                  pl.BlockSpec((1, 1, 1, D), lambda i, e, n: (layer, e[i], 0, 0))],
        out_specs=pl.BlockSpec((tm, D), lambda i, e, n: (i, 0)),
        scratch_shapes=[pltpu.VMEM((2, tm, D), F32), pltpu.VMEM((D, d2), BF16), pltpu.VMEM((d_ff, D), BF16),
                        pltpu.SemaphoreType.DMA((2,))],
    )
    return pl.pallas_call(
        functools.partial(_expert_kernel, tm=tm, d_ff=d_ff),
        grid_spec=grid_spec,
        out_shape=jax.ShapeDtypeStruct((n_blocks * tm, D), F32),
        compiler_params=_cparams(("arbitrary",), 58 * 1024 * 1024),
        name="moe_experts",
    )(blk_e, n_used, idx, idx, x1, wcol, w_gu, b_gu.reshape(nl, ne, 1, d2), w_dn, b_dn.reshape(nl, ne, 1, D))


def _combine_kernel(idx_ref, idxn_ref, y_hbm, x_ref, lg_ref, lb_ref, o_ref, ybuf, sem, *, tm, alpha):
    i = pl.program_id(0)
    n = pl.num_programs(0)
    slot = i % 2
    rows = TOP_K * tm

    @pl.when(i == 0)
    def _():
        _row_gather_start(y_hbm, ybuf.at[0], sem.at[0], idx_ref, rows)

    @pl.when(i + 1 < n)
    def _():
        nxt = ybuf.at[1 - slot]
        for r in range(rows):
            pltpu.make_async_copy(y_hbm.at[pl.ds(idxn_ref[0, 0, r], 1)], nxt.at[pl.ds(r, 1)],
                                  sem.at[1 - slot]).start(priority=r % 2)

    _row_gather_wait(y_hbm, ybuf.at[slot], sem.at[slot], rows)
    ffn = ybuf[slot, 0:tm]
    for k in range(1, TOP_K):
        ffn = ffn + ybuf[slot, k * tm:(k + 1) * tm]
    o_ref[...] = _layer_norm(alpha * x_ref[...] + ffn, lg_ref[...], lb_ref[...])


def _combine(y_sorted, pos, x1, ln_g, ln_b, alpha):
    T, D = x1.shape
    tm = COMBINE_TM
    nt = T // tm
    idx = pos.reshape(nt, tm, TOP_K).transpose(0, 2, 1).reshape(nt, 1, TOP_K * tm)
    last = nt - 1
    return pl.pallas_call(
        functools.partial(_combine_kernel, tm=tm, alpha=alpha),
        grid=(nt,),
        in_specs=[pl.BlockSpec((1, 1, TOP_K * tm), lambda i: (i, 0, 0), memory_space=pltpu.SMEM),
                  pl.BlockSpec((1, 1, TOP_K * tm), lambda i: (jnp.minimum(i + 1, last), 0, 0),
                               memory_space=pltpu.SMEM),
                  pl.BlockSpec(memory_space=pl.ANY),
                  pl.BlockSpec((tm, D), lambda i: (i, 0)),
                  pl.BlockSpec((1, D), lambda i: (0, 0)),
                  pl.BlockSpec((1, D), lambda i: (0, 0))],
        out_specs=pl.BlockSpec((tm, D), lambda i: (i, 0)),
        out_shape=jax.ShapeDtypeStruct((T, D), F32),
        scratch_shapes=[pltpu.VMEM((2, TOP_K * tm, D), F32), pltpu.SemaphoreType.DMA((2,))],
        compiler_params=_cparams(("arbitrary",)),
        name="moe_combine_ln",
    )(idx, idx, y_sorted, x1, ln_g.reshape(1, D), ln_b.reshape(1, D))


def _route(top_e, top_w, tm):
    T = top_e.shape[0]
    M = T * TOP_K
    i32 = jnp.int32
    flat_e = top_e.reshape(M).astype(i32)
    flat_w = top_w.reshape(M)
    ar = jnp.arange(M, dtype=i32)
    experts = jnp.arange(N_EXPERTS, dtype=i32)
    skey, sorted_w = lax.sort((flat_e * M + ar, flat_w), num_keys=1)
    sorted_e = skey // M
    order = skey - sorted_e * M
    counts = jnp.sum((flat_e[:, None] == experts[None, :]).astype(i32), axis=0)
    start = jnp.cumsum(counts) - counts
    padded = (counts + tm - 1) // tm * tm
    pad_end = jnp.cumsum(padded)
    pad_start = pad_end - padded
    dest = pad_start[sorted_e] + ar - start[sorted_e]
    n_blocks = -(-M // tm) + N_EXPERTS
    blk_first = jnp.arange(n_blocks, dtype=i32) * tm
    blk_e = jnp.minimum(jnp.sum((pad_end[None, :] <= blk_first[:, None]).astype(i32), axis=1), N_EXPERTS - 1)
    within = (blk_first - pad_start[blk_e])[:, None] + jnp.arange(tm, dtype=i32)[None, :]
    valid = within < counts[blk_e][:, None]
    src_row = jnp.clip(start[blk_e][:, None] + within, 0, M - 1)
    buf_tok = jnp.where(valid, order[src_row] // TOP_K, 0).reshape(n_blocks * tm)
    buf_w = jnp.where(valid, sorted_w[src_row], 0.0).reshape(n_blocks * tm)
    _, pos = lax.sort((order, dest), num_keys=1)
    n_used = (pad_end[-1] // tm).astype(i32).reshape(1)
    return blk_e, n_used, buf_tok, buf_w, pos


class _Plan:
    def __init__(self):
        self.src, self.scale, self.const, self.outs = [], [], [], []

    def group(self, width, src_cols=(), at=0, scale=1.0, ones=()):
        src = np.full((width,), -1, np.int64)
        sc = np.zeros((width,), np.float32)
        const = np.zeros((width,), np.float32)
        src[at:at + len(src_cols)] = src_cols
        sc[at:at + len(src_cols)] = scale
        for col, val in ones:
            const[col] = val
        self.src.append(src)
        self.scale.append(sc)
        self.const.append(const)

    def out(self, n_heads, width, dtype):
        self.outs.append((n_heads, width, dtype))


def _cols(start):
    return np.arange(start, start + HEAD_DIM)


def _value_group(plan, src, odd):
    if odd:
        plan.group(LANES, _cols(src), at=HALF, ones=[(0, 1.0)])
    else:
        plan.group(LANES, _cols(src), ones=[(HALF, 1.0)])


def _plans(S):
    qs = HEAD_DIM ** -0.5
    slope = 2.0 ** (-8.0 * np.arange(1, N_NSA_HEADS + 1) / N_NSA_HEADS)
    a = _Plan()
    for g in range(N_NSA_KV):
        for hl in NSA_ROW_HEADS:
            h = g * NSA_REP + hl
            a.group(LANES, _cols(h * HEAD_DIM), scale=qs, ones=[(HALF, slope[h]), (HALF + 1, slope[h])])
    a.out(N_NSA_HEADS, LANES, BF16)
    kv = lambda branch, which, g: SRC_KV + branch * 256 + which * 128 + g * HEAD_DIM
    for which in range(2):
        for g in range(N_NSA_KV):
            a.group(LANES, _cols(kv(0, which, g)))
    a.out(2 * N_NSA_KV, LANES, F32)
    for g in range(N_NSA_KV):
        a.group(2 * LANES, _cols(kv(1, 0, g)))
    a.out(N_NSA_KV, 2 * LANES, BF16)
    for g in range(N_NSA_KV):
        _value_group(a, kv(1, 1, g), False)
        _value_group(a, kv(1, 1, g), True)
    a.out(N_NSA_KV, 2 * LANES, BF16)
    for g in range(N_NSA_KV):
        a.group(LANES, _cols(kv(2, 0, g)))
    a.out(N_NSA_KV, LANES, BF16)
    for g in range(N_NSA_KV):
        _value_group(a, kv(2, 1, g), False)
        _value_group(a, kv(2, 1, g), True)
    a.out(N_NSA_KV, 2 * LANES, BF16)
    a.group(LANES, np.concatenate([np.arange(SRC_GATE, SRC_GATE + 24), np.arange(SRC_LOGF, SRC_LOGF + 4)]))
    a.out(1, LANES, F32)

    n_a = sum(len(s) for s in a.src)
    extra = np.zeros((S, n_a), np.float32)
    pos = np.arange(S)
    pos_hi, pos_lo = pos // SEL_LEN * SEL_LEN, pos % SEL_LEN
    off_ksel = N_NSA_HEADS * LANES + 2 * N_NSA_KV * LANES
    off_kwin = off_ksel + N_NSA_KV * 2 * LANES + N_NSA_KV * 2 * LANES
    nb = S // SEL_LEN
    for g in range(N_NSA_KV):
        o = off_ksel + g * 2 * LANES
        extra[:, o + HALF], extra[:, o + HALF + 1] = pos_hi, pos_lo
        extra[pos, o + LANES + pos // SEL_LEN] = 1.0
        o = off_kwin + g * LANES
        extra[:, o + HALF], extra[:, o + HALF + 1] = pos_hi, pos_lo
    assert nb <= LANES

    b = _Plan()
    fox = lambda which, h: SRC_FOX + which * 256 + h * HEAD_DIM
    sb = lambda which, h: SRC_SB + which * 256 + h * HEAD_DIM
    for h in range(N_FOX_HEADS):
        b.group(LANES, _cols(fox(0, h)), scale=qs, ones=[(HALF, 1.0), (HALF + 1, 1.0), (HALF + 2, 1.0)])
    b.out(N_FOX_HEADS, LANES, BF16)
    for h in range(N_FOX_HEADS):
        b.group(LANES, _cols(fox(1, h)))
    b.out(N_FOX_HEADS, LANES, BF16)
    for h in range(N_FOX_HEADS):
        _value_group(b, fox(2, h), h % 2 == 1)
    b.out(N_FOX_HEADS, LANES, BF16)
    for h in range(N_SB_HEADS):
        b.group(LANES, _cols(sb(0, h)), scale=qs)
    b.out(N_SB_HEADS, LANES, BF16)
    for h in range(N_SB_HEADS):
        b.group(LANES, _cols(sb(1, h)))
    b.out(N_SB_HEADS, LANES, BF16)
    for h in range(N_SB_HEADS):
        b.group(LANES, _cols(sb(2, h)), at=HALF if h % 2 else 0)
    b.out(N_SB_HEADS, LANES, BF16)
    return a, jnp.asarray(extra, dtype=BF16), b


def _constants(S):
    plan_a, extra_a, plan_b = _plans(S)
    nc = S // CMP_STRIDE
    nb = S // SEL_LEN
    n_cmp = (S - CMP_LEN) // CMP_STRIDE + 1
    cmp_idx = np.arange(nc)[:, None] * CMP_STRIDE + np.arange(CMP_LEN)[None, :]
    ovl = (cmp_idx[:, :, None] // SEL_LEN == np.arange(nb)[None, None, :]).astype(np.float32).mean(axis=1)
    ovl[n_cmp:] = 0.0
    ovl = np.pad(ovl, ((0, 0), (0, LANES - nb)))
    cmp_end = np.arange(nc) * CMP_STRIDE + CMP_LEN - 1
    kcmp_aug = np.zeros((nc, LANES), np.float32)
    kcmp_aug[:, HALF] = cmp_end // SEL_LEN * SEL_LEN
    kcmp_aug[:, HALF + 1] = cmp_end % SEL_LEN
    return dict(a=plan_a, a_outs=tuple(plan_a.outs), extra_a=extra_a, b=plan_b, b_outs=tuple(plan_b.outs),
                ovl_t=jnp.asarray(ovl.T, dtype=BF16), kcmp_aug=jnp.asarray(kcmp_aug))


def _plan_weights(w_in, b_in, plan):
    src = np.concatenate(plan.src)
    scale = np.concatenate(plan.scale)
    const = jnp.asarray(np.concatenate(plan.const))
    wb = jnp.concatenate([w_in, b_in[None, :]], axis=0)
    pieces, c = [], 0
    while c < len(src):
        e = c + 1
        if src[c] < 0:
            while e < len(src) and src[e] < 0:
                e += 1
            pieces.append(jnp.zeros((wb.shape[0], e - c), F32))
        else:
            while e < len(src) and src[e] == src[e - 1] + 1 and scale[e] == scale[c]:
                e += 1
            pieces.append(wb[:, src[c]:src[c] + e - c] * float(scale[c]))
        c = e
    wb_p = jnp.concatenate(pieces, axis=1)
    return wb_p[:-1].astype(BF16), (wb_p[-1] + const).reshape(1, -1)


def _key_norm_max(k2):
    kf = k2[..., 0:HEAD_DIM].astype(F32)
    return jnp.sqrt(jnp.max(jnp.sum(kf * kf, axis=-1), axis=-1)).reshape(-1)


def _layer(x, cs, w_in, b_in, pos_k, pos_v, w1k, w2k, w1v, w2v, g_out, w_out, ln1_g, ln1_b,
           router_w, router_b, w_gu, b_gu, w_dn, b_dn, ln2_g, ln2_b, alpha, layer):
    B, S, D = x.shape
    T = B * S
    x2d = x.reshape(T, D)
    wa, ba = _plan_weights(w_in, b_in, cs["a"])
    wb, bb = _plan_weights(w_in, b_in, cs["b"])
    q2, kvc, ksel, vsel, kwin, vwin, small = _proj(x2d, wa, ba, cs["extra_a"], cs["a_outs"], B, S)
    fq, fk, fv, sq, sk, sv = _proj(x2d, wb, bb, None, cs["b_outs"], B, S)
    small = small.reshape(B, S, LANES)

    kc = kvc[:, 0:N_NSA_KV, :, 0:HEAD_DIM]
    vc = kvc[:, N_NSA_KV:, :, 0:HEAD_DIM]
    k_cmp, v_cmp = _compress(kc, vc, pos_k, pos_v, w1k, w2k, w1v, w2v, cs["kcmp_aug"])
    o_c, msel = _nsa_cmp(q2, k_cmp, v_cmp, cs["ovl_t"], small)
    o_s = _nsa_sel(q2, msel, ksel, vsel, small, _key_norm_max(ksel), min(SEL_KB, S))
    o_w = _nsa_win(q2, kwin, vwin, small)

    c = _logf_cumsum(small[..., 24:24 + N_FOX_HEADS].transpose(0, 2, 1))
    c_hi = _trunc_bf16(c)
    c_mid = _trunc_bf16(c - c_hi)
    c_lo = c - c_hi - c_mid
    c_aug = jnp.pad(-jnp.stack([c_hi, c_mid, c_lo], axis=-1), ((0, 0), (0, 0), (0, 0), (HALF, HALF - 3)))
    kb = min(FOX_KB, S)
    tile_bias = lax.cummax(jnp.max((-c).reshape(B, N_FOX_HEADS, S // kb, kb), axis=-1), axis=2)
    o_fox = _fox(fq, fk + c_aug.astype(BF16), fv, _key_norm_max(fk), tile_bias.reshape(-1))
    o_sb = _sb(sq, sk, sv)

    flat = lambda o: o.reshape(T, o.shape[-1])
    x1, te, tw = _post(flat(o_c), flat(o_s), flat(o_w), flat(o_fox), flat(o_sb), x2d, g_out, w_out.astype(BF16),
                       ln1_g, ln1_b, router_w, router_b, alpha)
    blk_e, n_used, buf_tok, buf_w, pos = _route(te[:, :TOP_K], tw[:, :TOP_K], MOE_TM)
    y_sorted = _experts(x1, blk_e, n_used, buf_tok, buf_w, w_gu, b_gu, w_dn, b_dn, layer, MOE_TM)
    out = _combine(y_sorted, pos, x1, ln2_g, ln2_b, alpha)
    return out.reshape(B, S, D)


def kernel(x, w_in, b_in, cmp_pos_k, cmp_pos_v, cmp_w1_k, cmp_w2_k, cmp_w1_v, cmp_w2_v, g_out, w_out,
           ln1_g, ln1_b, router_w, router_b, w_gate_up, b_gate_up, w_down, b_down, ln2_g, ln2_b):
    depth = w_in.shape[0]
    alpha = (2 * depth) ** 0.25
    consts = _constants(x.shape[1])
    for l in range(depth):
        x = _layer(x, consts, w_in[l], b_in[l], cmp_pos_k[l], cmp_pos_v[l], cmp_w1_k[l], cmp_w2_k[l],
                   cmp_w1_v[l], cmp_w2_v[l], g_out[l], w_out[l], ln1_g[l], ln1_b[l], router_w[l], router_b[l],
                   w_gate_up, b_gate_up, w_down, b_down, ln2_g[l], ln2_b[l], alpha, l)
    return x
```

```python
import functools

import jax
import jax.numpy as jnp
import numpy as np
from jax import lax
from jax.experimental import pallas as pl
from jax.experimental.pallas import tpu as pltpu

F32 = jnp.float32
BF16 = jnp.bfloat16

HEAD_DIM = 64
N_NSA_HEADS = 8
N_NSA_KV = 2
NSA_REP = 4
N_FOX_HEADS = 4
N_SB_HEADS = 4
D_NSA = 512
D_FOX = 256
D_SB = 256
CMP_LEN = 32
CMP_STRIDE = 16
CMP_HIDDEN = 128
SEL_LEN = 64
SEL_TOPK = 16
WINDOW = 512
Q_BLOCK = 128
N_EXPERTS = 32
TOP_K = 4
SWIGLU_LIMIT = 7.0
SWIGLU_ALPHA = 1.702
LN_EPS = 1e-5
RMS_EPS = 1e-6
NEG = -1e30
BIG = 3e38
SB_CUTOFF = -90.0
SKIP_T = 100.0
LOG2E = 1.4426950408889634
SKIP_T2 = SKIP_T * LOG2E


def _bf16_parts(x, n=3):
    parts = []
    for _ in range(n - 1):
        p = (np.float32(x).view(np.uint32) & np.uint32(0xFFFF0000)).view(np.float32)
        parts.append(float(p))
        x = x - float(p)
    parts.append(float((np.float32(x).view(np.uint32) & np.uint32(0xFFFF0000)).view(np.float32)))
    return parts


LOG2E_PARTS = _bf16_parts(LOG2E)

LANES = 128
HALF = LANES // 2
NSA_ROW_HEADS = (0, 2, 1, 3)

SRC_KV = 512
SRC_GATE = SRC_KV + 768
SRC_FOX = SRC_GATE + 24
SRC_LOGF = SRC_FOX + 768
SRC_SB = SRC_LOGF + 4

VMEM_LIMIT = 52 * 1024 * 1024

PROJ_TM = 512
CMP_SUB = 4
SEL_KB = 256
FOX_TQ = 512
FOX_KB = 256
SB_TQ = 128
POST_TM = 256
MOE_TM = 256
COMBINE_TM = 128


def _cparams(sem, vmem=VMEM_LIMIT):
    return pltpu.CompilerParams(dimension_semantics=sem, vmem_limit_bytes=vmem)


def _dot(a, b):
    return jnp.dot(a, b, preferred_element_type=F32)


def _dot_nt(a, b):
    return lax.dot_general(a, b, (((1,), (1,)), ((), ())), preferred_element_type=F32)


def _split_bf16(x):
    hi = x.astype(BF16)
    lo = (x - hi.astype(F32)).astype(BF16)
    return hi, lo


def _trunc_bf16(x):
    bits = lax.bitcast_convert_type(x, jnp.uint32) & jnp.uint32(0xFFFF0000)
    return lax.bitcast_convert_type(bits, F32)


def _sigmoid(x):
    return 1.0 / (1.0 + jnp.exp(-x))


def _lane_column(tile, idx):
    lane = lax.broadcasted_iota(jnp.int32, tile.shape, 1)
    return jnp.sum(jnp.where(lane == idx, tile, 0.0), axis=1, keepdims=True)


def _low_half(shape):
    return lax.broadcasted_iota(jnp.int32, shape, 1) < HALF


def _proj_kernel(*refs, outs, has_extra):
    x_ref, w_ref, b_ref = refs[:3]
    o_refs = refs[3 + has_extra:]
    y = _dot(x_ref[...].astype(BF16), w_ref[...]) + b_ref[...]
    if has_extra:
        y = y + refs[3][...].astype(F32)
    off = 0
    for o_ref, (nh, width, _) in zip(o_refs, outs):
        for h in range(nh):
            o_ref[0, h] = y[:, off:off + width].astype(o_ref.dtype)
            off += width


def _proj(x2d, w_bf, b_row, extra, outs, B, S):
    T, D = x2d.shape
    N = w_bf.shape[1]
    tm = min(PROJ_TM, S)
    spb = S // tm
    in_specs = [pl.BlockSpec((tm, D), lambda i: (i, 0)),
                pl.BlockSpec((D, N), lambda i: (0, 0)),
                pl.BlockSpec((1, N), lambda i: (0, 0))]
    args = [x2d, w_bf, b_row]
    if extra is not None:
        in_specs.append(pl.BlockSpec((tm, N), lambda i: (i % spb, 0)))
        args.append(extra)
    return pl.pallas_call(
        functools.partial(_proj_kernel, outs=outs, has_extra=extra is not None),
        grid=(T // tm,),
        in_specs=in_specs,
        out_specs=[pl.BlockSpec((1, nh, tm, w), lambda i: (i // spb, 0, i % spb, 0)) for nh, w, _ in outs],
        out_shape=[jax.ShapeDtypeStruct((B, nh, S, w), dt) for nh, w, dt in outs],
        compiler_params=_cparams(("parallel",)),
        name="proj",
    )(*args)


def _gelu_tanh(x):
    return 0.5 * x * (1.0 + jnp.tanh(0.7978845608028654 * (x + 0.044715 * (x * x * x))))


def _compress_kernel(k_ref, v_ref, pk_ref, pv_ref, w1k_ref, w2k_ref, w1v_ref, w2v_ref, ka_ref, ok_ref, ov_ref):
    nc = k_ref.shape[2]
    half = CMP_STRIDE * HEAD_DIM

    def one(r_ref, p_ref, w1_ref, w2_ref):
        r = r_ref[0, 0]
        a = (r + p_ref[0:1, :]).astype(BF16)
        b = (r + p_ref[1:2, :]).astype(BF16)
        ha = _dot(a, w1_ref[0:half, :])
        hb = _dot(b, w1_ref[half:2 * half, :])
        hid = _gelu_tanh(ha + pltpu.roll(hb, nc - 1, 0))
        return _dot(hid.astype(BF16), w2_ref[...])

    ok_ref[0, 0] = (one(k_ref, pk_ref, w1k_ref, w2k_ref) + ka_ref[...]).astype(ok_ref.dtype)
    ov_ref[0, 0] = one(v_ref, pv_ref, w1v_ref, w2v_ref).astype(ov_ref.dtype)


def _compress(kc, vc, pos_k, pos_v, w1k, w2k, w1v, w2v, kcmp_aug):
    B, G, S, dh = kc.shape
    nc = S // CMP_STRIDE
    width = CMP_STRIDE * dh
    kr = kc.reshape(B, G, nc, width)
    vr = vc.reshape(B, G, nc, width)
    pk = pos_k.reshape(2, width)
    pv = pos_v.reshape(2, width)
    w2k_p = jnp.pad(w2k, ((0, 0), (0, LANES - dh))).astype(BF16)
    w2v_p = jnp.concatenate([jnp.pad(w2v, ((0, 0), (0, LANES - dh))), jnp.pad(w2v, ((0, 0), (LANES - dh, 0)))],
                            axis=1).astype(BF16)
    kv_spec = pl.BlockSpec((1, 1, nc, width), lambda b, g: (b, g, 0, 0))
    full = lambda shape: pl.BlockSpec(shape, lambda b, g: tuple(0 for _ in shape))
    return pl.pallas_call(
        _compress_kernel,
        grid=(B, G),
        in_specs=[kv_spec, kv_spec, full((2, width)), full((2, width)),
                  full((2 * width, CMP_HIDDEN)), full((CMP_HIDDEN, LANES)),
                  full((2 * width, CMP_HIDDEN)), full((CMP_HIDDEN, 2 * LANES)), full((nc, LANES))],
        out_specs=[pl.BlockSpec((1, 1, nc, LANES), lambda b, g: (b, g, 0, 0)),
                   pl.BlockSpec((1, 1, nc, 2 * LANES), lambda b, g: (b, g, 0, 0))],
        out_shape=[jax.ShapeDtypeStruct((B, G, nc, LANES), BF16),
                   jax.ShapeDtypeStruct((B, G, nc, 2 * LANES), BF16)],
        compiler_params=_cparams(("parallel", "parallel")),
        name="nsa_compress",
    )(kr, vr, pk, pv, w1k.astype(BF16), w2k_p, w1v.astype(BF16), w2v_p, kcmp_aug)


def _nsa_gates(small, branch, g):
    return [_sigmoid(_lane_column(small, branch * N_NSA_HEADS + g * NSA_REP + h)) for h in NSA_ROW_HEADS]


def _nsa_cmp_kernel(q_ref, kc_ref, vc_ref, ovl_ref, sm_ref, o_ref, sel_ref, *, n_sel, n_sub):
    g = pl.program_id(1)
    i = pl.program_id(2)
    results = [_nsa_cmp_block(q_ref, kc_ref, vc_ref, ovl_ref, sm_ref, g, i * n_sub + u, u, n_sel)
               for u in range(n_sub)]
    for u, (o, msel) in enumerate(results):
        o_ref[0, u * Q_BLOCK:(u + 1) * Q_BLOCK, :] = o
        sel_ref[0, 0, u] = msel


def _nsa_cmp_block(q_ref, kc_ref, vc_ref, ovl_ref, sm_ref, g, qb, u, n_sel):
    q0 = qb * Q_BLOCK
    rows = slice(u * Q_BLOCK, (u + 1) * Q_BLOCK)
    nc = kc_ref.shape[2]
    nb = ovl_ref.shape[0]
    kc = kc_ref[0, 0]
    vc = vc_ref[0, 0]
    gates = _nsa_gates(sm_ref[0, rows, :], 0, g)
    t = q0 + lax.broadcasted_iota(jnp.int32, (Q_BLOCK, nc), 0)
    cmp_end = lax.broadcasted_iota(jnp.int32, (Q_BLOCK, nc), 1) * CMP_STRIDE + (CMP_LEN - 1)
    mask = cmp_end <= t
    psum = jnp.zeros((Q_BLOCK, nc), F32)
    outs = []
    for r in range(NSA_REP):
        s = jnp.where(mask, _dot_nt(q_ref[0, r, rows, :], kc), NEG)
        m = jnp.max(s, axis=1, keepdims=True)
        p = jnp.where(mask, jnp.exp2(s - m), 0.0)
        l = jnp.sum(p, axis=1, keepdims=True)
        pn = p / jnp.where(l > 0.0, l, 1.0)
        v_half = vc[:, 0:LANES] if r < 2 else vc[:, LANES:2 * LANES]
        outs.append(_dot(pn.astype(BF16), v_half) * gates[r])
        psum = psum + pn
    o = jnp.concatenate([outs[0] + outs[2], outs[1] + outs[3]], axis=1)
    hi, lo = _split_bf16(psum)
    ovl = ovl_ref[...]
    imp = _dot_nt(ovl, hi) + _dot_nt(ovl, lo)
    jf = lax.broadcasted_iota(jnp.int32, (nb, Q_BLOCK), 0)
    tq = q0 + lax.broadcasted_iota(jnp.int32, (nb, Q_BLOCK), 1)
    cur = tq >> 6
    valid = jf <= cur
    forced = valid & ((jf == 0) | (jf >= cur - 1))
    v = jnp.where(valid & ~forced, imp, -BIG)
    jff = jf.astype(F32)
    picked = jnp.where(forced, 1.0, 0.0)
    for _ in range(max(n_sel - 3, 0)):
        mx = jnp.max(v, axis=0, keepdims=True)
        idx = jnp.min(jnp.where(v == mx, jff, float(nb)), axis=0, keepdims=True)
        hit = jff == idx
        picked = jnp.where(hit, 1.0, picked)
        v = jnp.where(hit, -BIG, v)
    msel = jnp.where(valid & (picked > 0.5), 0.0, NEG)
    return o, msel.T.astype(BF16)


def _nsa_cmp(q2, kcmp, vcmp, ovl_t, small):
    B, H, S, qw = q2.shape
    G = N_NSA_KV
    nq = S // Q_BLOCK
    nc = kcmp.shape[2]
    nb = ovl_t.shape[0]
    n_sel = min(SEL_TOPK, S // SEL_LEN)
    n_sub = CMP_SUB
    return pl.pallas_call(
        functools.partial(_nsa_cmp_kernel, n_sel=n_sel, n_sub=n_sub),
        grid=(B, G, nq // n_sub),
        in_specs=[pl.BlockSpec((1, NSA_REP, n_sub * Q_BLOCK, qw), lambda b, g, i: (b, g, i, 0)),
                  pl.BlockSpec((1, 1, nc, LANES), lambda b, g, i: (b, g, 0, 0)),
                  pl.BlockSpec((1, 1, nc, 2 * LANES), lambda b, g, i: (b, g, 0, 0)),
                  pl.BlockSpec((nb, nc), lambda b, g, i: (0, 0)),
                  pl.BlockSpec((1, n_sub * Q_BLOCK, LANES), lambda b, g, i: (b, i, 0))],
        out_specs=[pl.BlockSpec((1, n_sub * Q_BLOCK, 2 * LANES), lambda b, g, i: (b, i, g)),
                   pl.BlockSpec((1, 1, n_sub, Q_BLOCK, nb), lambda b, g, i: (b, g, i, 0, 0))],
        out_shape=[jax.ShapeDtypeStruct((B, S, D_NSA), F32),
                   jax.ShapeDtypeStruct((B, G, nq, Q_BLOCK, nb), BF16)],
        compiler_params=_cparams(("parallel", "parallel", "parallel")),
        name="nsa_cmp_select",
    )(q2, kcmp, vcmp, ovl_t, small)


def _flash_init(rows):
    return jnp.full((rows, 1), NEG, F32), jnp.zeros((rows, LANES), F32)


def _flash_update(carry, s, v_tile):
    m, acc = carry
    half = s.shape[0] // 2
    m_new = jnp.maximum(m, jnp.max(s, axis=1, keepdims=True))
    p = jnp.exp2(s - m_new).astype(BF16)
    pv = jnp.concatenate([_dot(p[:half], v_tile[:, 0:LANES]), _dot(p[half:], v_tile[:, LANES:2 * LANES])], axis=0)
    return m_new, jnp.exp2(m - m_new) * acc + pv


def _pair_out(acc_even, acc_odd):
    even = acc_even / acc_even[:, HALF:HALF + 1]
    odd = acc_odd / acc_odd[:, 0:1]
    return jnp.where(_low_half(acc_even.shape), even, odd)


def _nsa_out(acc, gates):
    c = [acc[r * Q_BLOCK:(r + 1) * Q_BLOCK] for r in range(NSA_REP)]
    low = _low_half((Q_BLOCK, LANES))
    pair_a = _pair_out(c[0], c[2]) * jnp.where(low, gates[0], gates[2])
    pair_b = _pair_out(c[1], c[3]) * jnp.where(low, gates[1], gates[3])
    return jnp.concatenate([pair_a, pair_b], axis=1)


def _stacked_rows(rows, kb):
    row = lax.broadcasted_iota(jnp.int32, (rows, kb), 0) & (Q_BLOCK - 1)
    col = lax.broadcasted_iota(jnp.int32, (rows, kb), 1)
    return row, col


def _nsa_sel_kernel(kmax_ref, q_ref, msel_ref, k_ref, v_ref, sm_ref, o_ref, qa_ref, sa_ref, sb_ref, *, kb):
    g = pl.program_id(1)
    i = pl.program_id(2)
    q0 = i * Q_BLOCK
    rows = NSA_REP * Q_BLOCK
    msel = msel_ref[0, 0, 0]
    for r in range(NSA_REP):
        qa_ref[r * Q_BLOCK:(r + 1) * Q_BLOCK, :] = jnp.concatenate([q_ref[0, r], msel], axis=1)
    qa = qa_ref[...]
    last = (q0 + Q_BLOCK - 1) // kb

    def logits(kt):
        k0 = pl.multiple_of(kt * kb, kb)
        return _dot_nt(qa, k_ref[0, 0, pl.ds(k0, kb), :])

    def v_tile(kt):
        return v_ref[0, 0, pl.ds(pl.multiple_of(kt * kb, kb), kb), :]

    reach = _query_norm(qa) * kmax_ref[pl.program_id(0) * N_NSA_KV + g]
    slope = sum(qa[:, HEAD_DIM + 2 * j:HEAD_DIM + 2 * j + 1].astype(F32) for j in range(len(LOG2E_PARTS)))
    row, col = _stacked_rows(rows, kb)
    causal = (last * kb + col) <= (q0 + row)
    sa_ref[...] = jnp.where(causal, logits(last), NEG)

    def margin(top, m):
        last_pos = (jnp.maximum(top, 0) * kb + kb - 1).astype(F32)
        return jnp.max(reach + slope * last_pos - m)

    def cond(state):
        top, ahead, _ = state
        return (top >= 0) & (ahead > -SKIP_T2)

    def body(state):
        top, _, carry = state
        ahead = margin(top - 2, carry[0])
        lower = jnp.maximum(top - 1, 0)
        sb_ref[...] = logits(lower) + jnp.where(top >= 1, 0.0, NEG)
        carry = _flash_update(carry, sa_ref[...], v_tile(top))
        sa_ref[...] = logits(jnp.maximum(top - 2, 0))
        return top - 2, ahead, _flash_update(carry, sb_ref[...], v_tile(lower))

    _, _, (_, acc) = lax.while_loop(cond, body, (last, jnp.float32(0.0), _flash_init(rows)))
    o_ref[0] = _nsa_out(acc, _nsa_gates(sm_ref[0], 1, g))


def _nsa_sel(q2, msel, k2, v2, small, kmax, kb):
    B, H, S, qw = q2.shape
    G = N_NSA_KV
    nq = S // Q_BLOCK
    nb = msel.shape[-1]
    grid_spec = pltpu.PrefetchScalarGridSpec(
        num_scalar_prefetch=1,
        grid=(B, G, nq),
        in_specs=[pl.BlockSpec((1, NSA_REP, Q_BLOCK, qw), lambda b, g, i, km: (b, g, i, 0)),
                  pl.BlockSpec((1, 1, 1, Q_BLOCK, nb), lambda b, g, i, km: (b, g, i, 0, 0)),
                  pl.BlockSpec((1, 1, S, k2.shape[-1]), lambda b, g, i, km: (b, g, 0, 0)),
                  pl.BlockSpec((1, 1, S, 2 * LANES), lambda b, g, i, km: (b, g, 0, 0)),
                  pl.BlockSpec((1, Q_BLOCK, LANES), lambda b, g, i, km: (b, i, 0))],
        out_specs=pl.BlockSpec((1, Q_BLOCK, 2 * LANES), lambda b, g, i, km: (b, i, g)),
        scratch_shapes=[pltpu.VMEM((NSA_REP * Q_BLOCK, qw + nb), BF16),
                        pltpu.VMEM((NSA_REP * Q_BLOCK, kb), F32),
                        pltpu.VMEM((NSA_REP * Q_BLOCK, kb), F32)],
    )
    return pl.pallas_call(
        functools.partial(_nsa_sel_kernel, kb=kb),
        grid_spec=grid_spec,
        out_shape=jax.ShapeDtypeStruct((B, S, D_NSA), F32),
        compiler_params=_cparams(("parallel", "parallel", "parallel")),
        name="nsa_selected",
    )(kmax, q2, msel, k2, v2, small)


def _nsa_win_kernel(q_ref, k_ref, v_ref, sm_ref, o_ref):
    g = pl.program_id(1)
    i = pl.program_id(2)
    kb = Q_BLOCK
    rows = NSA_REP * Q_BLOCK
    qa = q_ref[0].reshape(rows, q_ref.shape[-1])
    n_back = WINDOW // kb
    row, col = _stacked_rows(rows, kb)
    carry = _flash_init(rows)
    for d in range(n_back + 1):
        kt = i - n_back + d
        k0 = pl.multiple_of(jnp.maximum(kt, 0) * kb, kb)
        s = _dot_nt(qa, k_ref[0, 0, pl.ds(k0, kb), :])
        if d == 0:
            s = jnp.where((col > row) & (kt >= 0), s, NEG)
        elif d == n_back:
            s = jnp.where(col <= row, s, NEG)
        else:
            s = jnp.where(kt >= 0, s, NEG)
        carry = _flash_update(carry, s, v_ref[0, 0, pl.ds(k0, kb), :])
    o_ref[0] = _nsa_out(carry[1], _nsa_gates(sm_ref[0], 2, g))


def _nsa_win(q2, k2, v2, small):
    B, H, S, qw = q2.shape
    G = N_NSA_KV
    nq = S // Q_BLOCK
    return pl.pallas_call(
        _nsa_win_kernel,
        grid=(B, G, nq),
        in_specs=[pl.BlockSpec((1, NSA_REP, Q_BLOCK, qw), lambda b, g, i: (b, g, i, 0)),
                  pl.BlockSpec((1, 1, S, qw), lambda b, g, i: (b, g, 0, 0)),
                  pl.BlockSpec((1, 1, S, 2 * LANES), lambda b, g, i: (b, g, 0, 0)),
                  pl.BlockSpec((1, Q_BLOCK, LANES), lambda b, g, i: (b, i, 0))],
        out_specs=pl.BlockSpec((1, Q_BLOCK, 2 * LANES), lambda b, g, i: (b, i, g)),
        out_shape=jax.ShapeDtypeStruct((B, S, D_NSA), F32),
        compiler_params=_cparams(("parallel", "parallel", "parallel")),
        name="nsa_window",
    )(q2, k2, v2, small)


def _fox_update(carry, s, v_tile):
    m, acc = carry
    m_new = jnp.maximum(m, jnp.max(s, axis=1, keepdims=True))
    p = jnp.exp2(s - m_new).astype(BF16)
    return m_new, jnp.exp2(m - m_new) * acc + _dot(p, v_tile)


def _query_norm(q):
    qf = q.astype(F32)
    lane = lax.broadcasted_iota(jnp.int32, qf.shape, 1)
    return jnp.sqrt(jnp.sum(jnp.where(lane < HEAD_DIM, qf * qf, 0.0), axis=1, keepdims=True))


def _fox_kernel(kmax_ref, tb_ref, q_ref, k_ref, v_ref, o_ref, *s_refs, tq, kb, n_heads, n_tiles):
    b = pl.program_id(0)
    hp = pl.program_id(1)
    i = pl.program_id(2)
    heads = [b * n_heads + hp * 2 + h for h in range(2)]

    def logits(h, kt):
        k0 = pl.multiple_of(kt * kb, kb)
        return _dot_nt(q_ref[0, h], k_ref[0, h, pl.ds(k0, kb), :])

    def v_tile(h, kt):
        return v_ref[0, h, pl.ds(pl.multiple_of(kt * kb, kb), kb), :]

    row = lax.broadcasted_iota(jnp.int32, (tq, kb), 0)
    col = lax.broadcasted_iota(jnp.int32, (tq, kb), 1)
    reach, carries = [], []
    for h in range(2):
        reach.append(_query_norm(q_ref[0, h]) * kmax_ref[heads[h]])
        c = _fox_update(_flash_init(tq), jnp.where(col <= row, logits(h, 2 * i), NEG), v_tile(h, 2 * i))
        carries.append(_fox_update(c, jnp.where(kb + col <= row, logits(h, 2 * i + 1), NEG), v_tile(h, 2 * i + 1)))
        s_refs[2 * h][...] = logits(h, jnp.maximum(2 * i - 1, 0))

    def margin(top, carries):
        t = jnp.maximum(top, 0)
        a = jnp.max(reach[0] + tb_ref[heads[0] * n_tiles + t] - carries[0][0])
        return jnp.maximum(a, jnp.max(reach[1] + tb_ref[heads[1] * n_tiles + t] - carries[1][0]))

    def cond(state):
        top, ahead, _ = state
        return (top >= 0) & (ahead > -SKIP_T2)

    def body(state):
        top, _, carries = state
        ahead = margin(top - 2, carries)
        out = []
        for h in range(2):
            sa_ref, sb_ref = s_refs[2 * h], s_refs[2 * h + 1]
            sb_ref[...] = logits(h, top - 1)
            c = _fox_update(carries[h], sa_ref[...], v_tile(h, top))
            sa_ref[...] = logits(h, jnp.maximum(top - 2, 0))
            out.append(_fox_update(c, sb_ref[...], v_tile(h, top - 1)))
        return top - 2, ahead, tuple(out)

    _, _, carries = lax.while_loop(cond, body, (2 * i - 1, margin(2 * i - 1, carries), tuple(carries)))
    o_ref[0] = _pair_out(carries[0][1], carries[1][1])


def _fox(q2, k2, v2, kmax, tile_bias):
    B, H, S, qw = q2.shape
    tq, kb = min(FOX_TQ, S), min(FOX_KB, S)
    assert tq == 2 * kb
    grid_spec = pltpu.PrefetchScalarGridSpec(
        num_scalar_prefetch=2,
        grid=(B, H // 2, S // tq),
        in_specs=[pl.BlockSpec((1, 2, tq, qw), lambda b, h, i, km, tb: (b, h, i, 0)),
                  pl.BlockSpec((1, 2, S, qw), lambda b, h, i, km, tb: (b, h, 0, 0)),
                  pl.BlockSpec((1, 2, S, LANES), lambda b, h, i, km, tb: (b, h, 0, 0))],
        out_specs=pl.BlockSpec((1, tq, LANES), lambda b, h, i, km, tb: (b, i, h)),
        scratch_shapes=[pltpu.VMEM((tq, kb), F32)] * 4,
    )
    return pl.pallas_call(
        functools.partial(_fox_kernel, tq=tq, kb=kb, n_heads=H, n_tiles=S // kb),
        grid_spec=grid_spec,
        out_shape=jax.ShapeDtypeStruct((B, S, D_FOX), F32),
        compiler_params=_cparams(("parallel", "parallel", "parallel")),
        name="fox_attention",
    )(kmax, tile_bias, q2, k2, v2)


def _logf_cumsum_kernel(x_ref, c_ref):
    x = x_ref[0]
    nr, nl = x.shape
    log_f = jnp.minimum(x, 0.0) - jnp.log1p(jnp.exp(-jnp.abs(x)))
    hp = lax.Precision.HIGHEST
    incl = (lax.broadcasted_iota(jnp.int32, (nl, nl), 0) <= lax.broadcasted_iota(jnp.int32, (nl, nl), 1))
    within = jnp.dot(log_f, incl.astype(F32), preferred_element_type=F32, precision=hp)
    totals = jnp.broadcast_to(within[:, nl - 1:nl], (nr, nl))
    before = (lax.broadcasted_iota(jnp.int32, (nr, nr), 1) < lax.broadcasted_iota(jnp.int32, (nr, nr), 0))
    c_ref[0] = within + jnp.dot(before.astype(F32), totals, preferred_element_type=F32, precision=hp)


def _logf_cumsum(logits):
    B, H, S = logits.shape
    x = logits.reshape(B * H, S // LANES, LANES)
    c = pl.pallas_call(
        _logf_cumsum_kernel,
        grid=(B * H,),
        in_specs=[pl.BlockSpec((1, S // LANES, LANES), lambda i: (i, 0, 0))],
        out_specs=pl.BlockSpec((1, S // LANES, LANES), lambda i: (i, 0, 0)),
        out_shape=jax.ShapeDtypeStruct(x.shape, F32),
        compiler_params=_cparams(("parallel",)),
        name="logf_cumsum",
    )(x)
    return c.reshape(B, H, S)


def _sb_kernel(q_ref, k_ref, v_ref, o_ref, *, tq, nh):
    i = pl.program_id(1)
    q0 = i * tq
    trow = q0 + lax.broadcasted_iota(jnp.int32, (tq, tq), 0)
    col = lax.broadcasted_iota(jnp.int32, (tq, tq), 1)
    rr = lax.broadcasted_iota(jnp.int32, (tq, tq), 0)
    upper = (rr > col).astype(BF16)

    def cond(state):
        kt, carries, _ = state
        alive = jnp.max(carries[0])
        for h in range(1, nh):
            alive = jnp.maximum(alive, jnp.max(carries[h]))
        return (kt >= 0) & (alive > SB_CUTOFF)

    def body(state):
        kt, carries, accs = state
        k0 = pl.multiple_of(kt * tq, tq)
        strict = (k0 + col) < trow
        new_c, new_a = [], []
        for h in range(nh):
            z = _dot_nt(q_ref[0, h], k_ref[0, h, pl.ds(k0, tq), :])
            log_beta = jnp.minimum(z, 0.0) - jnp.log1p(jnp.exp(-jnp.abs(z)))
            log_keep = jnp.where(strict, log_beta - z, 0.0)
            hi, lo = _split_bf16(log_keep)
            later = _dot(hi, upper) + _dot(lo, upper)
            a = jnp.where(strict, jnp.exp(log_beta + later + carries[h]), 0.0)
            new_a.append(accs[h] + _dot(a.astype(BF16), v_ref[0, h, pl.ds(k0, tq), :]))
            new_c.append(carries[h] + jnp.sum(log_keep, axis=1, keepdims=True))
        return kt - 1, tuple(new_c), tuple(new_a)

    state = (i, tuple(jnp.zeros((tq, 1), F32) for _ in range(nh)),
             tuple(jnp.zeros((tq, LANES), F32) for _ in range(nh)))
    _, _, accs = lax.while_loop(cond, body, state)
    o_ref[0] = jnp.concatenate([accs[2 * j] + accs[2 * j + 1] for j in range(nh // 2)], axis=1)


def _sb(q2, k2, v2):
    B, H, S, w = q2.shape
    tq = min(SB_TQ, S)
    return pl.pallas_call(
        functools.partial(_sb_kernel, tq=tq, nh=H),
        grid=(B, S // tq),
        in_specs=[pl.BlockSpec((1, H, tq, w), lambda b, i: (b, 0, i, 0)),
                  pl.BlockSpec((1, H, S, w), lambda b, i: (b, 0, 0, 0)),
                  pl.BlockSpec((1, H, S, w), lambda b, i: (b, 0, 0, 0))],
        out_specs=pl.BlockSpec((1, tq, D_SB), lambda b, i: (b, i, 0)),
        out_shape=jax.ShapeDtypeStruct((B, S, D_SB), F32),
        compiler_params=_cparams(("parallel", "parallel")),
        name="sb_attention",
    )(q2, k2, v2)


def _layer_norm(y, g, b):
    mu = jnp.mean(y, axis=1, keepdims=True)
    d = y - mu
    var = jnp.mean(d * d, axis=1, keepdims=True)
    return d * lax.rsqrt(var + LN_EPS) * g + b


def _rms(o, g):
    return o * lax.rsqrt(jnp.mean(o * o, axis=1, keepdims=True) + RMS_EPS) * g


def _post_kernel(oc_ref, osel_ref, ow_ref, of_ref, os_ref, x_ref, go_ref, wo_ref, lg_ref, lb_ref, rw_ref, rb_ref,
                 x1_ref, te_ref, tw_ref, *, alpha):
    o_nsa = oc_ref[...] + osel_ref[...] + ow_ref[...]
    n1 = _rms(o_nsa, go_ref[:, 0:D_NSA]).astype(BF16)
    n2 = _rms(of_ref[...], go_ref[:, D_NSA:D_NSA + D_FOX]).astype(BF16)
    n3 = _rms(os_ref[...], go_ref[:, D_NSA + D_FOX:]).astype(BF16)
    mix = (_dot(n1, wo_ref[0:D_NSA, :]) + _dot(n2, wo_ref[D_NSA:D_NSA + D_FOX, :])
           + _dot(n3, wo_ref[D_NSA + D_FOX:, :]))
    x1 = _layer_norm(alpha * x_ref[...] + mix, lg_ref[...], lb_ref[...])
    x1_ref[...] = x1
    logits = jnp.dot(x1, rw_ref[...], preferred_element_type=F32, precision=lax.Precision.HIGHEST) + rb_ref[...]
    tm, ne = logits.shape
    lane = lax.broadcasted_iota(jnp.int32, (tm, ne), 1).astype(F32)
    wide = lax.broadcasted_iota(jnp.int32, (tm, LANES), 1)
    top_e = jnp.zeros((tm, LANES), F32)
    top_l = jnp.full((tm, LANES), NEG, F32)
    cur = logits
    for k in range(TOP_K):
        mx = jnp.max(cur, axis=1, keepdims=True)
        idx = jnp.min(jnp.where(cur == mx, lane, float(ne)), axis=1, keepdims=True)
        top_e = jnp.where(wide == k, idx, top_e)
        top_l = jnp.where(wide == k, mx, top_l)
        cur = jnp.where(lane == idx, -BIG, cur)
    pe = jnp.exp(top_l - jnp.max(top_l, axis=1, keepdims=True))
    te_ref[...] = top_e.astype(jnp.int32)
    tw_ref[...] = pe / jnp.sum(pe, axis=1, keepdims=True)


def _post(o_c, o_s, o_w, o_fox, o_sb, x2d, g_out, w_out_bf, ln_g, ln_b, router_w, router_b, alpha):
    T, D = x2d.shape
    tm = POST_TM
    row = lambda w: pl.BlockSpec((tm, w), lambda i: (i, 0))
    full = lambda shape: pl.BlockSpec(shape, lambda i: (0, 0))
    return pl.pallas_call(
        functools.partial(_post_kernel, alpha=alpha),
        grid=(T // tm,),
        in_specs=[row(D_NSA), row(D_NSA), row(D_NSA), row(D_FOX), row(D_SB), row(D), full((1, D)), full((D, D)),
                  full((1, D)), full((1, D)), full((D, N_EXPERTS)), full((1, N_EXPERTS))],
        out_specs=[row(D), row(LANES), row(LANES)],
        out_shape=[jax.ShapeDtypeStruct((T, D), F32), jax.ShapeDtypeStruct((T, LANES), jnp.int32),
                   jax.ShapeDtypeStruct((T, LANES), F32)],
        compiler_params=_cparams(("parallel",)),
        name="outproj_ln_router",
    )(o_c, o_s, o_w, o_fox, o_sb, x2d, g_out.reshape(1, D), w_out_bf, ln_g.reshape(1, D), ln_b.reshape(1, D),
      router_w, router_b.reshape(1, N_EXPERTS))


def _row_gather_start(src_hbm, dst, sem, idx_ref, n_rows):
    def issue(r8, c):
        for u in range(8):
            r = r8 * 8 + u
            pltpu.make_async_copy(src_hbm.at[pl.ds(idx_ref[0, 0, r], 1)], dst.at[pl.ds(r, 1)],
                                  sem).start(priority=u % 2)
        return c
    lax.fori_loop(0, n_rows // 8, issue, 0)


def _row_gather_wait(src_hbm, dst, sem, n_rows):
    pltpu.make_async_copy(src_hbm.at[pl.ds(0, n_rows)], dst, sem).wait()


def _expert_kernel(blk_e_ref, nused_ref, idx_ref, idxn_ref, x_hbm, w_ref, wgu_ref, bgu_ref, wdn_ref, bdn_ref,
                   y_ref, xbuf, wgu_bf, wdn_bf, sem, *, tm, d_ff):
    i = pl.program_id(0)
    n_used = nused_ref[0]
    slot = i % 2

    @pl.when(i == 0)
    def _():
        _row_gather_start(x_hbm, xbuf.at[0], sem.at[0], idx_ref, tm)

    @pl.when(i <= n_used)
    def _():
        _row_gather_wait(x_hbm, xbuf.at[slot], sem.at[slot], tm)

    @pl.when(i < n_used)
    def _():
        changed = (i == 0) | (blk_e_ref[i] != blk_e_ref[jnp.maximum(i - 1, 0)])

        @pl.when(changed)
        def _():
            wgu_bf[...] = wgu_ref[0, 0].astype(BF16)
            wdn_bf[...] = wdn_ref[0, 0].astype(BF16)

        nxt = xbuf.at[1 - slot]
        for r in range(tm):
            pltpu.make_async_copy(x_hbm.at[pl.ds(idxn_ref[0, 0, r], 1)], nxt.at[pl.ds(r, 1)],
                                  sem.at[1 - slot]).start(priority=r % 2)
        xb = xbuf[slot].astype(BF16)
        gu = _dot(xb, wgu_bf[...]) + bgu_ref[0, 0]
        gate = jnp.minimum(gu[:, :d_ff], SWIGLU_LIMIT)
        up = jnp.clip(gu[:, d_ff:], -SWIGLU_LIMIT, SWIGLU_LIMIT)
        act = gate * _sigmoid(SWIGLU_ALPHA * gate) * (up + 1.0)
        y = _dot(act.astype(BF16), wdn_bf[...]) + bdn_ref[0, 0]
        y_ref[...] = y * w_ref[0]

    @pl.when(i >= n_used)
    def _():
        y_ref[...] = jnp.zeros_like(y_ref)


def _experts(x1, blk_e, n_used, buf_tok, buf_w, w_gu, b_gu, w_dn, b_dn, layer, tm):
    T, D = x1.shape
    nl, ne, _, d2 = w_gu.shape
    d_ff = d2 // 2
    n_blocks = blk_e.shape[0]
    idx = buf_tok.reshape(n_blocks, 1, tm)
    wcol = buf_w.reshape(n_blocks, tm, 1)
    last = n_blocks - 1
    grid_spec = pltpu.PrefetchScalarGridSpec(
        num_scalar_prefetch=2,
        grid=(n_blocks,),
        in_specs=[pl.BlockSpec((1, 1, tm), lambda i, e, n: (i, 0, 0), memory_space=pltpu.SMEM),
                  pl.BlockSpec((1, 1, tm), lambda i, e, n: (jnp.minimum(i + 1, last), 0, 0),
                               memory_space=pltpu.SMEM),
                  pl.BlockSpec(memory_space=pl.ANY),
                  pl.BlockSpec((1, tm, 1), lambda i, e, n: (i, 0, 0)),
                  pl.BlockSpec((1, 1, D, d2), lambda i, e, n: (layer, e[i], 0, 0)),
                  pl.BlockSpec((1, 1, 1, d2), lambda i, e, n: (layer, e[i], 0, 0)),
                  pl.BlockSpec((1, 1, d_ff, D), lambda i, e, n: (layer, e[i], 0, 0)),
                  pl.BlockSpec((1, 1, 1, D), lambda i, e, n: (layer, e[i], 0, 0))],
        out_specs=pl.BlockSpec((tm, D), lambda i, e, n: (i, 0)),
        scratch_shapes=[pltpu.VMEM((2, tm, D), F32), pltpu.VMEM((D, d2), BF16), pltpu.VMEM((d_ff, D), BF16),
                        pltpu.SemaphoreType.DMA((2,))],
    )
    return pl.pallas_call(
        functools.partial(_expert_kernel, tm=tm, d_ff=d_ff),
        grid_spec=grid_spec,
        out_shape=jax.ShapeDtypeStruct((n_blocks * tm, D), F32),
        compiler_params=_cparams(("arbitrary",), 58 * 1024 * 1024),
        name="moe_experts",
    )(blk_e, n_used, idx, idx, x1, wcol, w_gu, b_gu.reshape(nl, ne, 1, d2), w_dn, b_dn.reshape(nl, ne, 1, D))


def _combine_kernel(idx_ref, idxn_ref, y_hbm, x_ref, lg_ref, lb_ref, o_ref, ybuf, sem, *, tm, alpha):
    i = pl.program_id(0)
    n = pl.num_programs(0)
    slot = i % 2
    rows = TOP_K * tm

    @pl.when(i == 0)
    def _():
        _row_gather_start(y_hbm, ybuf.at[0], sem.at[0], idx_ref, rows)

    @pl.when(i + 1 < n)
    def _():
        nxt = ybuf.at[1 - slot]
        for r in range(rows):
            pltpu.make_async_copy(y_hbm.at[pl.ds(idxn_ref[0, 0, r], 1)], nxt.at[pl.ds(r, 1)],
                                  sem.at[1 - slot]).start(priority=r % 2)

    _row_gather_wait(y_hbm, ybuf.at[slot], sem.at[slot], rows)
    ffn = ybuf[slot, 0:tm]
    for k in range(1, TOP_K):
        ffn = ffn + ybuf[slot, k * tm:(k + 1) * tm]
    o_ref[...] = _layer_norm(alpha * x_ref[...] + ffn, lg_ref[...], lb_ref[...])


def _combine(y_sorted, pos, x1, ln_g, ln_b, alpha):
    T, D = x1.shape
    tm = COMBINE_TM
    nt = T // tm
    idx = pos.reshape(nt, tm, TOP_K).transpose(0, 2, 1).reshape(nt, 1, TOP_K * tm)
    last = nt - 1
    return pl.pallas_call(
        functools.partial(_combine_kernel, tm=tm, alpha=alpha),
        grid=(nt,),
        in_specs=[pl.BlockSpec((1, 1, TOP_K * tm), lambda i: (i, 0, 0), memory_space=pltpu.SMEM),
                  pl.BlockSpec((1, 1, TOP_K * tm), lambda i: (jnp.minimum(i + 1, last), 0, 0),
                               memory_space=pltpu.SMEM),
                  pl.BlockSpec(memory_space=pl.ANY),
                  pl.BlockSpec((tm, D), lambda i: (i, 0)),
                  pl.BlockSpec((1, D), lambda i: (0, 0)),
                  pl.BlockSpec((1, D), lambda i: (0, 0))],
        out_specs=pl.BlockSpec((tm, D), lambda i: (i, 0)),
        out_shape=jax.ShapeDtypeStruct((T, D), F32),
        scratch_shapes=[pltpu.VMEM((2, TOP_K * tm, D), F32), pltpu.SemaphoreType.DMA((2,))],
        compiler_params=_cparams(("arbitrary",)),
        name="moe_combine_ln",
    )(idx, idx, y_sorted, x1, ln_g.reshape(1, D), ln_b.reshape(1, D))


def _route(top_e, top_w, tm):
    T = top_e.shape[0]
    M = T * TOP_K
    i32 = jnp.int32
    flat_e = top_e.reshape(M).astype(i32)
    flat_w = top_w.reshape(M)
    ar = jnp.arange(M, dtype=i32)
    experts = jnp.arange(N_EXPERTS, dtype=i32)
    skey, sorted_w = lax.sort((flat_e * M + ar, flat_w), num_keys=1)
    sorted_e = skey // M
    order = skey - sorted_e * M
    counts = jnp.sum((flat_e[:, None] == experts[None, :]).astype(i32), axis=0)
    start = jnp.cumsum(counts) - counts
    padded = (counts + tm - 1) // tm * tm
    pad_end = jnp.cumsum(padded)
    pad_start = pad_end - padded
    dest = pad_start[sorted_e] + ar - start[sorted_e]
    n_blocks = -(-M // tm) + N_EXPERTS
    blk_first = jnp.arange(n_blocks, dtype=i32) * tm
    blk_e = jnp.minimum(jnp.sum((pad_end[None, :] <= blk_first[:, None]).astype(i32), axis=1), N_EXPERTS - 1)
    within = (blk_first - pad_start[blk_e])[:, None] + jnp.arange(tm, dtype=i32)[None, :]
    valid = within < counts[blk_e][:, None]
    src_row = jnp.clip(start[blk_e][:, None] + within, 0, M - 1)
    buf_tok = jnp.where(valid, order[src_row] // TOP_K, 0).reshape(n_blocks * tm)
    buf_w = jnp.where(valid, sorted_w[src_row], 0.0).reshape(n_blocks * tm)
    _, pos = lax.sort((order, dest), num_keys=1)
    n_used = (pad_end[-1] // tm).astype(i32).reshape(1)
    return blk_e, n_used, buf_tok, buf_w, pos


class _Plan:
    def __init__(self):
        self.src, self.scale, self.const, self.outs = [], [], [], []

    def group(self, width, src_cols=(), at=0, scale=1.0, ones=()):
        src = np.full((width,), -1, np.int64)
        sc = np.zeros((width,), np.float32)
        const = np.zeros((width,), np.float32)
        src[at:at + len(src_cols)] = src_cols
        sc[at:at + len(src_cols)] = scale
        for col, val in ones:
            const[col] = val
        self.src.append(src)
        self.scale.append(sc)
        self.const.append(const)

    def out(self, n_heads, width, dtype):
        self.outs.append((n_heads, width, dtype))


def _cols(start):
    return np.arange(start, start + HEAD_DIM)


def _value_group(plan, src, odd):
    if odd:
        plan.group(LANES, _cols(src), at=HALF, ones=[(0, 1.0)])
    else:
        plan.group(LANES, _cols(src), ones=[(HALF, 1.0)])


def _plans(S):
    qs = HEAD_DIM ** -0.5
    slope = 2.0 ** (-8.0 * np.arange(1, N_NSA_HEADS + 1) / N_NSA_HEADS)
    a = _Plan()
    for g in range(N_NSA_KV):
        for hl in NSA_ROW_HEADS:
            h = g * NSA_REP + hl
            a.group(LANES, _cols(h * HEAD_DIM), scale=qs * LOG2E,
                    ones=[(HALF + 2 * j + u, slope[h] * part) for j, part in enumerate(LOG2E_PARTS) for u in range(2)])
    a.out(N_NSA_HEADS, LANES, BF16)
    kv = lambda branch, which, g: SRC_KV + branch * 256 + which * 128 + g * HEAD_DIM
    for which in range(2):
        for g in range(N_NSA_KV):
            a.group(LANES, _cols(kv(0, which, g)))
    a.out(2 * N_NSA_KV, LANES, F32)
    for g in range(N_NSA_KV):
        a.group(2 * LANES, _cols(kv(1, 0, g)))
    a.out(N_NSA_KV, 2 * LANES, BF16)
    for g in range(N_NSA_KV):
        _value_group(a, kv(1, 1, g), False)
        _value_group(a, kv(1, 1, g), True)
    a.out(N_NSA_KV, 2 * LANES, BF16)
    for g in range(N_NSA_KV):
        a.group(LANES, _cols(kv(2, 0, g)))
    a.out(N_NSA_KV, LANES, BF16)
    for g in range(N_NSA_KV):
        _value_group(a, kv(2, 1, g), False)
        _value_group(a, kv(2, 1, g), True)
    a.out(N_NSA_KV, 2 * LANES, BF16)
    a.group(LANES, np.concatenate([np.arange(SRC_GATE, SRC_GATE + 24), np.arange(SRC_LOGF, SRC_LOGF + 4)]))
    a.out(1, LANES, F32)

    n_a = sum(len(s) for s in a.src)
    extra = np.zeros((S, n_a), np.float32)
    pos = np.arange(S)
    pos_hi, pos_lo = pos // SEL_LEN * SEL_LEN, pos % SEL_LEN
    off_ksel = N_NSA_HEADS * LANES + 2 * N_NSA_KV * LANES
    off_kwin = off_ksel + N_NSA_KV * 2 * LANES + N_NSA_KV * 2 * LANES
    nb = S // SEL_LEN
    for g in range(N_NSA_KV):
        o = off_ksel + g * 2 * LANES
        extra[pos, o + LANES + pos // SEL_LEN] = 1.0
        for base in (o, off_kwin + g * LANES):
            for j in range(len(LOG2E_PARTS)):
                extra[:, base + HALF + 2 * j], extra[:, base + HALF + 2 * j + 1] = pos_hi, pos_lo
    assert nb <= LANES

    b = _Plan()
    fox = lambda which, h: SRC_FOX + which * 256 + h * HEAD_DIM
    sb = lambda which, h: SRC_SB + which * 256 + h * HEAD_DIM
    for h in range(N_FOX_HEADS):
        b.group(LANES, _cols(fox(0, h)), scale=qs * LOG2E, ones=[(HALF, 1.0), (HALF + 1, 1.0), (HALF + 2, 1.0)])
    b.out(N_FOX_HEADS, LANES, BF16)
    for h in range(N_FOX_HEADS):
        b.group(LANES, _cols(fox(1, h)))
    b.out(N_FOX_HEADS, LANES, BF16)
    for h in range(N_FOX_HEADS):
        _value_group(b, fox(2, h), h % 2 == 1)
    b.out(N_FOX_HEADS, LANES, BF16)
    for h in range(N_SB_HEADS):
        b.group(LANES, _cols(sb(0, h)), scale=qs)
    b.out(N_SB_HEADS, LANES, BF16)
    for h in range(N_SB_HEADS):
        b.group(LANES, _cols(sb(1, h)))
    b.out(N_SB_HEADS, LANES, BF16)
    for h in range(N_SB_HEADS):
        b.group(LANES, _cols(sb(2, h)), at=HALF if h % 2 else 0)
    b.out(N_SB_HEADS, LANES, BF16)
    return a, jnp.asarray(extra, dtype=BF16), b


def _constants(S):
    plan_a, extra_a, plan_b = _plans(S)
    nc = S // CMP_STRIDE
    nb = S // SEL_LEN
    n_cmp = (S - CMP_LEN) // CMP_STRIDE + 1
    cmp_idx = np.arange(nc)[:, None] * CMP_STRIDE + np.arange(CMP_LEN)[None, :]
    ovl = (cmp_idx[:, :, None] // SEL_LEN == np.arange(nb)[None, None, :]).astype(np.float32).mean(axis=1)
    ovl[n_cmp:] = 0.0
    ovl = np.pad(ovl, ((0, 0), (0, LANES - nb)))
    cmp_end = np.arange(nc) * CMP_STRIDE + CMP_LEN - 1
    kcmp_aug = np.zeros((nc, LANES), np.float32)
    for j in range(len(LOG2E_PARTS)):
        kcmp_aug[:, HALF + 2 * j] = cmp_end // SEL_LEN * SEL_LEN
        kcmp_aug[:, HALF + 2 * j + 1] = cmp_end % SEL_LEN
    return dict(a=plan_a, a_outs=tuple(plan_a.outs), extra_a=extra_a, b=plan_b, b_outs=tuple(plan_b.outs),
                ovl_t=jnp.asarray(ovl.T, dtype=BF16), kcmp_aug=jnp.asarray(kcmp_aug))


def _plan_weights(w_in, b_in, plan):
    src = np.concatenate(plan.src)
    scale = np.concatenate(plan.scale)
    const = jnp.asarray(np.concatenate(plan.const))
    wb = jnp.concatenate([w_in, b_in[None, :]], axis=0)
    pieces, c = [], 0
    while c < len(src):
        e = c + 1
        if src[c] < 0:
            while e < len(src) and src[e] < 0:
                e += 1
            pieces.append(jnp.zeros((wb.shape[0], e - c), F32))
        else:
            while e < len(src) and src[e] == src[e - 1] + 1 and scale[e] == scale[c]:
                e += 1
            pieces.append(wb[:, src[c]:src[c] + e - c] * float(scale[c]))
        c = e
    wb_p = jnp.concatenate(pieces, axis=1)
    return wb_p[:-1].astype(BF16), (wb_p[-1] + const).reshape(1, -1)


def _key_norm_max(k2):
    kf = k2[..., 0:HEAD_DIM].astype(F32)
    return jnp.sqrt(jnp.max(jnp.sum(kf * kf, axis=-1), axis=-1)).reshape(-1)


def _layer(x, cs, w_in, b_in, pos_k, pos_v, w1k, w2k, w1v, w2v, g_out, w_out, ln1_g, ln1_b,
           router_w, router_b, w_gu, b_gu, w_dn, b_dn, ln2_g, ln2_b, alpha, layer):
    B, S, D = x.shape
    T = B * S
    x2d = x.reshape(T, D)
    wa, ba = _plan_weights(w_in, b_in, cs["a"])
    wb, bb = _plan_weights(w_in, b_in, cs["b"])
    q2, kvc, ksel, vsel, kwin, vwin, small = _proj(x2d, wa, ba, cs["extra_a"], cs["a_outs"], B, S)
    fq, fk, fv, sq, sk, sv = _proj(x2d, wb, bb, None, cs["b_outs"], B, S)
    small = small.reshape(B, S, LANES)

    kc = kvc[:, 0:N_NSA_KV, :, 0:HEAD_DIM]
    vc = kvc[:, N_NSA_KV:, :, 0:HEAD_DIM]
    k_cmp, v_cmp = _compress(kc, vc, pos_k, pos_v, w1k, w2k, w1v, w2v, cs["kcmp_aug"])
    o_c, msel = _nsa_cmp(q2, k_cmp, v_cmp, cs["ovl_t"], small)
    o_s = _nsa_sel(q2, msel, ksel, vsel, small, _key_norm_max(ksel), min(SEL_KB, S))
    o_w = _nsa_win(q2, kwin, vwin, small)

    c = _logf_cumsum(small[..., 24:24 + N_FOX_HEADS].transpose(0, 2, 1))
    c = c * LOG2E
    c_hi = _trunc_bf16(c)
    c_mid = _trunc_bf16(c - c_hi)
    c_lo = c - c_hi - c_mid
    c_aug = jnp.pad(-jnp.stack([c_hi, c_mid, c_lo], axis=-1), ((0, 0), (0, 0), (0, 0), (HALF, HALF - 3)))
    kb = min(FOX_KB, S)
    tile_bias = lax.cummax(jnp.max((-c).reshape(B, N_FOX_HEADS, S // kb, kb), axis=-1), axis=2)
    o_fox = _fox(fq, fk + c_aug.astype(BF16), fv, _key_norm_max(fk), tile_bias.reshape(-1))
    o_sb = _sb(sq, sk, sv)

    flat = lambda o: o.reshape(T, o.shape[-1])
    x1, te, tw = _post(flat(o_c), flat(o_s), flat(o_w), flat(o_fox), flat(o_sb), x2d, g_out, w_out.astype(BF16),
                       ln1_g, ln1_b, router_w, router_b, alpha)
    blk_e, n_used, buf_tok, buf_w, pos = _route(te[:, :TOP_K], tw[:, :TOP_K], MOE_TM)
    y_sorted = _experts(x1, blk_e, n_used, buf_tok, buf_w, w_gu, b_gu, w_dn, b_dn, layer, MOE_TM)
    out = _combine(y_sorted, pos, x1, ln2_g, ln2_b, alpha)
    return out.reshape(B, S, D)


def kernel(x, w_in, b_in, cmp_pos_k, cmp_pos_v, cmp_w1_k, cmp_w2_k, cmp_w1_v, cmp_w2_v, g_out, w_out,
           ln1_g, ln1_b, router_w, router_b, w_gate_up, b_gate_up, w_down, b_down, ln2_g, ln2_b):
    depth = w_in.shape[0]
    alpha = (2 * depth) ** 0.25
    consts = _constants(x.shape[1])
    for l in range(depth):
        x = _layer(x, consts, w_in[l], b_in[l], cmp_pos_k[l], cmp_pos_v[l], cmp_w1_k[l], cmp_w2_k[l],
                   cmp_w1_v[l], cmp_w2_v[l], g_out[l], w_out[l], ln1_g[l], ln1_b[l], router_w[l], router_b[l],
                   w_gate_up, b_gate_up, w_down, b_down, ln2_g[l], ln2_b[l], alpha, l)
    return x
```

```python
import functools

import jax
import jax.numpy as jnp
import numpy as np
from jax import lax
from jax.experimental import pallas as pl
from jax.experimental.pallas import tpu as pltpu

F32 = jnp.float32
BF16 = jnp.bfloat16

HEAD_DIM = 64
N_NSA_HEADS = 8
N_NSA_KV = 2
NSA_REP = 4
N_FOX_HEADS = 4
N_SB_HEADS = 4
D_NSA = 512
D_FOX = 256
D_SB = 256
CMP_LEN = 32
CMP_STRIDE = 16
CMP_HIDDEN = 128
SEL_LEN = 64
SEL_TOPK = 16
WINDOW = 512
Q_BLOCK = 128
N_EXPERTS = 32
TOP_K = 4
SWIGLU_LIMIT = 7.0
SWIGLU_ALPHA = 1.702
LN_EPS = 1e-5
RMS_EPS = 1e-6
NEG = -1e30
BIG = 3e38
SB_CUTOFF = -90.0
SKIP_T = 100.0
LOG2E = 1.4426950408889634
SKIP_T2 = SKIP_T * LOG2E


def _bf16_parts(x, n=3):
    parts = []
    for _ in range(n - 1):
        p = (np.float32(x).view(np.uint32) & np.uint32(0xFFFF0000)).view(np.float32)
        parts.append(float(p))
        x = x - float(p)
    parts.append(float((np.float32(x).view(np.uint32) & np.uint32(0xFFFF0000)).view(np.float32)))
    return parts


LOG2E_PARTS = _bf16_parts(LOG2E)

LANES = 128
HALF = LANES // 2
NSA_ROW_HEADS = (0, 2, 1, 3)

SRC_KV = 512
SRC_GATE = SRC_KV + 768
SRC_FOX = SRC_GATE + 24
SRC_LOGF = SRC_FOX + 768
SRC_SB = SRC_LOGF + 4

VMEM_LIMIT = 52 * 1024 * 1024

PROJ_TM = 512
CMP_SUB = 4
SEL_KB = 256
FOX_TQ = 512
FOX_KB = 256
SB_TQ = 256
POST_TM = 256
MOE_TM = 256
COMBINE_TM = 128


def _cparams(sem, vmem=VMEM_LIMIT):
    return pltpu.CompilerParams(dimension_semantics=sem, vmem_limit_bytes=vmem)


def _dot(a, b):
    return jnp.dot(a, b, preferred_element_type=F32)


def _dot_nt(a, b):
    return lax.dot_general(a, b, (((1,), (1,)), ((), ())), preferred_element_type=F32)


def _split_bf16(x):
    hi = x.astype(BF16)
    lo = (x - hi.astype(F32)).astype(BF16)
    return hi, lo


def _trunc_bf16(x):
    bits = lax.bitcast_convert_type(x, jnp.uint32) & jnp.uint32(0xFFFF0000)
    return lax.bitcast_convert_type(bits, F32)


def _sigmoid(x):
    return 1.0 / (1.0 + jnp.exp(-x))


def _lane_column(tile, idx):
    lane = lax.broadcasted_iota(jnp.int32, tile.shape, 1)
    return jnp.sum(jnp.where(lane == idx, tile, 0.0), axis=1, keepdims=True)


def _low_half(shape):
    return lax.broadcasted_iota(jnp.int32, shape, 1) < HALF


def _proj_kernel(*refs, outs, has_extra):
    x_ref, w_ref, b_ref = refs[:3]
    o_refs = refs[3 + has_extra:]
    y = _dot(x_ref[...].astype(BF16), w_ref[...]) + b_ref[...]
    if has_extra:
        y = y + refs[3][...].astype(F32)
    off = 0
    for o_ref, (nh, width, _) in zip(o_refs, outs):
        for h in range(nh):
            o_ref[0, h] = y[:, off:off + width].astype(o_ref.dtype)
            off += width


def _proj(x2d, w_bf, b_row, extra, outs, B, S):
    T, D = x2d.shape
    N = w_bf.shape[1]
    tm = min(PROJ_TM, S)
    spb = S // tm
    in_specs = [pl.BlockSpec((tm, D), lambda i: (i, 0)),
                pl.BlockSpec((D, N), lambda i: (0, 0)),
                pl.BlockSpec((1, N), lambda i: (0, 0))]
    args = [x2d, w_bf, b_row]
    if extra is not None:
        in_specs.append(pl.BlockSpec((tm, N), lambda i: (i % spb, 0)))
        args.append(extra)
    return pl.pallas_call(
        functools.partial(_proj_kernel, outs=outs, has_extra=extra is not None),
        grid=(T // tm,),
        in_specs=in_specs,
        out_specs=[pl.BlockSpec((1, nh, tm, w), lambda i: (i // spb, 0, i % spb, 0)) for nh, w, _ in outs],
        out_shape=[jax.ShapeDtypeStruct((B, nh, S, w), dt) for nh, w, dt in outs],
        compiler_params=_cparams(("parallel",)),
        name="proj",
    )(*args)


def _gelu_tanh(x):
    return 0.5 * x * (1.0 + jnp.tanh(0.7978845608028654 * (x + 0.044715 * (x * x * x))))


def _compress_kernel(k_ref, v_ref, pk_ref, pv_ref, w1k_ref, w2k_ref, w1v_ref, w2v_ref, ka_ref, ok_ref, ov_ref):
    nc = k_ref.shape[2]
    half = CMP_STRIDE * HEAD_DIM

    def one(r_ref, p_ref, w1_ref, w2_ref):
        r = r_ref[0, 0]
        a = (r + p_ref[0:1, :]).astype(BF16)
        b = (r + p_ref[1:2, :]).astype(BF16)
        ha = _dot(a, w1_ref[0:half, :])
        hb = _dot(b, w1_ref[half:2 * half, :])
        hid = _gelu_tanh(ha + pltpu.roll(hb, nc - 1, 0))
        return _dot(hid.astype(BF16), w2_ref[...])

    ok_ref[0, 0] = (one(k_ref, pk_ref, w1k_ref, w2k_ref) + ka_ref[...]).astype(ok_ref.dtype)
    ov_ref[0, 0] = one(v_ref, pv_ref, w1v_ref, w2v_ref).astype(ov_ref.dtype)


def _compress(kc, vc, pos_k, pos_v, w1k, w2k, w1v, w2v, kcmp_aug):
    B, G, S, dh = kc.shape
    nc = S // CMP_STRIDE
    width = CMP_STRIDE * dh
    kr = kc.reshape(B, G, nc, width)
    vr = vc.reshape(B, G, nc, width)
    pk = pos_k.reshape(2, width)
    pv = pos_v.reshape(2, width)
    w2k_p = jnp.pad(w2k, ((0, 0), (0, LANES - dh))).astype(BF16)
    w2v_p = jnp.concatenate([jnp.pad(w2v, ((0, 0), (0, LANES - dh))), jnp.pad(w2v, ((0, 0), (LANES - dh, 0)))],
                            axis=1).astype(BF16)
    kv_spec = pl.BlockSpec((1, 1, nc, width), lambda b, g: (b, g, 0, 0))
    full = lambda shape: pl.BlockSpec(shape, lambda b, g: tuple(0 for _ in shape))
    return pl.pallas_call(
        _compress_kernel,
        grid=(B, G),
        in_specs=[kv_spec, kv_spec, full((2, width)), full((2, width)),
                  full((2 * width, CMP_HIDDEN)), full((CMP_HIDDEN, LANES)),
                  full((2 * width, CMP_HIDDEN)), full((CMP_HIDDEN, 2 * LANES)), full((nc, LANES))],
        out_specs=[pl.BlockSpec((1, 1, nc, LANES), lambda b, g: (b, g, 0, 0)),
                   pl.BlockSpec((1, 1, nc, 2 * LANES), lambda b, g: (b, g, 0, 0))],
        out_shape=[jax.ShapeDtypeStruct((B, G, nc, LANES), BF16),
                   jax.ShapeDtypeStruct((B, G, nc, 2 * LANES), BF16)],
        compiler_params=_cparams(("parallel", "parallel")),
        name="nsa_compress",
    )(kr, vr, pk, pv, w1k.astype(BF16), w2k_p, w1v.astype(BF16), w2v_p, kcmp_aug)


def _nsa_gates(small, branch, g):
    return [_sigmoid(_lane_column(small, branch * N_NSA_HEADS + g * NSA_REP + h)) for h in NSA_ROW_HEADS]


def _nsa_cmp_kernel(q_ref, kc_ref, vc_ref, ovl_ref, sm_ref, o_ref, sel_ref, *, n_sel, n_sub):
    g = pl.program_id(1)
    i = pl.program_id(2)
    results = [_nsa_cmp_block(q_ref, kc_ref, vc_ref, ovl_ref, sm_ref, g, i * n_sub + u, u, n_sel)
               for u in range(n_sub)]
    for u, (o, msel) in enumerate(results):
        o_ref[0, u * Q_BLOCK:(u + 1) * Q_BLOCK, :] = o
        sel_ref[0, 0, u] = msel


def _nsa_cmp_block(q_ref, kc_ref, vc_ref, ovl_ref, sm_ref, g, qb, u, n_sel):
    q0 = qb * Q_BLOCK
    rows = slice(u * Q_BLOCK, (u + 1) * Q_BLOCK)
    nc = kc_ref.shape[2]
    nb = ovl_ref.shape[0]
    kc = kc_ref[0, 0]
    vc = vc_ref[0, 0]
    gates = _nsa_gates(sm_ref[0, rows, :], 0, g)
    t = q0 + lax.broadcasted_iota(jnp.int32, (Q_BLOCK, nc), 0)
    cmp_end = lax.broadcasted_iota(jnp.int32, (Q_BLOCK, nc), 1) * CMP_STRIDE + (CMP_LEN - 1)
    mask = cmp_end <= t
    psum = jnp.zeros((Q_BLOCK, nc), F32)
    outs = []
    for r in range(NSA_REP):
        s = jnp.where(mask, _dot_nt(q_ref[0, r, rows, :], kc), NEG)
        m = jnp.max(s, axis=1, keepdims=True)
        p = jnp.where(mask, jnp.exp2(s - m), 0.0)
        l = jnp.sum(p, axis=1, keepdims=True)
        pn = p / jnp.where(l > 0.0, l, 1.0)
        v_half = vc[:, 0:LANES] if r < 2 else vc[:, LANES:2 * LANES]
        outs.append(_dot(pn.astype(BF16), v_half) * gates[r])
        psum = psum + pn
    o = jnp.concatenate([outs[0] + outs[2], outs[1] + outs[3]], axis=1)
    hi, lo = _split_bf16(psum)
    ovl = ovl_ref[...]
    imp = _dot_nt(ovl, hi) + _dot_nt(ovl, lo)
    jf = lax.broadcasted_iota(jnp.int32, (nb, Q_BLOCK), 0)
    tq = q0 + lax.broadcasted_iota(jnp.int32, (nb, Q_BLOCK), 1)
    cur = tq >> 6
    valid = jf <= cur
    forced = valid & ((jf == 0) | (jf >= cur - 1))
    v = jnp.where(valid & ~forced, imp, -BIG)
    jff = jf.astype(F32)
    picked = jnp.where(forced, 1.0, 0.0)
    for _ in range(max(n_sel - 3, 0)):
        mx = jnp.max(v, axis=0, keepdims=True)
        idx = jnp.min(jnp.where(v == mx, jff, float(nb)), axis=0, keepdims=True)
        hit = jff == idx
        picked = jnp.where(hit, 1.0, picked)
        v = jnp.where(hit, -BIG, v)
    msel = jnp.where(valid & (picked > 0.5), 0.0, NEG)
    return o, msel.T.astype(BF16)


def _nsa_cmp(q2, kcmp, vcmp, ovl_t, small):
    B, H, S, qw = q2.shape
    G = N_NSA_KV
    nq = S // Q_BLOCK
    nc = kcmp.shape[2]
    nb = ovl_t.shape[0]
    n_sel = min(SEL_TOPK, S // SEL_LEN)
    n_sub = CMP_SUB
    return pl.pallas_call(
        functools.partial(_nsa_cmp_kernel, n_sel=n_sel, n_sub=n_sub),
        grid=(B, G, nq // n_sub),
        in_specs=[pl.BlockSpec((1, NSA_REP, n_sub * Q_BLOCK, qw), lambda b, g, i: (b, g, i, 0)),
                  pl.BlockSpec((1, 1, nc, LANES), lambda b, g, i: (b, g, 0, 0)),
                  pl.BlockSpec((1, 1, nc, 2 * LANES), lambda b, g, i: (b, g, 0, 0)),
                  pl.BlockSpec((nb, nc), lambda b, g, i: (0, 0)),
                  pl.BlockSpec((1, n_sub * Q_BLOCK, LANES), lambda b, g, i: (b, i, 0))],
        out_specs=[pl.BlockSpec((1, n_sub * Q_BLOCK, 2 * LANES), lambda b, g, i: (b, i, g)),
                   pl.BlockSpec((1, 1, n_sub, Q_BLOCK, nb), lambda b, g, i: (b, g, i, 0, 0))],
        out_shape=[jax.ShapeDtypeStruct((B, S, D_NSA), F32),
                   jax.ShapeDtypeStruct((B, G, nq, Q_BLOCK, nb), BF16)],
        compiler_params=_cparams(("parallel", "parallel", "parallel")),
        name="nsa_cmp_select",
    )(q2, kcmp, vcmp, ovl_t, small)


def _flash_init(rows):
    return jnp.full((rows, 1), NEG, F32), jnp.zeros((rows, LANES), F32)


def _flash_update(carry, s, v_tile):
    m, acc = carry
    half = s.shape[0] // 2
    m_new = jnp.maximum(m, jnp.max(s, axis=1, keepdims=True))
    p = jnp.exp2(s - m_new).astype(BF16)
    pv = jnp.concatenate([_dot(p[:half], v_tile[:, 0:LANES]), _dot(p[half:], v_tile[:, LANES:2 * LANES])], axis=0)
    return m_new, jnp.exp2(m - m_new) * acc + pv


def _pair_out(acc_even, acc_odd):
    even = acc_even / acc_even[:, HALF:HALF + 1]
    odd = acc_odd / acc_odd[:, 0:1]
    return jnp.where(_low_half(acc_even.shape), even, odd)


def _nsa_out(acc, gates):
    c = [acc[r * Q_BLOCK:(r + 1) * Q_BLOCK] for r in range(NSA_REP)]
    low = _low_half((Q_BLOCK, LANES))
    pair_a = _pair_out(c[0], c[2]) * jnp.where(low, gates[0], gates[2])
    pair_b = _pair_out(c[1], c[3]) * jnp.where(low, gates[1], gates[3])
    return jnp.concatenate([pair_a, pair_b], axis=1)


def _stacked_rows(rows, kb):
    row = lax.broadcasted_iota(jnp.int32, (rows, kb), 0) & (Q_BLOCK - 1)
    col = lax.broadcasted_iota(jnp.int32, (rows, kb), 1)
    return row, col


def _nsa_sel_kernel(kmax_ref, q_ref, msel_ref, k_ref, v_ref, sm_ref, o_ref, qa_ref, sa_ref, sb_ref, *, kb):
    g = pl.program_id(1)
    i = pl.program_id(2)
    q0 = i * Q_BLOCK
    rows = NSA_REP * Q_BLOCK
    msel = msel_ref[0, 0, 0]
    for r in range(NSA_REP):
        qa_ref[r * Q_BLOCK:(r + 1) * Q_BLOCK, :] = jnp.concatenate([q_ref[0, r], msel], axis=1)
    qa = qa_ref[...]
    last = (q0 + Q_BLOCK - 1) // kb

    def logits(kt):
        k0 = pl.multiple_of(kt * kb, kb)
        return _dot_nt(qa, k_ref[0, 0, pl.ds(k0, kb), :])

    def v_tile(kt):
        return v_ref[0, 0, pl.ds(pl.multiple_of(kt * kb, kb), kb), :]

    reach = _query_norm(qa) * kmax_ref[pl.program_id(0) * N_NSA_KV + g]
    slope = sum(qa[:, HEAD_DIM + 2 * j:HEAD_DIM + 2 * j + 1].astype(F32) for j in range(len(LOG2E_PARTS)))
    row, col = _stacked_rows(rows, kb)
    causal = (last * kb + col) <= (q0 + row)
    sa_ref[...] = jnp.where(causal, logits(last), NEG)

    def margin(top, m):
        last_pos = (jnp.maximum(top, 0) * kb + kb - 1).astype(F32)
        return jnp.max(reach + slope * last_pos - m)

    def cond(state):
        top, ahead, _ = state
        return (top >= 0) & (ahead > -SKIP_T2)

    def body(state):
        top, _, carry = state
        ahead = margin(top - 2, carry[0])
        lower = jnp.maximum(top - 1, 0)
        sb_ref[...] = logits(lower) + jnp.where(top >= 1, 0.0, NEG)
        carry = _flash_update(carry, sa_ref[...], v_tile(top))
        sa_ref[...] = logits(jnp.maximum(top - 2, 0))
        return top - 2, ahead, _flash_update(carry, sb_ref[...], v_tile(lower))

    _, _, (_, acc) = lax.while_loop(cond, body, (last, jnp.float32(0.0), _flash_init(rows)))
    o_ref[0] = _nsa_out(acc, _nsa_gates(sm_ref[0], 1, g))


def _nsa_sel(q2, msel, k2, v2, small, kmax, kb):
    B, H, S, qw = q2.shape
    G = N_NSA_KV
    nq = S // Q_BLOCK
    nb = msel.shape[-1]
    grid_spec = pltpu.PrefetchScalarGridSpec(
        num_scalar_prefetch=1,
        grid=(B, G, nq),
        in_specs=[pl.BlockSpec((1, NSA_REP, Q_BLOCK, qw), lambda b, g, i, km: (b, g, i, 0)),
                  pl.BlockSpec((1, 1, 1, Q_BLOCK, nb), lambda b, g, i, km: (b, g, i, 0, 0)),
                  pl.BlockSpec((1, 1, S, k2.shape[-1]), lambda b, g, i, km: (b, g, 0, 0)),
                  pl.BlockSpec((1, 1, S, 2 * LANES), lambda b, g, i, km: (b, g, 0, 0)),
                  pl.BlockSpec((1, Q_BLOCK, LANES), lambda b, g, i, km: (b, i, 0))],
        out_specs=pl.BlockSpec((1, Q_BLOCK, 2 * LANES), lambda b, g, i, km: (b, i, g)),
        scratch_shapes=[pltpu.VMEM((NSA_REP * Q_BLOCK, qw + nb), BF16),
                        pltpu.VMEM((NSA_REP * Q_BLOCK, kb), F32),
                        pltpu.VMEM((NSA_REP * Q_BLOCK, kb), F32)],
    )
    return pl.pallas_call(
        functools.partial(_nsa_sel_kernel, kb=kb),
        grid_spec=grid_spec,
        out_shape=jax.ShapeDtypeStruct((B, S, D_NSA), F32),
        compiler_params=_cparams(("parallel", "parallel", "parallel")),
        name="nsa_selected",
    )(kmax, q2, msel, k2, v2, small)


def _nsa_win_kernel(q_ref, k_ref, v_ref, sm_ref, o_ref):
    g = pl.program_id(1)
    i = pl.program_id(2)
    kb = Q_BLOCK
    rows = NSA_REP * Q_BLOCK
    qa = q_ref[0].reshape(rows, q_ref.shape[-1])
    n_back = WINDOW // kb
    row, col = _stacked_rows(rows, kb)
    carry = _flash_init(rows)
    for d in range(n_back + 1):
        kt = i - n_back + d
        k0 = pl.multiple_of(jnp.maximum(kt, 0) * kb, kb)
        s = _dot_nt(qa, k_ref[0, 0, pl.ds(k0, kb), :])
        if d == 0:
            s = jnp.where((col > row) & (kt >= 0), s, NEG)
        elif d == n_back:
            s = jnp.where(col <= row, s, NEG)
        else:
            s = jnp.where(kt >= 0, s, NEG)
        carry = _flash_update(carry, s, v_ref[0, 0, pl.ds(k0, kb), :])
    o_ref[0] = _nsa_out(carry[1], _nsa_gates(sm_ref[0], 2, g))


def _nsa_win(q2, k2, v2, small):
    B, H, S, qw = q2.shape
    G = N_NSA_KV
    nq = S // Q_BLOCK
    return pl.pallas_call(
        _nsa_win_kernel,
        grid=(B, G, nq),
        in_specs=[pl.BlockSpec((1, NSA_REP, Q_BLOCK, qw), lambda b, g, i: (b, g, i, 0)),
                  pl.BlockSpec((1, 1, S, qw), lambda b, g, i: (b, g, 0, 0)),
                  pl.BlockSpec((1, 1, S, 2 * LANES), lambda b, g, i: (b, g, 0, 0)),
                  pl.BlockSpec((1, Q_BLOCK, LANES), lambda b, g, i: (b, i, 0))],
        out_specs=pl.BlockSpec((1, Q_BLOCK, 2 * LANES), lambda b, g, i: (b, i, g)),
        out_shape=jax.ShapeDtypeStruct((B, S, D_NSA), F32),
        compiler_params=_cparams(("parallel", "parallel", "parallel")),
        name="nsa_window",
    )(q2, k2, v2, small)


def _fox_update(carry, s, v_tile):
    m, acc = carry
    m_new = jnp.maximum(m, jnp.max(s, axis=1, keepdims=True))
    p = jnp.exp2(s - m_new).astype(BF16)
    return m_new, jnp.exp2(m - m_new) * acc + _dot(p, v_tile)


def _query_norm(q):
    qf = q.astype(F32)
    lane = lax.broadcasted_iota(jnp.int32, qf.shape, 1)
    return jnp.sqrt(jnp.sum(jnp.where(lane < HEAD_DIM, qf * qf, 0.0), axis=1, keepdims=True))


def _fox_kernel(kmax_ref, tb_ref, q_ref, k_ref, v_ref, o_ref, *s_refs, tq, kb, n_heads, n_tiles):
    b = pl.program_id(0)
    hp = pl.program_id(1)
    i = pl.program_id(2)
    heads = [b * n_heads + hp * 2 + h for h in range(2)]

    def logits(h, kt):
        k0 = pl.multiple_of(kt * kb, kb)
        return _dot_nt(q_ref[0, h], k_ref[0, h, pl.ds(k0, kb), :])

    def v_tile(h, kt):
        return v_ref[0, h, pl.ds(pl.multiple_of(kt * kb, kb), kb), :]

    row = lax.broadcasted_iota(jnp.int32, (tq, kb), 0)
    col = lax.broadcasted_iota(jnp.int32, (tq, kb), 1)
    reach, carries = [], []
    for h in range(2):
        reach.append(_query_norm(q_ref[0, h]) * kmax_ref[heads[h]])
        c = _fox_update(_flash_init(tq), jnp.where(col <= row, logits(h, 2 * i), NEG), v_tile(h, 2 * i))
        carries.append(_fox_update(c, jnp.where(kb + col <= row, logits(h, 2 * i + 1), NEG), v_tile(h, 2 * i + 1)))
        s_refs[2 * h][...] = logits(h, jnp.maximum(2 * i - 1, 0))

    def margin(top, carries):
        t = jnp.maximum(top, 0)
        a = jnp.max(reach[0] + tb_ref[heads[0] * n_tiles + t] - carries[0][0])
        return jnp.maximum(a, jnp.max(reach[1] + tb_ref[heads[1] * n_tiles + t] - carries[1][0]))

    def cond(state):
        top, ahead, _ = state
        return (top >= 0) & (ahead > -SKIP_T2)

    def body(state):
        top, _, carries = state
        ahead = margin(top - 2, carries)
        out = []
        for h in range(2):
            sa_ref, sb_ref = s_refs[2 * h], s_refs[2 * h + 1]
            sb_ref[...] = logits(h, top - 1)
            c = _fox_update(carries[h], sa_ref[...], v_tile(h, top))
            sa_ref[...] = logits(h, jnp.maximum(top - 2, 0))
            out.append(_fox_update(c, sb_ref[...], v_tile(h, top - 1)))
        return top - 2, ahead, tuple(out)

    _, _, carries = lax.while_loop(cond, body, (2 * i - 1, margin(2 * i - 1, carries), tuple(carries)))
    o_ref[0] = _pair_out(carries[0][1], carries[1][1])


def _fox(q2, k2, v2, kmax, tile_bias):
    B, H, S, qw = q2.shape
    tq, kb = min(FOX_TQ, S), min(FOX_KB, S)
    assert tq == 2 * kb
    grid_spec = pltpu.PrefetchScalarGridSpec(
        num_scalar_prefetch=2,
        grid=(B, H // 2, S // tq),
        in_specs=[pl.BlockSpec((1, 2, tq, qw), lambda b, h, i, km, tb: (b, h, i, 0)),
                  pl.BlockSpec((1, 2, S, qw), lambda b, h, i, km, tb: (b, h, 0, 0)),
                  pl.BlockSpec((1, 2, S, LANES), lambda b, h, i, km, tb: (b, h, 0, 0))],
        out_specs=pl.BlockSpec((1, tq, LANES), lambda b, h, i, km, tb: (b, i, h)),
        scratch_shapes=[pltpu.VMEM((tq, kb), F32)] * 4,
    )
    return pl.pallas_call(
        functools.partial(_fox_kernel, tq=tq, kb=kb, n_heads=H, n_tiles=S // kb),
        grid_spec=grid_spec,
        out_shape=jax.ShapeDtypeStruct((B, S, D_FOX), F32),
        compiler_params=_cparams(("parallel", "parallel", "parallel")),
        name="fox_attention",
    )(kmax, tile_bias, q2, k2, v2)


def _logf_cumsum_kernel(x_ref, c_ref):
    x = x_ref[0]
    nr, nl = x.shape
    log_f = jnp.minimum(x, 0.0) - jnp.log1p(jnp.exp(-jnp.abs(x)))
    hp = lax.Precision.HIGHEST
    incl = (lax.broadcasted_iota(jnp.int32, (nl, nl), 0) <= lax.broadcasted_iota(jnp.int32, (nl, nl), 1))
    within = jnp.dot(log_f, incl.astype(F32), preferred_element_type=F32, precision=hp)
    totals = jnp.broadcast_to(within[:, nl - 1:nl], (nr, nl))
    before = (lax.broadcasted_iota(jnp.int32, (nr, nr), 1) < lax.broadcasted_iota(jnp.int32, (nr, nr), 0))
    c_ref[0] = within + jnp.dot(before.astype(F32), totals, preferred_element_type=F32, precision=hp)


def _logf_cumsum(logits):
    B, H, S = logits.shape
    x = logits.reshape(B * H, S // LANES, LANES)
    c = pl.pallas_call(
        _logf_cumsum_kernel,
        grid=(B * H,),
        in_specs=[pl.BlockSpec((1, S // LANES, LANES), lambda i: (i, 0, 0))],
        out_specs=pl.BlockSpec((1, S // LANES, LANES), lambda i: (i, 0, 0)),
        out_shape=jax.ShapeDtypeStruct(x.shape, F32),
        compiler_params=_cparams(("parallel",)),
        name="logf_cumsum",
    )(x)
    return c.reshape(B, H, S)


def _sb_kernel(q_ref, k_ref, v_ref, o_ref, *, tq, nh):
    i = pl.program_id(1)
    q0 = i * tq
    trow = q0 + lax.broadcasted_iota(jnp.int32, (tq, tq), 0)
    col = lax.broadcasted_iota(jnp.int32, (tq, tq), 1)
    rr = lax.broadcasted_iota(jnp.int32, (tq, tq), 0)
    upper = (rr > col).astype(BF16)

    def cond(state):
        kt, carries, _ = state
        alive = jnp.max(carries[0])
        for h in range(1, nh):
            alive = jnp.maximum(alive, jnp.max(carries[h]))
        return (kt >= 0) & (alive > SB_CUTOFF)

    def body(state):
        kt, carries, accs = state
        k0 = pl.multiple_of(kt * tq, tq)
        strict = (k0 + col) < trow
        new_c, new_a = [], []
        for h in range(nh):
            z = _dot_nt(q_ref[0, h], k_ref[0, h, pl.ds(k0, tq), :])
            log_beta = jnp.minimum(z, 0.0) - jnp.log1p(jnp.exp(-jnp.abs(z)))
            log_keep = jnp.where(strict, log_beta - z, 0.0)
            hi, lo = _split_bf16(log_keep)
            later = _dot(hi, upper) + _dot(lo, upper)
            a = jnp.where(strict, jnp.exp(log_beta + later + carries[h]), 0.0)
            new_a.append(accs[h] + _dot(a.astype(BF16), v_ref[0, h, pl.ds(k0, tq), :]))
            new_c.append(carries[h] + jnp.sum(log_keep, axis=1, keepdims=True))
        return kt - 1, tuple(new_c), tuple(new_a)

    state = (i, tuple(jnp.zeros((tq, 1), F32) for _ in range(nh)),
             tuple(jnp.zeros((tq, LANES), F32) for _ in range(nh)))
    _, _, accs = lax.while_loop(cond, body, state)
    o_ref[0] = jnp.concatenate([accs[2 * j] + accs[2 * j + 1] for j in range(nh // 2)], axis=1)


def _sb(q2, k2, v2):
    B, H, S, w = q2.shape
    tq = min(SB_TQ, S)
    return pl.pallas_call(
        functools.partial(_sb_kernel, tq=tq, nh=H),
        grid=(B, S // tq),
        in_specs=[pl.BlockSpec((1, H, tq, w), lambda b, i: (b, 0, i, 0)),
                  pl.BlockSpec((1, H, S, w), lambda b, i: (b, 0, 0, 0)),
                  pl.BlockSpec((1, H, S, w), lambda b, i: (b, 0, 0, 0))],
        out_specs=pl.BlockSpec((1, tq, D_SB), lambda b, i: (b, i, 0)),
        out_shape=jax.ShapeDtypeStruct((B, S, D_SB), F32),
        compiler_params=_cparams(("parallel", "parallel")),
        name="sb_attention",
    )(q2, k2, v2)


def _layer_norm(y, g, b):
    mu = jnp.mean(y, axis=1, keepdims=True)
    d = y - mu
    var = jnp.mean(d * d, axis=1, keepdims=True)
    return d * lax.rsqrt(var + LN_EPS) * g + b


def _rms(o, g):
    return o * lax.rsqrt(jnp.mean(o * o, axis=1, keepdims=True) + RMS_EPS) * g


def _post_kernel(oc_ref, osel_ref, ow_ref, of_ref, os_ref, x_ref, go_ref, wo_ref, lg_ref, lb_ref, rw_ref, rb_ref,
                 x1_ref, te_ref, tw_ref, *, alpha):
    o_nsa = oc_ref[...] + osel_ref[...] + ow_ref[...]
    n1 = _rms(o_nsa, go_ref[:, 0:D_NSA]).astype(BF16)
    n2 = _rms(of_ref[...], go_ref[:, D_NSA:D_NSA + D_FOX]).astype(BF16)
    n3 = _rms(os_ref[...], go_ref[:, D_NSA + D_FOX:]).astype(BF16)
    mix = (_dot(n1, wo_ref[0:D_NSA, :]) + _dot(n2, wo_ref[D_NSA:D_NSA + D_FOX, :])
           + _dot(n3, wo_ref[D_NSA + D_FOX:, :]))
    x1 = _layer_norm(alpha * x_ref[...] + mix, lg_ref[...], lb_ref[...])
    x1_ref[...] = x1
    logits = jnp.dot(x1, rw_ref[...], preferred_element_type=F32, precision=lax.Precision.HIGHEST) + rb_ref[...]
    tm, ne = logits.shape
    lane = lax.broadcasted_iota(jnp.int32, (tm, ne), 1).astype(F32)
    wide = lax.broadcasted_iota(jnp.int32, (tm, LANES), 1)
    top_e = jnp.zeros((tm, LANES), F32)
    top_l = jnp.full((tm, LANES), NEG, F32)
    cur = logits
    for k in range(TOP_K):
        mx = jnp.max(cur, axis=1, keepdims=True)
        idx = jnp.min(jnp.where(cur == mx, lane, float(ne)), axis=1, keepdims=True)
        top_e = jnp.where(wide == k, idx, top_e)
        top_l = jnp.where(wide == k, mx, top_l)
        cur = jnp.where(lane == idx, -BIG, cur)
    pe = jnp.exp(top_l - jnp.max(top_l, axis=1, keepdims=True))
    te_ref[...] = top_e.astype(jnp.int32)
    tw_ref[...] = pe / jnp.sum(pe, axis=1, keepdims=True)


def _post(o_c, o_s, o_w, o_fox, o_sb, x2d, g_out, w_out_bf, ln_g, ln_b, router_w, router_b, alpha):
    T, D = x2d.shape
    tm = POST_TM
    row = lambda w: pl.BlockSpec((tm, w), lambda i: (i, 0))
    full = lambda shape: pl.BlockSpec(shape, lambda i: (0, 0))
    return pl.pallas_call(
        functools.partial(_post_kernel, alpha=alpha),
        grid=(T // tm,),
        in_specs=[row(D_NSA), row(D_NSA), row(D_NSA), row(D_FOX), row(D_SB), row(D), full((1, D)), full((D, D)),
                  full((1, D)), full((1, D)), full((D, N_EXPERTS)), full((1, N_EXPERTS))],
        out_specs=[row(D), row(LANES), row(LANES)],
        out_shape=[jax.ShapeDtypeStruct((T, D), F32), jax.ShapeDtypeStruct((T, LANES), jnp.int32),
                   jax.ShapeDtypeStruct((T, LANES), F32)],
        compiler_params=_cparams(("parallel",)),
        name="outproj_ln_router",
    )(o_c, o_s, o_w, o_fox, o_sb, x2d, g_out.reshape(1, D), w_out_bf, ln_g.reshape(1, D), ln_b.reshape(1, D),
      router_w, router_b.reshape(1, N_EXPERTS))


def _row_gather_start(src_hbm, dst, sem, idx_ref, n_rows):
    def issue(r8, c):
        for u in range(8):
            r = r8 * 8 + u
            pltpu.make_async_copy(src_hbm.at[pl.ds(idx_ref[0, 0, r], 1)], dst.at[pl.ds(r, 1)],
                                  sem).start(priority=u % 2)
        return c
    lax.fori_loop(0, n_rows // 8, issue, 0)


def _row_gather_wait(src_hbm, dst, sem, n_rows):
    pltpu.make_async_copy(src_hbm.at[pl.ds(0, n_rows)], dst, sem).wait()


def _expert_kernel(blk_e_ref, nused_ref, idx_ref, idxn_ref, x_hbm, w_ref, wgu_ref, bgu_ref, wdn_ref, bdn_ref,
                   y_ref, xbuf, wgu_bf, wdn_bf, sem, *, tm, d_ff):
    i = pl.program_id(0)
    n_used = nused_ref[0]
    slot = i % 2

    @pl.when(i == 0)
    def _():
        _row_gather_start(x_hbm, xbuf.at[0], sem.at[0], idx_ref, tm)

    @pl.when(i <= n_used)
    def _():
        _row_gather_wait(x_hbm, xbuf.at[slot], sem.at[slot], tm)

    @pl.when(i < n_used)
    def _():
        changed = (i == 0) | (blk_e_ref[i] != blk_e_ref[jnp.maximum(i - 1, 0)])

        @pl.when(changed)
        def _():
            wgu_bf[...] = wgu_ref[0, 0].astype(BF16)
            wdn_bf[...] = wdn_ref[0, 0].astype(BF16)

        nxt = xbuf.at[1 - slot]
        for r in range(tm):
            pltpu.make_async_copy(x_hbm.at[pl.ds(idxn_ref[0, 0, r], 1)], nxt.at[pl.ds(r, 1)],
                                  sem.at[1 - slot]).start(priority=r % 2)
        xb = xbuf[slot].astype(BF16)
        gu = _dot(xb, wgu_bf[...]) + bgu_ref[0, 0]
        gate = jnp.minimum(gu[:, :d_ff], SWIGLU_LIMIT)
        up = jnp.clip(gu[:, d_ff:], -SWIGLU_LIMIT, SWIGLU_LIMIT)
        act = gate * _sigmoid(SWIGLU_ALPHA * gate) * (up + 1.0)
        y = _dot(act.astype(BF16), wdn_bf[...]) + bdn_ref[0, 0]
        y_ref[...] = y * w_ref[0]

    @pl.when(i >= n_used)
    def _():
        y_ref[...] = jnp.zeros_like(y_ref)


def _experts(x1, blk_e, n_used, buf_tok, buf_w, w_gu, b_gu, w_dn, b_dn, layer, tm):
    T, D = x1.shape
    nl, ne, _, d2 = w_gu.shape
    d_ff = d2 // 2
    n_blocks = blk_e.shape[0]
    idx = buf_tok.reshape(n_blocks, 1, tm)
    wcol = buf_w.reshape(n_blocks, tm, 1)
    last = n_blocks - 1
    grid_spec = pltpu.PrefetchScalarGridSpec(
        num_scalar_prefetch=2,
        grid=(n_blocks,),
        in_specs=[pl.BlockSpec((1, 1, tm), lambda i, e, n: (i, 0, 0), memory_space=pltpu.SMEM),
                  pl.BlockSpec((1, 1, tm), lambda i, e, n: (jnp.minimum(i + 1, last), 0, 0),
                               memory_space=pltpu.SMEM),
                  pl.BlockSpec(memory_space=pl.ANY),
                  pl.BlockSpec((1, tm, 1), lambda i, e, n: (i, 0, 0)),
                  pl.BlockSpec((1, 1, D, d2), lambda i, e, n: (layer, e[i], 0, 0)),
                  pl.BlockSpec((1, 1, 1, d2), lambda i, e, n: (layer, e[i], 0, 0)),
                  pl.BlockSpec((1, 1, d_ff, D), lambda i, e, n: (layer, e[i], 0, 0)),
                  pl.BlockSpec((1, 1, 1, D), lambda i, e, n: (layer, e[i], 0, 0))],
        out_specs=pl.BlockSpec((tm, D), lambda i, e, n: (i, 0)),
        scratch_shapes=[pltpu.VMEM((2, tm, D), F32), pltpu.VMEM((D, d2), BF16), pltpu.VMEM((d_ff, D), BF16),
                        pltpu.SemaphoreType.DMA((2,))],
    )
    return pl.pallas_call(
        functools.partial(_expert_kernel, tm=tm, d_ff=d_ff),
        grid_spec=grid_spec,
        out_shape=jax.ShapeDtypeStruct((n_blocks * tm, D), F32),
        compiler_params=_cparams(("arbitrary",), 58 * 1024 * 1024),
        name="moe_experts",
    )(blk_e, n_used, idx, idx, x1, wcol, w_gu, b_gu.reshape(nl, ne, 1, d2), w_dn, b_dn.reshape(nl, ne, 1, D))


def _combine_kernel(idx_ref, idxn_ref, y_hbm, x_ref, lg_ref, lb_ref, o_ref, ybuf, sem, *, tm, alpha):
    i = pl.program_id(0)
    n = pl.num_programs(0)
    slot = i % 2
    rows = TOP_K * tm

    @pl.when(i == 0)
    def _():
        _row_gather_start(y_hbm, ybuf.at[0], sem.at[0], idx_ref, rows)

    @pl.when(i + 1 < n)
    def _():
        nxt = ybuf.at[1 - slot]
        for r in range(rows):
            pltpu.make_async_copy(y_hbm.at[pl.ds(idxn_ref[0, 0, r], 1)], nxt.at[pl.ds(r, 1)],
                                  sem.at[1 - slot]).start(priority=r % 2)

    _row_gather_wait(y_hbm, ybuf.at[slot], sem.at[slot], rows)
    ffn = ybuf[slot, 0:tm]
    for k in range(1, TOP_K):
        ffn = ffn + ybuf[slot, k * tm:(k + 1) * tm]
    o_ref[...] = _layer_norm(alpha * x_ref[...] + ffn, lg_ref[...], lb_ref[...])


def _combine(y_sorted, pos, x1, ln_g, ln_b, alpha):
    T, D = x1.shape
    tm = COMBINE_TM
    nt = T // tm
    idx = pos.reshape(nt, tm, TOP_K).transpose(0, 2, 1).reshape(nt, 1, TOP_K * tm)
    last = nt - 1
    return pl.pallas_call(
        functools.partial(_combine_kernel, tm=tm, alpha=alpha),
        grid=(nt,),
        in_specs=[pl.BlockSpec((1, 1, TOP_K * tm), lambda i: (i, 0, 0), memory_space=pltpu.SMEM),
                  pl.BlockSpec((1, 1, TOP_K * tm), lambda i: (jnp.minimum(i + 1, last), 0, 0),
                               memory_space=pltpu.SMEM),
                  pl.BlockSpec(memory_space=pl.ANY),
                  pl.BlockSpec((tm, D), lambda i: (i, 0)),
                  pl.BlockSpec((1, D), lambda i: (0, 0)),
                  pl.BlockSpec((1, D), lambda i: (0, 0))],
        out_specs=pl.BlockSpec((tm, D), lambda i: (i, 0)),
        out_shape=jax.ShapeDtypeStruct((T, D), F32),
        scratch_shapes=[pltpu.VMEM((2, TOP_K * tm, D), F32), pltpu.SemaphoreType.DMA((2,))],
        compiler_params=_cparams(("arbitrary",)),
        name="moe_combine_ln",
    )(idx, idx, y_sorted, x1, ln_g.reshape(1, D), ln_b.reshape(1, D))


def _route(top_e, top_w, tm):
    T = top_e.shape[0]
    M = T * TOP_K
    i32 = jnp.int32
    flat_e = top_e.reshape(M).astype(i32)
    flat_w = top_w.reshape(M)
    ar = jnp.arange(M, dtype=i32)
    experts = jnp.arange(N_EXPERTS, dtype=i32)
    skey, sorted_w = lax.sort((flat_e * M + ar, flat_w), num_keys=1)
    sorted_e = skey // M
    order = skey - sorted_e * M
    counts = jnp.sum((flat_e[:, None] == experts[None, :]).astype(i32), axis=0)
    start = jnp.cumsum(counts) - counts
    padded = (counts + tm - 1) // tm * tm
    pad_end = jnp.cumsum(padded)
    pad_start = pad_end - padded
    dest = pad_start[sorted_e] + ar - start[sorted_e]
    n_blocks = -(-M // tm) + N_EXPERTS
    blk_first = jnp.arange(n_blocks, dtype=i32) * tm
    blk_e = jnp.minimum(jnp.sum((pad_end[None, :] <= blk_first[:, None]).astype(i32), axis=1), N_EXPERTS - 1)
    within = (blk_first - pad_start[blk_e])[:, None] + jnp.arange(tm, dtype=i32)[None, :]
    valid = within < counts[blk_e][:, None]
    src_row = jnp.clip(start[blk_e][:, None] + within, 0, M - 1)
    buf_tok = jnp.where(valid, order[src_row] // TOP_K, 0).reshape(n_blocks * tm)
    buf_w = jnp.where(valid, sorted_w[src_row], 0.0).reshape(n_blocks * tm)
    _, pos = lax.sort((order, dest), num_keys=1)
    n_used = (pad_end[-1] // tm).astype(i32).reshape(1)
    return blk_e, n_used, buf_tok, buf_w, pos


class _Plan:
    def __init__(self):
        self.src, self.scale, self.const, self.outs = [], [], [], []

    def group(self, width, src_cols=(), at=0, scale=1.0, ones=()):
        src = np.full((width,), -1, np.int64)
        sc = np.zeros((width,), np.float32)
        const = np.zeros((width,), np.float32)
        src[at:at + len(src_cols)] = src_cols
        sc[at:at + len(src_cols)] = scale
        for col, val in ones:
            const[col] = val
        self.src.append(src)
        self.scale.append(sc)
        self.const.append(const)

    def out(self, n_heads, width, dtype):
        self.outs.append((n_heads, width, dtype))


def _cols(start):
    return np.arange(start, start + HEAD_DIM)


def _value_group(plan, src, odd):
    if odd:
        plan.group(LANES, _cols(src), at=HALF, ones=[(0, 1.0)])
    else:
        plan.group(LANES, _cols(src), ones=[(HALF, 1.0)])


def _plans(S):
    qs = HEAD_DIM ** -0.5
    slope = 2.0 ** (-8.0 * np.arange(1, N_NSA_HEADS + 1) / N_NSA_HEADS)
    a = _Plan()
    for g in range(N_NSA_KV):
        for hl in NSA_ROW_HEADS:
            h = g * NSA_REP + hl
            a.group(LANES, _cols(h * HEAD_DIM), scale=qs * LOG2E,
                    ones=[(HALF + 2 * j + u, slope[h] * part) for j, part in enumerate(LOG2E_PARTS) for u in range(2)])
    a.out(N_NSA_HEADS, LANES, BF16)
    kv = lambda branch, which, g: SRC_KV + branch * 256 + which * 128 + g * HEAD_DIM
    for which in range(2):
        for g in range(N_NSA_KV):
            a.group(LANES, _cols(kv(0, which, g)))
    a.out(2 * N_NSA_KV, LANES, F32)
    for g in range(N_NSA_KV):
        a.group(2 * LANES, _cols(kv(1, 0, g)))
    a.out(N_NSA_KV, 2 * LANES, BF16)
    for g in range(N_NSA_KV):
        _value_group(a, kv(1, 1, g), False)
        _value_group(a, kv(1, 1, g), True)
    a.out(N_NSA_KV, 2 * LANES, BF16)
    for g in range(N_NSA_KV):
        a.group(LANES, _cols(kv(2, 0, g)))
    a.out(N_NSA_KV, LANES, BF16)
    for g in range(N_NSA_KV):
        _value_group(a, kv(2, 1, g), False)
        _value_group(a, kv(2, 1, g), True)
    a.out(N_NSA_KV, 2 * LANES, BF16)
    a.group(LANES, np.concatenate([np.arange(SRC_GATE, SRC_GATE + 24), np.arange(SRC_LOGF, SRC_LOGF + 4)]))
    a.out(1, LANES, F32)

    n_a = sum(len(s) for s in a.src)
    extra = np.zeros((S, n_a), np.float32)
    pos = np.arange(S)
    pos_hi, pos_lo = pos // SEL_LEN * SEL_LEN, pos % SEL_LEN
    off_ksel = N_NSA_HEADS * LANES + 2 * N_NSA_KV * LANES
    off_kwin = off_ksel + N_NSA_KV * 2 * LANES + N_NSA_KV * 2 * LANES
    nb = S // SEL_LEN
    for g in range(N_NSA_KV):
        o = off_ksel + g * 2 * LANES
        extra[pos, o + LANES + pos // SEL_LEN] = 1.0
        for base in (o, off_kwin + g * LANES):
            for j in range(len(LOG2E_PARTS)):
                extra[:, base + HALF + 2 * j], extra[:, base + HALF + 2 * j + 1] = pos_hi, pos_lo
    assert nb <= LANES

    b = _Plan()
    fox = lambda which, h: SRC_FOX + which * 256 + h * HEAD_DIM
    sb = lambda which, h: SRC_SB + which * 256 + h * HEAD_DIM
    for h in range(N_FOX_HEADS):
        b.group(LANES, _cols(fox(0, h)), scale=qs * LOG2E, ones=[(HALF, 1.0), (HALF + 1, 1.0), (HALF + 2, 1.0)])
    b.out(N_FOX_HEADS, LANES, BF16)
    for h in range(N_FOX_HEADS):
        b.group(LANES, _cols(fox(1, h)))
    b.out(N_FOX_HEADS, LANES, BF16)
    for h in range(N_FOX_HEADS):
        _value_group(b, fox(2, h), h % 2 == 1)
    b.out(N_FOX_HEADS, LANES, BF16)
    for h in range(N_SB_HEADS):
        b.group(LANES, _cols(sb(0, h)), scale=qs)
    b.out(N_SB_HEADS, LANES, BF16)
    for h in range(N_SB_HEADS):
        b.group(LANES, _cols(sb(1, h)))
    b.out(N_SB_HEADS, LANES, BF16)
    for h in range(N_SB_HEADS):
        b.group(LANES, _cols(sb(2, h)), at=HALF if h % 2 else 0)
    b.out(N_SB_HEADS, LANES, BF16)
    return a, jnp.asarray(extra, dtype=BF16), b


def _constants(S):
    plan_a, extra_a, plan_b = _plans(S)
    nc = S // CMP_STRIDE
    nb = S // SEL_LEN
    n_cmp = (S - CMP_LEN) // CMP_STRIDE + 1
    cmp_idx = np.arange(nc)[:, None] * CMP_STRIDE + np.arange(CMP_LEN)[None, :]
    ovl = (cmp_idx[:, :, None] // SEL_LEN == np.arange(nb)[None, None, :]).astype(np.float32).mean(axis=1)
    ovl[n_cmp:] = 0.0
    ovl = np.pad(ovl, ((0, 0), (0, LANES - nb)))
    cmp_end = np.arange(nc) * CMP_STRIDE + CMP_LEN - 1
    kcmp_aug = np.zeros((nc, LANES), np.float32)
    for j in range(len(LOG2E_PARTS)):
        kcmp_aug[:, HALF + 2 * j] = cmp_end // SEL_LEN * SEL_LEN
        kcmp_aug[:, HALF + 2 * j + 1] = cmp_end % SEL_LEN
    return dict(a=plan_a, a_outs=tuple(plan_a.outs), extra_a=extra_a, b=plan_b, b_outs=tuple(plan_b.outs),
                ovl_t=jnp.asarray(ovl.T, dtype=BF16), kcmp_aug=jnp.asarray(kcmp_aug))


def _plan_weights(w_in, b_in, plan):
    src = np.concatenate(plan.src)
    scale = np.concatenate(plan.scale)
    const = jnp.asarray(np.concatenate(plan.const))
    wb = jnp.concatenate([w_in, b_in[None, :]], axis=0)
    pieces, c = [], 0
    while c < len(src):
        e = c + 1
        if src[c] < 0:
            while e < len(src) and src[e] < 0:
                e += 1
            pieces.append(jnp.zeros((wb.shape[0], e - c), F32))
        else:
            while e < len(src) and src[e] == src[e - 1] + 1 and scale[e] == scale[c]:
                e += 1
            pieces.append(wb[:, src[c]:src[c] + e - c] * float(scale[c]))
        c = e
    wb_p = jnp.concatenate(pieces, axis=1)
    return wb_p[:-1].astype(BF16), (wb_p[-1] + const).reshape(1, -1)


def _key_norm_max(k2):
    kf = k2[..., 0:HEAD_DIM].astype(F32)
    return jnp.sqrt(jnp.max(jnp.sum(kf * kf, axis=-1), axis=-1)).reshape(-1)


def _layer(x, cs, w_in, b_in, pos_k, pos_v, w1k, w2k, w1v, w2v, g_out, w_out, ln1_g, ln1_b,
           router_w, router_b, w_gu, b_gu, w_dn, b_dn, ln2_g, ln2_b, alpha, layer):
    B, S, D = x.shape
    T = B * S
    x2d = x.reshape(T, D)
    wa, ba = _plan_weights(w_in, b_in, cs["a"])
    wb, bb = _plan_weights(w_in, b_in, cs["b"])
    q2, kvc, ksel, vsel, kwin, vwin, small = _proj(x2d, wa, ba, cs["extra_a"], cs["a_outs"], B, S)
    fq, fk, fv, sq, sk, sv = _proj(x2d, wb, bb, None, cs["b_outs"], B, S)
    small = small.reshape(B, S, LANES)

    kc = kvc[:, 0:N_NSA_KV, :, 0:HEAD_DIM]
    vc = kvc[:, N_NSA_KV:, :, 0:HEAD_DIM]
    k_cmp, v_cmp = _compress(kc, vc, pos_k, pos_v, w1k, w2k, w1v, w2v, cs["kcmp_aug"])
    o_c, msel = _nsa_cmp(q2, k_cmp, v_cmp, cs["ovl_t"], small)
    o_s = _nsa_sel(q2, msel, ksel, vsel, small, _key_norm_max(ksel), min(SEL_KB, S))
    o_w = _nsa_win(q2, kwin, vwin, small)

    c = _logf_cumsum(small[..., 24:24 + N_FOX_HEADS].transpose(0, 2, 1))
    c = c * LOG2E
    c_hi = _trunc_bf16(c)
    c_mid = _trunc_bf16(c - c_hi)
    c_lo = c - c_hi - c_mid
    c_aug = jnp.pad(-jnp.stack([c_hi, c_mid, c_lo], axis=-1), ((0, 0), (0, 0), (0, 0), (HALF, HALF - 3)))
    kb = min(FOX_KB, S)
    tile_bias = lax.cummax(jnp.max((-c).reshape(B, N_FOX_HEADS, S // kb, kb), axis=-1), axis=2)
    o_fox = _fox(fq, fk + c_aug.astype(BF16), fv, _key_norm_max(fk), tile_bias.reshape(-1))
    o_sb = _sb(sq, sk, sv)

    flat = lambda o: o.reshape(T, o.shape[-1])
    x1, te, tw = _post(flat(o_c), flat(o_s), flat(o_w), flat(o_fox), flat(o_sb), x2d, g_out, w_out.astype(BF16),
                       ln1_g, ln1_b, router_w, router_b, alpha)
    blk_e, n_used, buf_tok, buf_w, pos = _route(te[:, :TOP_K], tw[:, :TOP_K], MOE_TM)
    y_sorted = _experts(x1, blk_e, n_used, buf_tok, buf_w, w_gu, b_gu, w_dn, b_dn, layer, MOE_TM)
    out = _combine(y_sorted, pos, x1, ln2_g, ln2_b, alpha)
    return out.reshape(B, S, D)


def kernel(x, w_in, b_in, cmp_pos_k, cmp_pos_v, cmp_w1_k, cmp_w2_k, cmp_w1_v, cmp_w2_v, g_out, w_out,
           ln1_g, ln1_b, router_w, router_b, w_gate_up, b_gate_up, w_down, b_down, ln2_g, ln2_b):
    depth = w_in.shape[0]
    alpha = (2 * depth) ** 0.25
    consts = _constants(x.shape[1])
    for l in range(depth):
        x = _layer(x, consts, w_in[l], b_in[l], cmp_pos_k[l], cmp_pos_v[l], cmp_w1_k[l], cmp_w2_k[l],
                   cmp_w1_v[l], cmp_w2_v[l], g_out[l], w_out[l], ln1_g[l], ln1_b[l], router_w[l], router_b[l],
                   w_gate_up, b_gate_up, w_down, b_down, ln2_g[l], ln2_b[l], alpha, l)
    return x
```
